```python
import math
import jax, jax.numpy as jnp
from jax import lax
import numpy as np

D_MODEL = 2048
BATCH = 4
SEQ = 2048
DEPTH = 4
DEC_BATCH = 32
DEC_SEQ = 4
PAST_LEN = 16384
PAGE_SIZE = 128

N_MIXERS = 2
N_A_LAYERS = (DEPTH + 1) // 2
N_B_LAYERS = DEPTH // 2
HG_HEADS = 16
HG_DK = 128
HG_DV = D_MODEL // HG_HEADS
HG_HK = HG_HEADS * HG_DK
HG_HV = HG_HEADS * HG_DV
HG_IN = 2 * HG_HK + HG_HV + D_MODEL
HG_CHUNK = 64
N_Q_HEADS = 32
N_KV_HEADS = 8
HEAD_DIM = D_MODEL // N_Q_HEADS
GQA_GROUP = N_Q_HEADS // N_KV_HEADS
ATTN_IN = (N_Q_HEADS + 2 * N_KV_HEADS) * HEAD_DIM
WINDOW = 128
ROPE_THETA = 10000.0
D_FF = 5632
NORM_EPS = 1e-6
N_NORMS = 6

kernel_name = 'hgrn2_swa_sink_macaron_decode_step'

F32 = jnp.float32


def rms_norm(x, g):
    xf = x.astype(F32)
    y = xf * lax.rsqrt(jnp.mean(xf * xf, axis=-1, keepdims=True) + NORM_EPS)
    return (y * g.astype(F32)).astype(x.dtype)


def swiglu(h, w_in, w_out):
    gate, up = jnp.split(h @ w_in, 2, axis=-1)
    return (jax.nn.silu(gate) * up) @ w_out


def macaron_half(x, g_pre, g_post, w_in, w_out):
    return x + 0.5 * rms_norm(swiglu(rms_norm(x, g_pre), w_in, w_out), g_post)


def rope(x, pos):
    half = HEAD_DIM // 2
    inv = jnp.power(ROPE_THETA, -jnp.arange(half, dtype=F32) * (2.0 / HEAD_DIM))
    ang = pos.astype(F32)[:, None] * inv[None, :]
    cos = jnp.cos(ang)[:, None, :]
    sin = jnp.sin(ang)[:, None, :]
    xf = x.astype(F32)
    x1, x2 = xf[..., :half], xf[..., half:]
    return jnp.concatenate([x1 * cos - x2 * sin, x2 * cos + x1 * sin], axis=-1).astype(x.dtype)


def attn_project(h, w_in, pos):
    B, L, _ = h.shape
    q, k, v = jnp.split(h @ w_in, [N_Q_HEADS * HEAD_DIM, (N_Q_HEADS + N_KV_HEADS) * HEAD_DIM], axis=-1)
    q = rope(q.reshape(B, L, N_Q_HEADS, HEAD_DIM), pos)
    k = rope(k.reshape(B, L, N_KV_HEADS, HEAD_DIM), pos)
    v = v.reshape(B, L, N_KV_HEADS, HEAD_DIM)
    return q, k, v


def sink_attend(q, k, v, mask, sinks):
    s = jnp.einsum('bnqkgd,bnskd->bnkgqs', q, k).astype(F32) * (HEAD_DIM ** -0.5)
    s = jnp.where(mask[None, :, None, None], s, -jnp.inf)
    sink = sinks.astype(F32).reshape(1, 1, N_KV_HEADS, GQA_GROUP, 1, 1)
    m = jnp.maximum(jnp.max(s, axis=-1, keepdims=True), sink)
    p = jnp.exp(s - m)
    denom = jnp.sum(p, axis=-1, keepdims=True) + jnp.exp(sink - m)
    return jnp.einsum('bnkgqs,bnskd->bnqkgd', (p / denom).astype(v.dtype), v)


def swa_prompt(h, w_in, sinks, w_out):
    B, L, _ = h.shape
    buf = min(WINDOW, PAST_LEN)
    pos = jnp.arange(L, dtype=jnp.int32)
    q, k, v = attn_project(h, w_in, pos)
    nb = L // WINDOW
    qb = q.reshape(B, nb, WINDOW, N_KV_HEADS, GQA_GROUP, HEAD_DIM)
    zpad = jnp.zeros((B, WINDOW, N_KV_HEADS, HEAD_DIM), k.dtype)
    kb = jnp.concatenate([zpad, k], axis=1).reshape(B, nb + 1, WINDOW, N_KV_HEADS, HEAD_DIM)
    vb = jnp.concatenate([zpad, v], axis=1).reshape(B, nb + 1, WINDOW, N_KV_HEADS, HEAD_DIM)
    kk = jnp.concatenate([kb[:, :-1], kb[:, 1:]], axis=2)
    vv = jnp.concatenate([vb[:, :-1], vb[:, 1:]], axis=2)
    blk = jnp.arange(nb)[:, None, None]
    qpos = blk * WINDOW + jnp.arange(WINDOW)[None, :, None]
    kpos = (blk - 1) * WINDOW + jnp.arange(2 * WINDOW)[None, None, :]
    mask = (kpos >= 0) & (kpos <= qpos) & (kpos >= qpos - WINDOW)
    o = sink_attend(qb, kk, vv, mask, sinks).reshape(B, L, N_Q_HEADS * HEAD_DIM)
    return o @ w_out, k[:, L - buf:], v[:, L - buf:]


def swa_sample(h, k_buf, v_buf, w_in, sinks, w_out):
    B, L, _ = h.shape
    buf = k_buf.shape[1]
    pos = PAST_LEN + jnp.arange(L, dtype=jnp.int32)
    q, k, v = attn_project(h, w_in, pos)
    kk = jnp.concatenate([k_buf.astype(k.dtype), k], axis=1)
    vv = jnp.concatenate([v_buf.astype(v.dtype), v], axis=1)
    qpos = pos[:, None]
    kpos = (PAST_LEN - buf + jnp.arange(buf + L, dtype=jnp.int32))[None, :]
    mask = ((kpos <= qpos) & (kpos >= qpos - WINDOW))[None]
    qb = q.reshape(B, 1, L, N_KV_HEADS, GQA_GROUP, HEAD_DIM)
    o = sink_attend(qb, kk[:, None], vv[:, None], mask, sinks).reshape(B, L, N_Q_HEADS * HEAD_DIM)
    return o @ w_out, kk[:, -buf:], vv[:, -buf:]


def hgrn2_scan(q, log_f, k, v, S0):
    B, L, H, _ = q.shape
    C = min(HG_CHUNK, L)
    n = -(-L // C)
    pad = n * C - L

    def prep(a):
        a = jnp.pad(a.astype(F32), ((0, 0), (0, pad), (0, 0), (0, 0)))
        return a.reshape(B, n, C, H, a.shape[-1]).transpose(1, 0, 3, 2, 4)

    causal = jnp.tril(jnp.ones((C, C), dtype=bool))[None, None, :, :, None]

    def step(S, xs):
        qc, gc, kc, vc = xs
        b = jnp.cumsum(gc, axis=2)
        diff = b[:, :, :, None, :] - b[:, :, None, :, :]
        decay = jnp.exp(jnp.where(causal, diff, -jnp.inf))
        A = jnp.einsum('bhtk,bhtsk,bhsk->bhts', qc, decay, kc)
        o = jnp.einsum('bhts,bhsv->bhtv', A, vc) + jnp.einsum('bhtk,bhkv->bhtv', qc * jnp.exp(b), S)
        bl = b[:, :, -1:, :]
        S = jnp.exp(bl[:, :, 0, :])[..., None] * S + jnp.einsum('bhsk,bhsv->bhkv', kc * jnp.exp(bl - b), vc)
        return S, o

    S, o = lax.scan(step, S0.astype(F32), (prep(q), prep(log_f), prep(k), prep(v)))
    o = o.transpose(1, 0, 3, 2, 4).reshape(B, n * C, H, HG_DV)[:, :L]
    return o, S


def hgrn2_mixer(h, S0, w_in, lb, norm_g, w_out):
    B, L, _ = h.shape
    qz, fz, iz, gz = jnp.split(h @ w_in, [HG_HK, 2 * HG_HK, 2 * HG_HK + HG_HV], axis=-1)
    fzf = fz.astype(F32)
    lbf = lb.astype(F32)
    log_f = jnp.logaddexp(jnp.log(lbf), jnp.log1p(-lbf) + jax.nn.log_sigmoid(fzf))
    key = (1.0 - lbf) * jax.nn.sigmoid(-fzf)
    shp = (B, L, HG_HEADS, HG_DK)
    o, S = hgrn2_scan(qz.reshape(shp), log_f.reshape(shp), key.reshape(shp),
                      iz.reshape(B, L, HG_HEADS, HG_DV), S0)
    o = o * lax.rsqrt(jnp.mean(o * o, axis=-1, keepdims=True) + NORM_EPS) * norm_g.astype(F32).reshape(HG_HEADS, HG_DV)
    o = o.astype(h.dtype) * jax.nn.silu(gz).reshape(B, L, HG_HEADS, HG_DV)
    return o.reshape(B, L, HG_HV) @ w_out, S


def setup_inputs(seed: int = 0) -> dict:
    key = jax.random.key(seed)
    ks = jax.random.split(key, 16)
    buf = min(WINDOW, PAST_LEN)
    nrm = jax.random.normal
    return {
        'x_prompt': nrm(ks[0], (BATCH, SEQ, D_MODEL), F32),
        'x_sample': nrm(ks[1], (DEC_BATCH, DEC_SEQ, D_MODEL), F32),
        'state_hgrn': 0.5 * nrm(ks[2], (N_A_LAYERS, DEC_BATCH, HG_HEADS, HG_DK, HG_DV), F32),
        'cache_k_win': nrm(ks[3], (N_B_LAYERS, DEC_BATCH, buf, N_KV_HEADS, HEAD_DIM), F32),
        'cache_v_win': nrm(ks[4], (N_B_LAYERS, DEC_BATCH, buf, N_KV_HEADS, HEAD_DIM), F32),
        'norm_gains': 1.0 + 0.05 * nrm(ks[5], (DEPTH, N_NORMS, D_MODEL), F32),
        'w_ffn_in': nrm(ks[6], (DEPTH, 2, D_MODEL, 2 * D_FF), F32) * D_MODEL ** -0.5,
        'w_ffn_out': nrm(ks[7], (DEPTH, 2, D_FF, D_MODEL), F32) * D_FF ** -0.5,
        'w_hgrn_in': nrm(ks[8], (N_A_LAYERS, D_MODEL, HG_IN), F32) * D_MODEL ** -0.5,
        'hgrn_lb_logits': nrm(ks[9], (N_A_LAYERS, HG_HK), F32),
        'hgrn_norm_gain': 1.0 + 0.05 * nrm(ks[10], (N_A_LAYERS, HG_HV), F32),
        'w_hgrn_out': nrm(ks[11], (N_A_LAYERS, HG_HV, D_MODEL), F32) * HG_HV ** -0.5,
        'w_attn_in': nrm(ks[12], (N_B_LAYERS, D_MODEL, ATTN_IN), F32) * D_MODEL ** -0.5,
        'attn_sinks': nrm(ks[13], (N_B_LAYERS, N_Q_HEADS), F32),
        'w_attn_out': nrm(ks[14], (N_B_LAYERS, N_Q_HEADS * HEAD_DIM, D_MODEL), F32) * (N_Q_HEADS * HEAD_DIM) ** -0.5,
    }


def reference(x_prompt, x_sample, state_hgrn, cache_k_win, cache_v_win, norm_gains, w_ffn_in, w_ffn_out,
              w_hgrn_in, hgrn_lb_logits, hgrn_norm_gain, w_hgrn_out, w_attn_in, attn_sinks, w_attn_out):
    lb_all = jnp.cumsum(jax.nn.softmax(hgrn_lb_logits.astype(F32), axis=0), axis=0)
    lb_all = lb_all - lb_all[0:1]
    yp, ys = x_prompt, x_sample
    hg_p, hg_s, kp_l, vp_l, ks_l, vs_l = [], [], [], [], [], []
    for layer in range(DEPTH):
        g = norm_gains[layer]
        yp = macaron_half(yp, g[0], g[1], w_ffn_in[layer, 0], w_ffn_out[layer, 0])
        ys = macaron_half(ys, g[0], g[1], w_ffn_in[layer, 0], w_ffn_out[layer, 0])
        hp = rms_norm(yp, g[2])
        hs = rms_norm(ys, g[2])
        if layer % N_MIXERS == 0:
            a = layer // N_MIXERS
            S0p = jnp.zeros((hp.shape[0], HG_HEADS, HG_DK, HG_DV), F32)
            op, Sp = hgrn2_mixer(hp, S0p, w_hgrn_in[a], lb_all[a], hgrn_norm_gain[a], w_hgrn_out[a])
            os_, Ss = hgrn2_mixer(hs, state_hgrn[a], w_hgrn_in[a], lb_all[a], hgrn_norm_gain[a], w_hgrn_out[a])
            hg_p.append(Sp.astype(state_hgrn.dtype))
            hg_s.append(Ss.astype(state_hgrn.dtype))
        else:
            b = layer // N_MIXERS
            op, kp, vp = swa_prompt(hp, w_attn_in[b], attn_sinks[b], w_attn_out[b])
            os_, kn, vn = swa_sample(hs, cache_k_win[b], cache_v_win[b], w_attn_in[b], attn_sinks[b], w_attn_out[b])
            kp_l.append(kp.astype(cache_k_win.dtype))
            vp_l.append(vp.astype(cache_v_win.dtype))
            ks_l.append(kn.astype(cache_k_win.dtype))
            vs_l.append(vn.astype(cache_v_win.dtype))
        yp = yp + rms_norm(op, g[3])
        ys = ys + rms_norm(os_, g[3])
        yp = macaron_half(yp, g[4], g[5], w_ffn_in[layer, 1], w_ffn_out[layer, 1])
        ys = macaron_half(ys, g[4], g[5], w_ffn_in[layer, 1], w_ffn_out[layer, 1])
    return (yp, ys, jnp.stack(hg_p), jnp.stack(hg_s), jnp.stack(kp_l), jnp.stack(vp_l), jnp.stack(ks_l), jnp.stack(vs_l))
```

```python
import functools
import math

import numpy as np
import jax
import jax.numpy as jnp
from jax import lax
from jax.experimental import pallas as pl
from jax.experimental.pallas import tpu as pltpu

F32 = jnp.float32
BF16 = jnp.bfloat16

NORM_EPS = 1e-6
HEAD_DIM = 64
HG_DK = 128
HG_DV = 128
HG_CHUNK = 64
WINDOW = 128
PAST_LEN = 16384
ROPE_THETA = 10000.0
N_MIXERS = 2

LANES = 128
VMEM_BYTES_V7X = 64 * 1024 * 1024
VMEM_HEADROOM = 6 * 1024 * 1024

FFN_TM, FFN_TF = 1040, 256
PROJ_TM, PROJ_TN = 1040, 512
OUT_TM, OUT_TK = 640, 512
SAMPLE_PAD = 16


def _vmem_limit(nbytes):
    return int(min(VMEM_BYTES_V7X - VMEM_HEADROOM, max(nbytes * 5 // 4, 16 * 1024 * 1024)))


def _rms(x, g):
    return x * lax.rsqrt(jnp.mean(x * x, axis=-1, keepdims=True) + NORM_EPS) * g


def _dot(a, b):
    return jnp.dot(a, b, preferred_element_type=F32)


def _dot_nt(a, b):
    return lax.dot_general(a, b, (((1,), (1,)), ((), ())), preferred_element_type=F32)


def _dot_tn(a, b):
    return lax.dot_general(a, b, (((0,), (0,)), ((), ())), preferred_element_type=F32)


def _ffn_kernel(x_ref, gpre_ref, gpost_ref, wg_ref, wu_ref, wo_ref, o_ref, xn_ref, acc_ref, *, nj):
    j = pl.program_id(1)

    @pl.when(j == 0)
    def _():
        xn_ref[...] = _rms(x_ref[...], gpre_ref[...]).astype(BF16)
        acc_ref[...] = jnp.zeros_like(acc_ref)

    xn = xn_ref[...]
    hg = _dot(xn, wg_ref[...].astype(BF16))
    hu = _dot(xn, wu_ref[...].astype(BF16))
    act = (hg * jax.nn.sigmoid(hg) * hu).astype(BF16)
    acc_ref[...] += _dot(act, wo_ref[...].astype(BF16))

    @pl.when(j == nj - 1)
    def _():
        o_ref[...] = x_ref[...] + 0.5 * _rms(acc_ref[...], gpost_ref[...])


def _ffn_half(x, g_pre, g_post, w_in, w_out, layer, half):
    T, D = x.shape
    F = w_out.shape[2]
    TM, TF = FFN_TM, FFN_TF
    nj = F // TF
    est = (2 * TM * D * 4 + TM * D * (4 + 2) + 2 * 3 * D * TF * 4
           + 3 * D * TF * 2 + 6 * TM * TF * 4)
    return pl.pallas_call(
        functools.partial(_ffn_kernel, nj=nj),
        grid=(T // TM, nj),
        in_specs=[
            pl.BlockSpec((TM, D), lambda i, j: (i, 0), pipeline_mode=pl.Buffered(1)),
            pl.BlockSpec((1, D), lambda i, j: (0, 0)),
            pl.BlockSpec((1, D), lambda i, j: (0, 0)),
            pl.BlockSpec((None, None, D, TF), lambda i, j: (layer, half, 0, j)),
            pl.BlockSpec((None, None, D, TF), lambda i, j: (layer, half, 0, nj + j)),
            pl.BlockSpec((None, None, TF, D), lambda i, j: (layer, half, j, 0)),
        ],
        out_specs=pl.BlockSpec((TM, D), lambda i, j: (i, 0), pipeline_mode=pl.Buffered(1)),
        out_shape=jax.ShapeDtypeStruct((T, D), F32),
        scratch_shapes=[pltpu.VMEM((TM, D), BF16), pltpu.VMEM((TM, D), F32)],
        compiler_params=pltpu.CompilerParams(
            dimension_semantics=("parallel", "arbitrary"),
            vmem_limit_bytes=_vmem_limit(est)),
        name="ffn_half",
    )(x, g_pre, g_post, w_in, w_in, w_out)


def _rope_tile(y, cos, sin_signed):
    lane = lax.broadcasted_iota(jnp.int32, y.shape, 1)
    first_half = (lane & (HEAD_DIM - 1)) < (HEAD_DIM // 2)
    rot = jnp.where(first_half,
                    pltpu.roll(y, LANES - HEAD_DIM // 2, 1),
                    pltpu.roll(y, HEAD_DIM // 2, 1))
    return y * cos + rot * sin_signed


def _proj_kernel(x_ref, g_ref, w_ref, o_ref, xn_ref):
    @pl.when(pl.program_id(1) == 0)
    def _():
        xn_ref[...] = _rms(x_ref[...], g_ref[...]).astype(BF16)

    o_ref[...] = _dot(xn_ref[...], w_ref[...].astype(BF16))


def _proj_rope_kernel(x_ref, g_ref, w_ref, cos_ref, sin_ref, o_ref, xn_ref, *, n_rope_blocks):
    j = pl.program_id(1)

    @pl.when(j == 0)
    def _():
        xn_ref[...] = _rms(x_ref[...], g_ref[...]).astype(BF16)

    y = _dot(xn_ref[...], w_ref[...].astype(BF16))

    @pl.when(j < n_rope_blocks)
    def _():
        cos = cos_ref[...]
        sin = sin_ref[...]
        for c in range(y.shape[1] // LANES):
            cols = slice(c * LANES, (c + 1) * LANES)
            o_ref[:, cols] = _rope_tile(y[:, cols], cos, sin)

    @pl.when(j >= n_rope_blocks)
    def _():
        o_ref[...] = y


def _proj(x, g, w, layer, rope=None, n_rope_cols=0):
    T, D = x.shape
    N = w.shape[2]
    TM, TN = PROJ_TM, PROJ_TN
    est = 2 * TM * D * 4 + TM * D * 2 + 2 * D * TN * 4 + D * TN * 2 + 4 * TM * TN * 4
    in_specs = [
        pl.BlockSpec((TM, D), lambda i, j: (i, 0)),
        pl.BlockSpec((1, D), lambda i, j: (0, 0)),
        pl.BlockSpec((None, D, TN), lambda i, j: (layer, 0, j)),
    ]
    args = [x, g, w]
    if rope is None:
        body = _proj_kernel
    else:
        body = functools.partial(_proj_rope_kernel, n_rope_blocks=n_rope_cols // TN)
        in_specs += [pl.BlockSpec((TM, LANES), lambda i, j: (i, 0)),
                     pl.BlockSpec((TM, LANES), lambda i, j: (i, 0))]
        args += list(rope)
    return pl.pallas_call(
        body,
        grid=(T // TM, N // TN),
        in_specs=in_specs,
        out_specs=pl.BlockSpec((TM, TN), lambda i, j: (i, j)),
        out_shape=jax.ShapeDtypeStruct((T, N), F32),
        scratch_shapes=[pltpu.VMEM((TM, D), BF16)],
        compiler_params=pltpu.CompilerParams(
            dimension_semantics=("parallel", "arbitrary"),
            vmem_limit_bytes=_vmem_limit(est)),
        name="mixer_in_proj",
    )(*args)


def _outproj_kernel(o_ref, w_ref, x_ref, g_ref, y_ref, acc_ref, *, nk):
    k = pl.program_id(1)

    @pl.when(k == 0)
    def _():
        acc_ref[...] = jnp.zeros_like(acc_ref)

    acc_ref[...] += _dot(o_ref[...], w_ref[...].astype(BF16))

    @pl.when(k == nk - 1)
    def _():
        y_ref[...] = x_ref[...] + _rms(acc_ref[...], g_ref[...])


def _outproj(o, w, x, g, layer):
    T, D = x.shape
    K = o.shape[1]
    TM, TK = OUT_TM, OUT_TK
    nk = K // TK
    est = 2 * TM * TK * 2 + 2 * TK * D * 4 + TK * D * 2 + 5 * TM * D * 4
    return pl.pallas_call(
        functools.partial(_outproj_kernel, nk=nk),
        grid=(T // TM, nk),
        in_specs=[
            pl.BlockSpec((TM, TK), lambda i, k: (i, k)),
            pl.BlockSpec((None, TK, D), lambda i, k: (layer, k, 0)),
            pl.BlockSpec((TM, D), lambda i, k: (i, 0)),
            pl.BlockSpec((1, D), lambda i, k: (0, 0)),
        ],
        out_specs=pl.BlockSpec((TM, D), lambda i, k: (i, 0)),
        out_shape=jax.ShapeDtypeStruct((T, D), F32),
        scratch_shapes=[pltpu.VMEM((TM, D), F32)],
        compiler_params=pltpu.CompilerParams(
            dimension_semantics=("parallel", "arbitrary"),
            vmem_limit_bytes=_vmem_limit(est)),
        name="mixer_out_proj",
    )(o, w, x, g)


def _level_matrix(C):
    nlev = int(math.log2(C))
    t = np.arange(C)[:, None]
    u = np.arange(C)[None, :]
    rows = [u <= t]
    for li in range(nlev):
        m = C >> (li + 1)
        upper = (t & m) != 0
        ref = (t // (2 * m)) * (2 * m) + m - 1
        rows.append(upper & (u > ref) & (u <= t))
        rows.append((~upper) & (u > t) & (u <= ref))
    return np.concatenate(rows, axis=0).astype(np.float32)


def _hgrn_chunk(q, fz, v, gz, loglb, log1m, onem, ng, cm, st, *, C, valid_len):
    nlev = int(math.log2(C))
    lsig = jnp.minimum(fz, 0.0) - jnp.log1p(jnp.exp(-jnp.abs(fz)))
    ct = log1m + lsig
    g = jnp.maximum(loglb, ct) + jnp.log1p(jnp.exp(-jnp.abs(loglb - ct)))
    k = onem * jax.nn.sigmoid(-fz)
    row = lax.broadcasted_iota(jnp.int32, (C, LANES), 0)
    if valid_len < C:
        valid = row < valid_len
        g = jnp.where(valid, g, 0.0)
        k = jnp.where(valid, k, 0.0)

    g1 = g.astype(BF16)
    r1 = g - g1.astype(F32)
    g2 = r1.astype(BF16)
    g3 = (r1 - g2.astype(F32)).astype(BF16)
    e3 = _dot(cm, jnp.concatenate([g1, g2, g3], axis=1))
    e = e3[:, :LANES] + e3[:, LANES:2 * LANES] + e3[:, 2 * LANES:]
    b = e[0:C]

    ti = lax.broadcasted_iota(jnp.int32, (C, C), 0)
    si = lax.broadcasted_iota(jnp.int32, (C, C), 1)
    a = jnp.where(ti == si, _dot_nt(q.astype(BF16), k.astype(BF16)), 0.0)
    for li in range(nlev):
        m = C >> (li + 1)
        eq = e[(1 + 2 * li) * C:(2 + 2 * li) * C]
        ek = e[(2 + 2 * li) * C:(3 + 2 * li) * C]
        upper = (row & m) != 0
        qt = jnp.where(upper, q * jnp.exp(eq), 0.0).astype(BF16)
        kt = jnp.where(upper, 0.0, k * jnp.exp(ek)).astype(BF16)
        am = _dot_nt(qt, kt)
        if 2 * m < C:
            am = jnp.where((ti & (-2 * m)) == (si & (-2 * m)), am, 0.0)
        a = a + am

    o = _dot(a.astype(BF16), v.astype(BF16)) + _dot_nt((q * jnp.exp(b)).astype(BF16), st.astype(BF16))
    bl = b[C - 1:C]
    kb = (k * jnp.exp(bl - b)).astype(BF16)
    st_new = st * jnp.exp(bl) + _dot_tn(v.astype(BF16), kb)
    on = o * lax.rsqrt(jnp.mean(o * o, axis=-1, keepdims=True) + NORM_EPS) * ng
    return on * (gz * jax.nn.sigmoid(gz)), st_new


def _hgrn_kernel(*refs, C, NC, HB, valid_len, has_s0):
    if has_s0:
        (q_ref, f_ref, i_ref, g_ref, loglb_ref, log1m_ref, onem_ref, ng_ref, cm_ref, s0_ref,
         o_ref, s_ref, st_ref) = refs
    else:
        (q_ref, f_ref, i_ref, g_ref, loglb_ref, log1m_ref, onem_ref, ng_ref, cm_ref,
         o_ref, s_ref, st_ref) = refs
        s0_ref = None

    for h in range(HB):
        if has_s0:
            st_ref[h] = s0_ref[0, h].T
        else:
            st_ref[h] = jnp.zeros((HG_DV, HG_DK), F32)

    def body(c, carry):
        r0 = pl.multiple_of(c * C, C)
        for h in range(HB):
            cols = slice(h * LANES, (h + 1) * LANES)
            out, st_new = _hgrn_chunk(
                q_ref[pl.ds(r0, C), cols], f_ref[pl.ds(r0, C), cols],
                i_ref[pl.ds(r0, C), cols], g_ref[pl.ds(r0, C), cols],
                loglb_ref[:, cols], log1m_ref[:, cols], onem_ref[:, cols], ng_ref[:, cols],
                cm_ref[...], st_ref[h], C=C, valid_len=valid_len)
            st_ref[h] = st_new
            o_ref[pl.ds(r0, C), cols] = out.astype(o_ref.dtype)
        return carry

    if NC == 1:
        body(0, 0)
    else:
        lax.fori_loop(0, NC, body, 0)

    for h in range(HB):
        s_ref[0, h] = st_ref[h].T


def _hgrn_scan(p, nseq, L, C, HB, valid_len, lbp, ng, s0, out_dtype):
    H = ng.shape[1] // LANES
    nhb = H // HB
    W = HB * LANES
    NC = L // C
    cm = jnp.asarray(_level_matrix(C), dtype=BF16)
    loglb, log1m, onem = lbp
    has_s0 = s0 is not None

    def sect(s):
        return pl.BlockSpec((L, W), lambda b, hb, s=s: (b, s * nhb + hb))

    vec = pl.BlockSpec((1, W), lambda b, hb: (0, hb))
    in_specs = [sect(0), sect(1), sect(2), sect(3), vec, vec, vec, vec,
                pl.BlockSpec(cm.shape, lambda b, hb: (0, 0))]
    args = [p, p, p, p, loglb, log1m, onem, ng, cm]
    if has_s0:
        s0_all, s0_layer = s0
        in_specs.append(pl.BlockSpec((None, 1, HB, HG_DK, HG_DV), lambda b, hb: (s0_layer, b, hb, 0, 0)))
        args.append(s0_all)
    est = 2 * 4 * L * W * 4 + 2 * L * W * 4 + 6 * HB * HG_DK * HG_DV * 4 + (8 << 20)
    return pl.pallas_call(
        functools.partial(_hgrn_kernel, C=C, NC=NC, HB=HB, valid_len=valid_len, has_s0=has_s0),
        grid=(nseq, nhb),
        in_specs=in_specs,
        out_specs=[pl.BlockSpec((L, W), lambda b, hb: (b, hb)),
                   pl.BlockSpec((1, HB, HG_DK, HG_DV), lambda b, hb: (b, hb, 0, 0))],
        out_shape=[jax.ShapeDtypeStruct((nseq * L, H * LANES), out_dtype),
                   jax.ShapeDtypeStruct((nseq, H, HG_DK, HG_DV), F32)],
        scratch_shapes=[pltpu.VMEM((HB, HG_DV, HG_DK), F32)],
        compiler_params=pltpu.CompilerParams(
            dimension_semantics=("parallel", "parallel"),
            vmem_limit_bytes=_vmem_limit(est)),
        name="hgrn_scan",
    )(*args)


def _sink_softmax(scores, sink):
    m = sink
    for s in scores:
        m = jnp.maximum(m, jnp.max(s, axis=-1, keepdims=True))
    ps = [jnp.exp(s - m) for s in scores]
    denom = jnp.exp(sink - m)
    for p in ps:
        denom = denom + jnp.sum(p, axis=-1, keepdims=True)
    return [p / denom for p in ps]


def _attn_prompt_kernel(sinks_ref, q_ref, kc_ref, kp_ref, vc_ref, vp_ref, o_ref, *, n_kv, group):
    i = pl.program_id(1)
    W = WINDOW
    r = lax.broadcasted_iota(jnp.int32, (W, 2 * W), 0)
    j = lax.broadcasted_iota(jnp.int32, (W, 2 * W), 1)
    lo = jnp.maximum(r, jnp.where(i > 0, 0, W))
    mask = (j >= lo) & (j <= r + W)
    scale = HEAD_DIM ** -0.5
    for kh in range(n_kv):
        kcols = slice(kh * HEAD_DIM, (kh + 1) * HEAD_DIM)
        kk = jnp.concatenate([kp_ref[:, kcols], kc_ref[:, kcols]], axis=0).astype(BF16)
        vv = jnp.concatenate([vp_ref[:, kcols], vc_ref[:, kcols]], axis=0).astype(BF16)
        outs = []
        for gq in range(group):
            hq = kh * group + gq
            qh = q_ref[:, hq * HEAD_DIM:(hq + 1) * HEAD_DIM].astype(BF16)
            s = jnp.where(mask, _dot_nt(qh, kk) * scale, -jnp.inf)
            (p,) = _sink_softmax([s], sinks_ref[hq])
            outs.append(_dot(p.astype(BF16), vv))
        for pr in range(group // 2):
            c0 = (kh * group + 2 * pr) * HEAD_DIM
            o_ref[:, c0:c0 + 2 * HEAD_DIM] = jnp.concatenate(
                [outs[2 * pr], outs[2 * pr + 1]], axis=1).astype(o_ref.dtype)


def _attn_prompt(p, sinks, B, L, n_q, n_kv):
    W = WINDOW
    nb = L // W
    QW = n_q * HEAD_DIM
    KW = n_kv * HEAD_DIM
    kblk = QW // KW
    vblk = (QW + KW) // KW
    est = 2 * (W * QW * 4 + 4 * W * KW * 4 + W * QW * 2) + (16 << 20)
    return pl.pallas_call(
        functools.partial(_attn_prompt_kernel, n_kv=n_kv, group=n_q // n_kv),
        grid=(B, nb),
        in_specs=[
            pl.BlockSpec(memory_space=pltpu.SMEM),
            pl.BlockSpec((W, QW), lambda b, i: (b * nb + i, 0)),
            pl.BlockSpec((W, KW), lambda b, i: (b * nb + i, kblk)),
            pl.BlockSpec((W, KW), lambda b, i: (b * nb + jnp.maximum(i - 1, 0), kblk)),
            pl.BlockSpec((W, KW), lambda b, i: (b * nb + i, vblk)),
            pl.BlockSpec((W, KW), lambda b, i: (b * nb + jnp.maximum(i - 1, 0), vblk)),
        ],
        out_specs=pl.BlockSpec((W, QW), lambda b, i: (b * nb + i, 0)),
        out_shape=jax.ShapeDtypeStruct((B * L, QW), BF16),
        compiler_params=pltpu.CompilerParams(
            dimension_semantics=("parallel", "parallel"),
            vmem_limit_bytes=_vmem_limit(est)),
        name="swa_prompt",
    )(sinks, p, p, p, p, p)


def _attn_sample_kernel(sinks_ref, qkv_ref, ck_ref, cv_ref, o_ref, *, n_q, n_kv, n_new):
    R = qkv_ref.shape[1]
    W = WINDOW
    group = n_q // n_kv
    QW = n_q * HEAD_DIM
    KW = n_kv * HEAD_DIM
    scale = HEAD_DIM ** -0.5
    r = lax.broadcasted_iota(jnp.int32, (R, 2 * W), 0)
    j = lax.broadcasted_iota(jnp.int32, (R, 2 * W), 1)
    mask = (j >= r) & (j <= r + W) & (j < W + n_new)
    zpad = jnp.zeros((W - R, HEAD_DIM), F32)
    for kh in range(n_kv):
        kcols = slice(kh * HEAD_DIM, (kh + 1) * HEAD_DIM)
        kn = qkv_ref[0, :, QW + kh * HEAD_DIM:QW + (kh + 1) * HEAD_DIM]
        vn = qkv_ref[0, :, QW + KW + kh * HEAD_DIM:QW + KW + (kh + 1) * HEAD_DIM]
        kk = jnp.concatenate([ck_ref[0, :, kcols], kn, zpad], axis=0).astype(BF16)
        vv = jnp.concatenate([cv_ref[0, :, kcols], vn, zpad], axis=0).astype(BF16)
        outs = []
        for gq in range(group):
            hq = kh * group + gq
            qh = qkv_ref[0, :, hq * HEAD_DIM:(hq + 1) * HEAD_DIM].astype(BF16)
            s = jnp.where(mask, _dot_nt(qh, kk) * scale, -jnp.inf)
            (p,) = _sink_softmax([s], sinks_ref[hq])
            outs.append(_dot(p.astype(BF16), vv))
        for pr in range(group // 2):
            c0 = (kh * group + 2 * pr) * HEAD_DIM
            o_ref[0, :, c0:c0 + 2 * HEAD_DIM] = jnp.concatenate(
                [outs[2 * pr], outs[2 * pr + 1]], axis=1)


def _attn_sample(qkv, ck, cv, layer, sinks, n_q, n_kv, n_new):
    B, R, NW = qkv.shape
    QW = n_q * HEAD_DIM
    KW = n_kv * HEAD_DIM
    return pl.pallas_call(
        functools.partial(_attn_sample_kernel, n_q=n_q, n_kv=n_kv, n_new=n_new),
        grid=(B,),
        in_specs=[
            pl.BlockSpec(memory_space=pltpu.SMEM),
            pl.BlockSpec((1, R, NW), lambda b: (b, 0, 0)),
            pl.BlockSpec((None, 1, WINDOW, KW), lambda b: (layer, b, 0, 0)),
            pl.BlockSpec((None, 1, WINDOW, KW), lambda b: (layer, b, 0, 0)),
        ],
        out_specs=pl.BlockSpec((1, R, QW), lambda b: (b, 0, 0)),
        out_shape=jax.ShapeDtypeStruct((B, R, QW), F32),
        compiler_params=pltpu.CompilerParams(dimension_semantics=("parallel",)),
        name="swa_sample",
    )(sinks, qkv, ck, cv)


def _rope_tables(pos):
    half = HEAD_DIM // 2
    inv = jnp.power(ROPE_THETA, -jnp.arange(half, dtype=F32) * (2.0 / HEAD_DIM))
    ang = pos.astype(F32)[:, None] * inv[None, :]
    cos = jnp.cos(ang)
    sin = jnp.sin(ang)
    reps = LANES // HEAD_DIM
    cos_t = jnp.tile(jnp.concatenate([cos, cos], axis=1), (1, reps))
    sin_t = jnp.tile(jnp.concatenate([-sin, sin], axis=1), (1, reps))
    return cos_t, sin_t


def kernel(x_prompt, x_sample, state_hgrn, cache_k_win, cache_v_win, norm_gains, w_ffn_in, w_ffn_out,
           w_hgrn_in, hgrn_lb_logits, hgrn_norm_gain, w_hgrn_out, w_attn_in, attn_sinks, w_attn_out):
    B, L, D = x_prompt.shape
    SB, SL, _ = x_sample.shape
    depth = norm_gains.shape[0]
    TP, TS = B * L, SB * SL
    H = hgrn_norm_gain.shape[1] // HG_DV
    n_kv = cache_k_win.shape[3]
    n_q = w_attn_out.shape[1] // HEAD_DIM
    QW, KW = n_q * HEAD_DIM, n_kv * HEAD_DIM
    buf = cache_k_win.shape[2]

    x = jnp.concatenate([x_prompt.reshape(TP, D), x_sample.reshape(TS, D)], axis=0)

    lb_all = jnp.cumsum(jax.nn.softmax(hgrn_lb_logits.astype(F32), axis=0), axis=0)
    lb_all = lb_all - lb_all[0:1]

    pos = jnp.concatenate([jnp.tile(jnp.arange(L, dtype=jnp.int32), B),
                           jnp.tile(PAST_LEN + jnp.arange(SL, dtype=jnp.int32), SB)])
    rope = _rope_tables(pos)

    hg_p, hg_s, kp_l, vp_l, ks_l, vs_l = [], [], [], [], [], []
    for layer in range(depth):
        g = norm_gains[layer]
        x = _ffn_half(x, g[0:1], g[1:2], w_ffn_in, w_ffn_out, layer, 0)
        if layer % N_MIXERS == 0:
            a = layer // N_MIXERS
            p = _proj(x, g[2:3], w_hgrn_in, a)
            lb = lb_all[a][None, :]
            lbp = (jnp.log(lb), jnp.log1p(-lb), 1.0 - lb)
            ng = hgrn_norm_gain[a][None, :].astype(F32)
            o_p, s_p = _hgrn_scan(p, B, L, HG_CHUNK, 2, HG_CHUNK, lbp, ng, None, BF16)
            ps = jnp.pad(p[TP:].reshape(SB, SL, -1), ((0, 0), (0, SAMPLE_PAD - SL), (0, 0)))
            o_s, s_s = _hgrn_scan(ps.reshape(SB * SAMPLE_PAD, -1), SB, SAMPLE_PAD, SAMPLE_PAD, H, SL,
                                  lbp, ng, (state_hgrn, a), F32)
            o_s = o_s.reshape(SB, SAMPLE_PAD, -1)[:, :SL].reshape(TS, -1).astype(BF16)
            o = jnp.concatenate([o_p, o_s], axis=0)
            x = _outproj(o, w_hgrn_out, x, g[3:4], a)
            hg_p.append(s_p.astype(state_hgrn.dtype))
            hg_s.append(s_s.astype(state_hgrn.dtype))
        else:
            bb = layer // N_MIXERS
            p = _proj(x, g[2:3], w_attn_in, bb, rope=rope, n_rope_cols=QW + KW)
            sinks = attn_sinks[bb].astype(F32)
            o_p = _attn_prompt(p, sinks, B, L, n_q, n_kv)
            pn = p[TP:].reshape(SB, SL, -1)
            ck = cache_k_win.reshape(-1, SB, buf, KW)
            cv = cache_v_win.reshape(-1, SB, buf, KW)
            o_s = _attn_sample(jnp.pad(pn, ((0, 0), (0, 8 - SL), (0, 0))), ck, cv, bb, sinks, n_q, n_kv, SL)
            o = jnp.concatenate([o_p, o_s[:, :SL].reshape(TS, QW).astype(BF16)], axis=0)
            x = _outproj(o, w_attn_out, x, g[3:4], bb)
            pp = p[:TP].reshape(B, L, -1)
            kp_l.append(pp[:, L - buf:, QW:QW + KW].reshape(B, buf, n_kv, HEAD_DIM))
            vp_l.append(pp[:, L - buf:, QW + KW:].reshape(B, buf, n_kv, HEAD_DIM))
            k_new = pn[:, :, QW:QW + KW].reshape(SB, SL, n_kv, HEAD_DIM)
            v_new = pn[:, :, QW + KW:].reshape(SB, SL, n_kv, HEAD_DIM)
            ks_l.append(jnp.concatenate([cache_k_win[bb], k_new], axis=1)[:, -buf:])
            vs_l.append(jnp.concatenate([cache_v_win[bb], v_new], axis=1)[:, -buf:])
        x = _ffn_half(x, g[4:5], g[5:6], w_ffn_in, w_ffn_out, layer, 1)

    yp = x[:TP].reshape(B, L, D)
    ys = x[TP:].reshape(SB, SL, D)
    return (yp, ys, jnp.stack(hg_p), jnp.stack(hg_s), jnp.stack(kp_l), jnp.stack(vp_l),
            jnp.stack(ks_l), jnp.stack(vs_l))
```

```python
import functools
import math

import numpy as np
import jax
import jax.numpy as jnp
from jax import lax
from jax.experimental import pallas as pl
from jax.experimental.pallas import tpu as pltpu

F32 = jnp.float32
BF16 = jnp.bfloat16

NORM_EPS = 1e-6
HEAD_DIM = 64
HG_DK = 128
HG_DV = 128
HG_CHUNK = 64
WINDOW = 128
PAST_LEN = 16384
ROPE_THETA = 10000.0
N_MIXERS = 2

LANES = 128
SUBLANES = 8
VMEM_BYTES_V7X = 64 * 1024 * 1024
VMEM_HEADROOM = 6 * 1024 * 1024

FFN_TM, FFN_TF = 1040, 256
PROJ_TM, PROJ_TN = 1040, 512
OUT_TM, OUT_TK = 640, 512
SAMPLE_PAD = 16
HGRN_HB = 4


def _vmem_limit(nbytes):
    return int(min(VMEM_BYTES_V7X - VMEM_HEADROOM, max(nbytes * 5 // 4, 16 * 1024 * 1024)))


def _rms(x, g):
    return x * lax.rsqrt(jnp.mean(x * x, axis=-1, keepdims=True) + NORM_EPS) * g


def _dot(a, b):
    return jnp.dot(a, b, preferred_element_type=F32)


def _dot_nt(a, b):
    return lax.dot_general(a, b, (((1,), (1,)), ((), ())), preferred_element_type=F32)


def _dot_tn(a, b):
    return lax.dot_general(a, b, (((0,), (0,)), ((), ())), preferred_element_type=F32)


def _ffn_kernel(x_ref, gpre_ref, gpost_ref, wg_ref, wu_ref, wo_ref, o_ref, xn_ref, *, nj):
    j = pl.program_id(1)

    @pl.when(j == 0)
    def _():
        xn_ref[...] = _rms(x_ref[...], gpre_ref[...]).astype(BF16)
        o_ref[...] = jnp.zeros_like(o_ref)

    xn = xn_ref[...]
    hg = _dot(xn, wg_ref[...].astype(BF16))
    hu = _dot(xn, wu_ref[...].astype(BF16))
    act = (hg * jax.nn.sigmoid(hg) * hu).astype(BF16)
    o_ref[...] += _dot(act, wo_ref[...].astype(BF16))

    @pl.when(j == nj - 1)
    def _():
        o_ref[...] = x_ref[...] + 0.5 * _rms(o_ref[...], gpost_ref[...])


def _ffn_half(x, g_pre, g_post, w_in, w_out, layer, half):
    T, D = x.shape
    F = w_out.shape[2]
    TM, TF = FFN_TM, FFN_TF
    nj = F // TF
    est = (3 * TM * D * 4 + TM * D * 2 + 2 * 3 * D * TF * 4
           + 3 * D * TF * 2 + 6 * TM * TF * 4)
    return pl.pallas_call(
        functools.partial(_ffn_kernel, nj=nj),
        grid=(T // TM, nj),
        in_specs=[
            pl.BlockSpec((TM, D), lambda i, j: (i, 0), pipeline_mode=pl.Buffered(1)),
            pl.BlockSpec((1, D), lambda i, j: (0, 0)),
            pl.BlockSpec((1, D), lambda i, j: (0, 0)),
            pl.BlockSpec((None, None, D, TF), lambda i, j: (layer, half, 0, j)),
            pl.BlockSpec((None, None, D, TF), lambda i, j: (layer, half, 0, nj + j)),
            pl.BlockSpec((None, None, TF, D), lambda i, j: (layer, half, j, 0)),
        ],
        out_specs=pl.BlockSpec((TM, D), lambda i, j: (i, 0)),
        out_shape=jax.ShapeDtypeStruct((T, D), F32),
        scratch_shapes=[pltpu.VMEM((TM, D), BF16)],
        compiler_params=pltpu.CompilerParams(
            dimension_semantics=("parallel", "arbitrary"),
            vmem_limit_bytes=_vmem_limit(est)),
        name="ffn_half",
    )(x, g_pre, g_post, w_in, w_in, w_out)


def _rope_tile(y, cos, sin_signed):
    lane = lax.broadcasted_iota(jnp.int32, y.shape, 1)
    first_half = (lane & (HEAD_DIM - 1)) < (HEAD_DIM // 2)
    rot = jnp.where(first_half,
                    pltpu.roll(y, LANES - HEAD_DIM // 2, 1),
                    pltpu.roll(y, HEAD_DIM // 2, 1))
    return y * cos + rot * sin_signed


def _proj_kernel(x_ref, g_ref, w_ref, o_ref, xn_ref):
    @pl.when(pl.program_id(1) == 0)
    def _():
        xn_ref[...] = _rms(x_ref[...], g_ref[...]).astype(BF16)

    o_ref[...] = _dot(xn_ref[...], w_ref[...].astype(BF16))


def _proj_rope_kernel(x_ref, g_ref, w_ref, cos_ref, sin_ref, o_ref, xn_ref, *, n_rope_blocks):
    j = pl.program_id(1)

    @pl.when(j == 0)
    def _():
        xn_ref[...] = _rms(x_ref[...], g_ref[...]).astype(BF16)

    y = _dot(xn_ref[...], w_ref[...].astype(BF16))

    @pl.when(j < n_rope_blocks)
    def _():
        cos = cos_ref[...]
        sin = sin_ref[...]
        for c in range(y.shape[1] // LANES):
            cols = slice(c * LANES, (c + 1) * LANES)
            o_ref[:, cols] = _rope_tile(y[:, cols], cos, sin)

    @pl.when(j >= n_rope_blocks)
    def _():
        o_ref[...] = y


def _proj(x, g, w, layer, rope=None, n_rope_cols=0):
    T, D = x.shape
    N = w.shape[2]
    TM, TN = PROJ_TM, PROJ_TN
    est = 2 * TM * D * 4 + TM * D * 2 + 2 * D * TN * 4 + D * TN * 2 + 4 * TM * TN * 4
    in_specs = [
        pl.BlockSpec((TM, D), lambda i, j: (i, 0)),
        pl.BlockSpec((1, D), lambda i, j: (0, 0)),
        pl.BlockSpec((None, D, TN), lambda i, j: (layer, 0, j)),
    ]
    args = [x, g, w]
    if rope is None:
        body = _proj_kernel
    else:
        body = functools.partial(_proj_rope_kernel, n_rope_blocks=n_rope_cols // TN)
        in_specs += [pl.BlockSpec((TM, LANES), lambda i, j: (i, 0)),
                     pl.BlockSpec((TM, LANES), lambda i, j: (i, 0))]
        args += list(rope)
    return pl.pallas_call(
        body,
        grid=(T // TM, N // TN),
        in_specs=in_specs,
        out_specs=pl.BlockSpec((TM, TN), lambda i, j: (i, j)),
        out_shape=jax.ShapeDtypeStruct((T, N), F32),
        scratch_shapes=[pltpu.VMEM((TM, D), BF16)],
        compiler_params=pltpu.CompilerParams(
            dimension_semantics=("parallel", "arbitrary"),
            vmem_limit_bytes=_vmem_limit(est)),
        name="mixer_in_proj",
    )(*args)


def _outproj_kernel(o_ref, w_ref, x_ref, g_ref, y_ref, *, nk):
    k = pl.program_id(1)

    @pl.when(k == 0)
    def _():
        y_ref[...] = jnp.zeros_like(y_ref)

    y_ref[...] += _dot(o_ref[...], w_ref[...].astype(BF16))

    @pl.when(k == nk - 1)
    def _():
        y_ref[...] = x_ref[...] + _rms(y_ref[...], g_ref[...])


def _outproj(o, w, x, g, layer):
    T, D = x.shape
    K = o.shape[1]
    TM, TK = OUT_TM, OUT_TK
    nk = K // TK
    est = 2 * TM * TK * 2 + 2 * TK * D * 4 + TK * D * 2 + 5 * TM * D * 4
    return pl.pallas_call(
        functools.partial(_outproj_kernel, nk=nk),
        grid=(T // TM, nk),
        in_specs=[
            pl.BlockSpec((TM, TK), lambda i, k: (i, k)),
            pl.BlockSpec((None, TK, D), lambda i, k: (layer, k, 0)),
            pl.BlockSpec((TM, D), lambda i, k: (i, 0)),
            pl.BlockSpec((1, D), lambda i, k: (0, 0)),
        ],
        out_specs=pl.BlockSpec((TM, D), lambda i, k: (i, 0)),
        out_shape=jax.ShapeDtypeStruct((T, D), F32),
        compiler_params=pltpu.CompilerParams(
            dimension_semantics=("parallel", "arbitrary"),
            vmem_limit_bytes=_vmem_limit(est)),
        name="mixer_out_proj",
    )(o, w, x, g)


def _tiles(x):
    return [x[SUBLANES * j:SUBLANES * (j + 1)] for j in range(x.shape[0] // SUBLANES)]


def _bcast_row(t, i):
    return jnp.broadcast_to(t[i:i + 1], t.shape)


def _tile_prefix(t, sub):
    p = t + jnp.where(sub >= 1, pltpu.roll(t, 1, 0), 0.0)
    p = p + jnp.where(sub >= 2, pltpu.roll(p, 2, 0), 0.0)
    return p + jnp.where(sub >= 4, pltpu.roll(p, 4, 0), 0.0)


def _hgrn_gates(fz, loglb, log1m, onem, valid):
    lsig = jnp.minimum(fz, 0.0) - jnp.log1p(jnp.exp(-jnp.abs(fz)))
    ct = log1m + lsig
    g = jnp.maximum(loglb, ct) + jnp.log1p(jnp.exp(-jnp.abs(loglb - ct)))
    k = onem * jax.nn.sigmoid(-fz)
    if valid is not None:
        g = jnp.where(valid, g, 0.0)
        k = jnp.where(valid, k, 0.0)
    return g, k


def _hgrn_level_operands(m, qt, kt, gt, bt, sub):
    nv = len(qt)
    zero = jnp.zeros_like(qt[0])
    qm, km = [], []
    if m >= SUBLANES:
        mv = m // SUBLANES
        for j in range(nv):
            jr = (j // (2 * mv)) * (2 * mv) + mv - 1
            ref = _bcast_row(bt[jr], SUBLANES - 1)
            if j % (2 * mv) >= mv:
                qm.append(qt[j] * jnp.exp(bt[j] - ref))
                km.append(zero)
            else:
                qm.append(zero)
                km.append(kt[j] * jnp.exp(ref - bt[j]))
        return qm, km
    upper = (sub & m) != 0
    for j in range(nv):
        if m == 1:
            e = jnp.exp(gt[j])
            qm.append(jnp.where(upper, qt[j] * e, 0.0))
            km.append(jnp.where(upper, 0.0, kt[j]))
            continue
        if m == 4:
            ref = _bcast_row(bt[j], 3)
        else:
            ref = jnp.where(sub < 4, _bcast_row(bt[j], 1), _bcast_row(bt[j], 5))
        e = jnp.exp(-jnp.abs(bt[j] - ref))
        qm.append(jnp.where(upper, qt[j] * e, 0.0))
        km.append(jnp.where(upper, 0.0, kt[j] * e))
    return qm, km


def _hgrn_chunk_group(loads, params, sts, *, C, valid_len):
    nh = len(loads)
    sub = lax.broadcasted_iota(jnp.int32, (SUBLANES, LANES), 0)
    valid = None
    if valid_len < C:
        valid = lax.broadcasted_iota(jnp.int32, (C, LANES), 0) < valid_len
    ti = lax.broadcasted_iota(jnp.int32, (C, C), 0)
    si = lax.broadcasted_iota(jnp.int32, (C, C), 1)

    gk = [_hgrn_gates(loads[h][1], params[h][0], params[h][1], params[h][2], valid) for h in range(nh)]

    gts, bts = [], []
    for h in range(nh):
        gt = _tiles(gk[h][0])
        bt = []
        for j, t in enumerate(gt):
            p = _tile_prefix(t, sub)
            bt.append(p if j == 0 else p + _bcast_row(bt[j - 1], SUBLANES - 1))
        gts.append(gt)
        bts.append(bt)

    scores = []
    for h in range(nh):
        q, k = loads[h][0], gk[h][1]
        qt, kt = _tiles(q), _tiles(k)
        a = jnp.where(ti == si, _dot_nt(q.astype(BF16), k.astype(BF16)), 0.0)
        m = C // 2
        while m >= 1:
            qm, km = _hgrn_level_operands(m, qt, kt, gts[h], bts[h], sub)
            am = _dot_nt(jnp.concatenate(qm, axis=0).astype(BF16), jnp.concatenate(km, axis=0).astype(BF16))
            if 2 * m < C:
                am = jnp.where((ti & (-2 * m)) == (si & (-2 * m)), am, 0.0)
            a = a + am
            m //= 2
        scores.append(a)

    outs = []
    for h in range(nh):
        q, _, v, gz = loads[h]
        k = gk[h][1]
        ng = params[h][3]
        st = sts[h]
        b = jnp.concatenate(bts[h], axis=0)
        o = _dot(scores[h].astype(BF16), v.astype(BF16)) + _dot_nt((q * jnp.exp(b)).astype(BF16), st.astype(BF16))
        bl = b[C - 1:C]
        kb = (k * jnp.exp(bl - b)).astype(BF16)
        st_new = st * jnp.exp(bl) + _dot_tn(v.astype(BF16), kb)
        on = o * lax.rsqrt(jnp.mean(o * o, axis=-1, keepdims=True) + NORM_EPS) * ng
        outs.append((on * (gz * jax.nn.sigmoid(gz)), st_new))
    return outs


def _hgrn_kernel(*refs, C, NC, HB, valid_len, has_s0):
    if has_s0:
        (q_ref, f_ref, i_ref, g_ref, loglb_ref, log1m_ref, onem_ref, ng_ref, s0_ref,
         o_ref, s_ref, st_ref) = refs
    else:
        (q_ref, f_ref, i_ref, g_ref, loglb_ref, log1m_ref, onem_ref, ng_ref,
         o_ref, s_ref, st_ref) = refs
        s0_ref = None

    for h in range(HB):
        if has_s0:
            st_ref[h] = s0_ref[0, h].T
        else:
            st_ref[h] = jnp.zeros((HG_DV, HG_DK), F32)

    def body(c, carry):
        rows = pl.ds(pl.multiple_of(c * C, C), C)
        cols = [slice(h * LANES, (h + 1) * LANES) for h in range(HB)]
        loads = [(q_ref[rows, cs], f_ref[rows, cs], i_ref[rows, cs], g_ref[rows, cs]) for cs in cols]
        params = [(loglb_ref[:, cs], log1m_ref[:, cs], onem_ref[:, cs], ng_ref[:, cs]) for cs in cols]
        outs = _hgrn_chunk_group(loads, params, [st_ref[h] for h in range(HB)], C=C, valid_len=valid_len)
        for h in range(HB):
            st_ref[h] = outs[h][1]
            o_ref[rows, cols[h]] = outs[h][0].astype(o_ref.dtype)
        return carry

    if NC == 1:
        body(0, 0)
    else:
        lax.fori_loop(0, NC, body, 0)

    for h in range(HB):
        s_ref[0, h] = st_ref[h].T


def _hgrn_scan(p, nseq, L, C, HB, valid_len, lbp, ng, s0, out_dtype):
    H = ng.shape[1] // LANES
    nhb = H // HB
    W = HB * LANES
    NC = L // C
    loglb, log1m, onem = lbp
    has_s0 = s0 is not None

    def sect(s):
        return pl.BlockSpec((L, W), lambda b, hb, s=s: (b, s * nhb + hb))

    vec = pl.BlockSpec((1, W), lambda b, hb: (0, hb))
    in_specs = [sect(0), sect(1), sect(2), sect(3), vec, vec, vec, vec]
    args = [p, p, p, p, loglb, log1m, onem, ng]
    if has_s0:
        s0_all, s0_layer = s0
        in_specs.append(pl.BlockSpec((None, 1, HB, HG_DK, HG_DV), lambda b, hb: (s0_layer, b, hb, 0, 0)))
        args.append(s0_all)
    est = 2 * 4 * L * W * 4 + 2 * L * W * 4 + 6 * HB * HG_DK * HG_DV * 4 + (8 << 20)
    return pl.pallas_call(
        functools.partial(_hgrn_kernel, C=C, NC=NC, HB=HB, valid_len=valid_len, has_s0=has_s0),
        grid=(nseq, nhb),
        in_specs=in_specs,
        out_specs=[pl.BlockSpec((L, W), lambda b, hb: (b, hb)),
                   pl.BlockSpec((1, HB, HG_DK, HG_DV), lambda b, hb: (b, hb, 0, 0))],
        out_shape=[jax.ShapeDtypeStruct((nseq * L, H * LANES), out_dtype),
                   jax.ShapeDtypeStruct((nseq, H, HG_DK, HG_DV), F32)],
        scratch_shapes=[pltpu.VMEM((HB, HG_DV, HG_DK), F32)],
        compiler_params=pltpu.CompilerParams(
            dimension_semantics=("parallel", "parallel"),
            vmem_limit_bytes=_vmem_limit(est)),
        name="hgrn_scan",
    )(*args)


def _sink_softmax(scores, sink):
    m = sink
    for s in scores:
        m = jnp.maximum(m, jnp.max(s, axis=-1, keepdims=True))
    ps = [jnp.exp(s - m) for s in scores]
    denom = jnp.exp(sink - m)
    for p in ps:
        denom = denom + jnp.sum(p, axis=-1, keepdims=True)
    return [p / denom for p in ps]


def _attn_prompt_kernel(sinks_ref, q_ref, kc_ref, kp_ref, vc_ref, vp_ref, o_ref, *, n_kv, group):
    i = pl.program_id(1)
    W = WINDOW
    HD = HEAD_DIM
    j = lax.broadcasted_iota(jnp.int32, (2 * W, W), 0)
    r = lax.broadcasted_iota(jnp.int32, (2 * W, W), 1)
    mask = (j >= jnp.maximum(r, jnp.where(i > 0, 0, W))) & (j <= r + W)
    scale = HD ** -0.5
    zeros = jnp.zeros((HD, W), F32)

    s_all, vt_all = [], []
    for c in range(n_kv // 2):
        cs = slice(c * LANES, (c + 1) * LANES)
        ktile = jnp.concatenate([kp_ref[:, cs], kc_ref[:, cs]], axis=0).astype(BF16)
        vt_all.append(jnp.concatenate([vp_ref[:, cs], vc_ref[:, cs]], axis=0).T.astype(BF16))
        for par in range(2):
            kh = 2 * c + par
            rhs = []
            for pr in range(group // 2):
                c0 = (kh * group + 2 * pr) * HD
                pt = (q_ref[:, c0:c0 + LANES] * scale).T
                for half in range(2):
                    blk = pt[half * HD:(half + 1) * HD]
                    rhs.append(jnp.concatenate([blk, zeros] if par == 0 else [zeros, blk], axis=0))
            s_all.append(_dot(ktile, jnp.concatenate(rhs, axis=1).astype(BF16)))

    p_all = []
    for kh in range(n_kv):
        pn = []
        for hh in range(group):
            sink = sinks_ref[kh * group + hh]
            sm = jnp.where(mask, s_all[kh][:, hh * W:(hh + 1) * W], -jnp.inf)
            mx = jnp.maximum(jnp.max(sm, axis=0, keepdims=True), sink)
            p = jnp.exp(sm - mx)
            den = jnp.sum(p, axis=0, keepdims=True) + jnp.exp(sink - mx)
            pn.append((p * (1.0 / den)).astype(BF16))
        p_all.append(jnp.concatenate(pn, axis=1))

    for kh in range(n_kv):
        par = kh % 2
        o_t = _dot(vt_all[kh // 2], p_all[kh])[par * HD:(par + 1) * HD]
        for pr in range(group // 2):
            pair_t = jnp.concatenate([o_t[:, (2 * pr) * W:(2 * pr + 1) * W],
                                      o_t[:, (2 * pr + 1) * W:(2 * pr + 2) * W]], axis=0)
            c0 = (kh * group + 2 * pr) * HD
            o_ref[:, c0:c0 + LANES] = pair_t.T.astype(o_ref.dtype)


def _attn_prompt(p, sinks, B, L, n_q, n_kv):
    W = WINDOW
    nb = L // W
    QW = n_q * HEAD_DIM
    KW = n_kv * HEAD_DIM
    kblk = QW // KW
    vblk = (QW + KW) // KW
    est = 2 * (W * QW * 4 + 4 * W * KW * 4 + W * QW * 2) + (16 << 20)
    return pl.pallas_call(
        functools.partial(_attn_prompt_kernel, n_kv=n_kv, group=n_q // n_kv),
        grid=(B, nb),
        in_specs=[
            pl.BlockSpec(memory_space=pltpu.SMEM),
            pl.BlockSpec((W, QW), lambda b, i: (b * nb + i, 0)),
            pl.BlockSpec((W, KW), lambda b, i: (b * nb + i, kblk)),
            pl.BlockSpec((W, KW), lambda b, i: (b * nb + jnp.maximum(i - 1, 0), kblk)),
            pl.BlockSpec((W, KW), lambda b, i: (b * nb + i, vblk)),
            pl.BlockSpec((W, KW), lambda b, i: (b * nb + jnp.maximum(i - 1, 0), vblk)),
        ],
        out_specs=pl.BlockSpec((W, QW), lambda b, i: (b * nb + i, 0)),
        out_shape=jax.ShapeDtypeStruct((B * L, QW), BF16),
        compiler_params=pltpu.CompilerParams(
            dimension_semantics=("parallel", "parallel"),
            vmem_limit_bytes=_vmem_limit(est)),
        name="swa_prompt",
    )(sinks, p, p, p, p, p)


def _attn_sample_kernel(sinks_ref, qkv_ref, ck_ref, cv_ref, o_ref, *, n_q, n_kv, n_new):
    R = qkv_ref.shape[1]
    W = WINDOW
    HD = HEAD_DIM
    group = n_q // n_kv
    npair = group // 2
    QW = n_q * HD
    KW = n_kv * HD
    RR = npair * R
    scale = HD ** -0.5
    row = lax.broadcasted_iota(jnp.int32, (RR, 2 * W), 0)
    r = row & (R - 1)
    j = lax.broadcasted_iota(jnp.int32, (RR, 2 * W), 1)
    mask = (j >= r) & (j <= r + W) & (j < W + n_new)
    pair_of_row = lax.broadcasted_iota(jnp.int32, (RR, 1), 0) // R
    first = lax.broadcasted_iota(jnp.int32, (2 * W, LANES), 1) < HD
    zpad = jnp.zeros((W - R, LANES), F32)

    s_all, v_all = [], []
    for c in range(n_kv // 2):
        cs = slice(c * LANES, (c + 1) * LANES)
        kn = qkv_ref[0, :, QW + c * LANES:QW + (c + 1) * LANES]
        vn = qkv_ref[0, :, QW + KW + c * LANES:QW + KW + (c + 1) * LANES]
        ktile = jnp.concatenate([ck_ref[0, :, cs], kn, zpad], axis=0)
        vtile = jnp.concatenate([cv_ref[0, :, cs], vn, zpad], axis=0)
        kswap = pltpu.roll(ktile, HD, 1)
        vswap = pltpu.roll(vtile, HD, 1)
        for par in range(2):
            kh = 2 * c + par
            k_lo, k_hi = (ktile, kswap) if par == 0 else (kswap, ktile)
            v_lo, v_hi = (vtile, vswap) if par == 0 else (vswap, vtile)
            kcat = jnp.concatenate([jnp.where(first, k_lo, 0.0), jnp.where(first, 0.0, k_hi)], axis=0)
            vcat = jnp.concatenate([jnp.where(first, v_lo, 0.0), jnp.where(first, 0.0, v_hi)], axis=0)
            qs = jnp.concatenate(
                [qkv_ref[0, :, (kh * group + 2 * pr) * HD:(kh * group + 2 * pr) * HD + LANES]
                 for pr in range(npair)], axis=0) * scale
            s_all.append(_dot_nt(qs.astype(BF16), kcat.astype(BF16)))
            v_all.append(vcat.astype(BF16))

    p_all = []
    for kh in range(n_kv):
        halves = []
        for hf in range(2):
            sink = jnp.full((RR, 1), sinks_ref[kh * group + hf], F32)
            for pr in range(1, npair):
                sink = jnp.where(pair_of_row == pr, sinks_ref[kh * group + 2 * pr + hf], sink)
            sm = jnp.where(mask, s_all[kh][:, hf * 2 * W:(hf + 1) * 2 * W], -jnp.inf)
            (p,) = _sink_softmax([sm], sink)
            halves.append(p.astype(BF16))
        p_all.append(jnp.concatenate(halves, axis=1))

    for kh in range(n_kv):
        o = _dot(p_all[kh], v_all[kh])
        for pr in range(npair):
            c0 = (kh * group + 2 * pr) * HD
            o_ref[0, :, c0:c0 + LANES] = o[pr * R:(pr + 1) * R]


def _attn_sample(qkv, ck, cv, layer, sinks, n_q, n_kv, n_new):
    B, R, NW = qkv.shape
    QW = n_q * HEAD_DIM
    KW = n_kv * HEAD_DIM
    return pl.pallas_call(
        functools.partial(_attn_sample_kernel, n_q=n_q, n_kv=n_kv, n_new=n_new),
        grid=(B,),
        in_specs=[
            pl.BlockSpec(memory_space=pltpu.SMEM),
            pl.BlockSpec((1, R, NW), lambda b: (b, 0, 0)),
            pl.BlockSpec((None, 1, WINDOW, KW), lambda b: (layer, b, 0, 0)),
            pl.BlockSpec((None, 1, WINDOW, KW), lambda b: (layer, b, 0, 0)),
        ],
        out_specs=pl.BlockSpec((1, R, QW), lambda b: (b, 0, 0)),
        out_shape=jax.ShapeDtypeStruct((B, R, QW), F32),
        compiler_params=pltpu.CompilerParams(dimension_semantics=("parallel",)),
        name="swa_sample",
    )(sinks, qkv, ck, cv)


def _rope_tables(pos):
    half = HEAD_DIM // 2
    inv = jnp.power(ROPE_THETA, -jnp.arange(half, dtype=F32) * (2.0 / HEAD_DIM))
    ang = pos.astype(F32)[:, None] * inv[None, :]
    cos = jnp.cos(ang)
    sin = jnp.sin(ang)
    reps = LANES // HEAD_DIM
    cos_t = jnp.tile(jnp.concatenate([cos, cos], axis=1), (1, reps))
    sin_t = jnp.tile(jnp.concatenate([-sin, sin], axis=1), (1, reps))
    return cos_t, sin_t


def kernel(x_prompt, x_sample, state_hgrn, cache_k_win, cache_v_win, norm_gains, w_ffn_in, w_ffn_out,
           w_hgrn_in, hgrn_lb_logits, hgrn_norm_gain, w_hgrn_out, w_attn_in, attn_sinks, w_attn_out):
    B, L, D = x_prompt.shape
    SB, SL, _ = x_sample.shape
    depth = norm_gains.shape[0]
    TP, TS = B * L, SB * SL
    H = hgrn_norm_gain.shape[1] // HG_DV
    n_kv = cache_k_win.shape[3]
    n_q = w_attn_out.shape[1] // HEAD_DIM
    QW, KW = n_q * HEAD_DIM, n_kv * HEAD_DIM
    buf = cache_k_win.shape[2]

    x = jnp.concatenate([x_prompt.reshape(TP, D), x_sample.reshape(TS, D)], axis=0)

    lb_all = jnp.cumsum(jax.nn.softmax(hgrn_lb_logits.astype(F32), axis=0), axis=0)
    lb_all = lb_all - lb_all[0:1]

    pos = jnp.concatenate([jnp.tile(jnp.arange(L, dtype=jnp.int32), B),
                           jnp.tile(PAST_LEN + jnp.arange(SL, dtype=jnp.int32), SB)])
    rope = _rope_tables(pos)

    hg_p, hg_s, kp_l, vp_l, ks_l, vs_l = [], [], [], [], [], []
    for layer in range(depth):
        g = norm_gains[layer]
        x = _ffn_half(x, g[0:1], g[1:2], w_ffn_in, w_ffn_out, layer, 0)
        if layer % N_MIXERS == 0:
            a = layer // N_MIXERS
            p = _proj(x, g[2:3], w_hgrn_in, a)
            lb = lb_all[a][None, :]
            lbp = (jnp.log(lb), jnp.log1p(-lb), 1.0 - lb)
            ng = hgrn_norm_gain[a][None, :].astype(F32)
            o_p, s_p = _hgrn_scan(p, B, L, HG_CHUNK, HGRN_HB, HG_CHUNK, lbp, ng, None, BF16)
            ps = jnp.pad(p[TP:].reshape(SB, SL, -1), ((0, 0), (0, SAMPLE_PAD - SL), (0, 0)))
            o_s, s_s = _hgrn_scan(ps.reshape(SB * SAMPLE_PAD, -1), SB, SAMPLE_PAD, SAMPLE_PAD, H, SL,
                                  lbp, ng, (state_hgrn, a), F32)
            o_s = o_s.reshape(SB, SAMPLE_PAD, -1)[:, :SL].reshape(TS, -1).astype(BF16)
            o = jnp.concatenate([o_p, o_s], axis=0)
            x = _outproj(o, w_hgrn_out, x, g[3:4], a)
            hg_p.append(s_p.astype(state_hgrn.dtype))
            hg_s.append(s_s.astype(state_hgrn.dtype))
        else:
            bb = layer // N_MIXERS
            p = _proj(x, g[2:3], w_attn_in, bb, rope=rope, n_rope_cols=QW + KW)
            sinks = attn_sinks[bb].astype(F32)
            o_p = _attn_prompt(p, sinks, B, L, n_q, n_kv)
            pn = p[TP:].reshape(SB, SL, -1)
            ck = cache_k_win.reshape(-1, SB, buf, KW)
            cv = cache_v_win.reshape(-1, SB, buf, KW)
            o_s = _attn_sample(jnp.pad(pn, ((0, 0), (0, 8 - SL), (0, 0))), ck, cv, bb, sinks, n_q, n_kv, SL)
            o = jnp.concatenate([o_p, o_s[:, :SL].reshape(TS, QW).astype(BF16)], axis=0)
            x = _outproj(o, w_attn_out, x, g[3:4], bb)
            pp = p[:TP].reshape(B, L, -1)
            kp_l.append(pp[:, L - buf:, QW:QW + KW].reshape(B, buf, n_kv, HEAD_DIM))
            vp_l.append(pp[:, L - buf:, QW + KW:].reshape(B, buf, n_kv, HEAD_DIM))
            k_new = pn[:, :, QW:QW + KW].reshape(SB, SL, n_kv, HEAD_DIM)
            v_new = pn[:, :, QW + KW:].reshape(SB, SL, n_kv, HEAD_DIM)
            ks_l.append(jnp.concatenate([cache_k_win[bb], k_new], axis=1)[:, -buf:])
            vs_l.append(jnp.concatenate([cache_v_win[bb], v_new], axis=1)[:, -buf:])
        x = _ffn_half(x, g[4:5], g[5:6], w_ffn_in, w_ffn_out, layer, 1)

    yp = x[:TP].reshape(B, L, D)
    ys = x[TP:].reshape(SB, SL, D)
    return (yp, ys, jnp.stack(hg_p), jnp.stack(hg_s), jnp.stack(kp_l), jnp.stack(vp_l),
            jnp.stack(ks_l), jnp.stack(vs_l))
```

```python
import functools
import math

import numpy as np
import jax
import jax.numpy as jnp
from jax import lax
from jax.experimental import pallas as pl
from jax.experimental.pallas import tpu as pltpu

F32 = jnp.float32
BF16 = jnp.bfloat16

NORM_EPS = 1e-6
HEAD_DIM = 64
HG_DK = 128
HG_DV = 128
HG_CHUNK = 64
WINDOW = 128
PAST_LEN = 16384
ROPE_THETA = 10000.0
N_MIXERS = 2

LANES = 128
SUBLANES = 8
VMEM_BYTES_V7X = 64 * 1024 * 1024
VMEM_HEADROOM = 6 * 1024 * 1024

FFN_TM, FFN_TF = 1040, 256
HGRN_PROJ_TILES = (832, 1024)
ATTN_PROJ_TILES = (1040, 512)
OUT_TM = 512
SAMPLE_PAD = 16
HGRN_HB = 4


def _vmem_limit(nbytes):
    return int(min(VMEM_BYTES_V7X - VMEM_HEADROOM, max(nbytes * 5 // 4, 16 * 1024 * 1024)))


def _rms(x, g):
    return x * lax.rsqrt(jnp.mean(x * x, axis=-1, keepdims=True) + NORM_EPS) * g


def _dot(a, b):
    return jnp.dot(a, b, preferred_element_type=F32)


def _dot_nt(a, b):
    return lax.dot_general(a, b, (((1,), (1,)), ((), ())), preferred_element_type=F32)


def _dot_tn(a, b):
    return lax.dot_general(a, b, (((0,), (0,)), ((), ())), preferred_element_type=F32)


def _ffn_kernel(x_ref, gpre_ref, gpost_ref, wg_ref, wu_ref, wo_ref, o_ref, xn_ref, *, nj):
    j = pl.program_id(1)

    @pl.when(j == 0)
    def _():
        xn_ref[...] = _rms(x_ref[...], gpre_ref[...]).astype(BF16)
        o_ref[...] = jnp.zeros_like(o_ref)

    xn = xn_ref[...]
    hg = _dot(xn, wg_ref[...].astype(BF16))
    hu = _dot(xn, wu_ref[...].astype(BF16))
    act = (hg * jax.nn.sigmoid(hg) * hu).astype(BF16)
    o_ref[...] += _dot(act, wo_ref[...].astype(BF16))

    @pl.when(j == nj - 1)
    def _():
        o_ref[...] = x_ref[...] + 0.5 * _rms(o_ref[...], gpost_ref[...])


def _ffn_half(x, g_pre, g_post, w_in, w_out, layer, half):
    T, D = x.shape
    F = w_out.shape[2]
    TM, TF = FFN_TM, FFN_TF
    nj = F // TF
    est = (3 * TM * D * 4 + TM * D * 2 + 2 * 3 * D * TF * 4
           + 3 * D * TF * 2 + 6 * TM * TF * 4)
    return pl.pallas_call(
        functools.partial(_ffn_kernel, nj=nj),
        grid=(T // TM, nj),
        in_specs=[
            pl.BlockSpec((TM, D), lambda i, j: (i, 0), pipeline_mode=pl.Buffered(1)),
            pl.BlockSpec((1, D), lambda i, j: (0, 0)),
            pl.BlockSpec((1, D), lambda i, j: (0, 0)),
            pl.BlockSpec((None, None, D, TF), lambda i, j: (layer, half, 0, j)),
            pl.BlockSpec((None, None, D, TF), lambda i, j: (layer, half, 0, nj + j)),
            pl.BlockSpec((None, None, TF, D), lambda i, j: (layer, half, j, 0)),
        ],
        out_specs=pl.BlockSpec((TM, D), lambda i, j: (i, 0)),
        out_shape=jax.ShapeDtypeStruct((T, D), F32),
        scratch_shapes=[pltpu.VMEM((TM, D), BF16)],
        compiler_params=pltpu.CompilerParams(
            dimension_semantics=("parallel", "arbitrary"),
            vmem_limit_bytes=_vmem_limit(est)),
        name="ffn_half",
    )(x, g_pre, g_post, w_in, w_in, w_out)


def _rope_tile(y, cos, sin_signed):
    lane = lax.broadcasted_iota(jnp.int32, y.shape, 1)
    first_half = (lane & (HEAD_DIM - 1)) < (HEAD_DIM // 2)
    rot = jnp.where(first_half,
                    pltpu.roll(y, LANES - HEAD_DIM // 2, 1),
                    pltpu.roll(y, HEAD_DIM // 2, 1))
    return y * cos + rot * sin_signed


def _proj_kernel(x_ref, g_ref, w_ref, o_ref, xn_ref):
    @pl.when(pl.program_id(1) == 0)
    def _():
        xn_ref[...] = _rms(x_ref[...], g_ref[...]).astype(BF16)

    o_ref[...] = _dot(xn_ref[...], w_ref[...].astype(BF16))


def _proj_rope_kernel(x_ref, g_ref, w_ref, cos_ref, sin_ref, o_ref, xn_ref, *, n_rope_blocks):
    j = pl.program_id(1)

    @pl.when(j == 0)
    def _():
        xn_ref[...] = _rms(x_ref[...], g_ref[...]).astype(BF16)

    y = _dot(xn_ref[...], w_ref[...].astype(BF16))

    @pl.when(j < n_rope_blocks)
    def _():
        cos = cos_ref[...]
        sin = sin_ref[...]
        for c in range(y.shape[1] // LANES):
            cols = slice(c * LANES, (c + 1) * LANES)
            o_ref[:, cols] = _rope_tile(y[:, cols], cos, sin)

    @pl.when(j >= n_rope_blocks)
    def _():
        o_ref[...] = y


def _proj(x, g, w, layer, tiles, rope=None, n_rope_cols=0):
    T, D = x.shape
    N = w.shape[2]
    TM, TN = tiles
    assert T % TM == 0 and N % TN == 0 and n_rope_cols % TN == 0
    est = 2 * TM * D * 4 + TM * D * 2 + 2 * D * TN * 4 + D * TN * 2 + 4 * TM * TN * 4
    in_specs = [
        pl.BlockSpec((TM, D), lambda i, j: (i, 0)),
        pl.BlockSpec((1, D), lambda i, j: (0, 0)),
        pl.BlockSpec((None, D, TN), lambda i, j: (layer, 0, j)),
    ]
    args = [x, g, w]
    if rope is None:
        body = _proj_kernel
    else:
        body = functools.partial(_proj_rope_kernel, n_rope_blocks=n_rope_cols // TN)
        in_specs += [pl.BlockSpec((TM, LANES), lambda i, j: (i, 0)),
                     pl.BlockSpec((TM, LANES), lambda i, j: (i, 0))]
        args += list(rope)
    return pl.pallas_call(
        body,
        grid=(T // TM, N // TN),
        in_specs=in_specs,
        out_specs=pl.BlockSpec((TM, TN), lambda i, j: (i, j)),
        out_shape=jax.ShapeDtypeStruct((T, N), F32),
        scratch_shapes=[pltpu.VMEM((TM, D), BF16)],
        compiler_params=pltpu.CompilerParams(
            dimension_semantics=("parallel", "arbitrary"),
            vmem_limit_bytes=_vmem_limit(est)),
        name="mixer_in_proj",
    )(*args)


def _outproj_kernel(op_ref, os_ref, w_ref, x_ref, g_ref, y_ref, wbf_ref, *, n_prompt_tiles, ts):
    i = pl.program_id(0)

    @pl.when(i == 0)
    def _():
        wbf_ref[...] = w_ref[...].astype(BF16)

    @pl.when(i < n_prompt_tiles)
    def _():
        y_ref[...] = x_ref[...] + _rms(_dot(op_ref[...], wbf_ref[...]), g_ref[...])

    @pl.when(i == n_prompt_tiles)
    def _():
        y_ref[0:ts] = x_ref[0:ts] + _rms(_dot(os_ref[...], wbf_ref[...]), g_ref[...])


def _outproj(o_prompt, o_sample, w, x, g, layer):
    T, D = x.shape
    TP, K = o_prompt.shape
    TS = o_sample.shape[0]
    TM = OUT_TM
    assert TP % TM == 0 and TS <= TM and TS % SUBLANES == 0 and TP + TS == T
    npt = TP // TM
    est = K * D * (4 + 2) + 2 * TM * K * 2 + 2 * TS * K * 2 + 5 * TM * D * 4
    return pl.pallas_call(
        functools.partial(_outproj_kernel, n_prompt_tiles=npt, ts=TS),
        grid=(npt + 1,),
        in_specs=[
            pl.BlockSpec((TM, K), lambda i: (jnp.minimum(i, npt - 1), 0)),
            pl.BlockSpec((TS, K), lambda i: (0, 0)),
            pl.BlockSpec((None, K, D), lambda i: (layer, 0, 0), pipeline_mode=pl.Buffered(1)),
            pl.BlockSpec((TM, D), lambda i: (i, 0)),
            pl.BlockSpec((1, D), lambda i: (0, 0)),
        ],
        out_specs=pl.BlockSpec((TM, D), lambda i: (i, 0)),
        out_shape=jax.ShapeDtypeStruct((T, D), F32),
        scratch_shapes=[pltpu.VMEM((K, D), BF16)],
        compiler_params=pltpu.CompilerParams(
            dimension_semantics=("arbitrary",),
            vmem_limit_bytes=_vmem_limit(est)),
        name="mixer_out_proj",
    )(o_prompt, o_sample, w, x, g)


def _tiles(x):
    return [x[SUBLANES * j:SUBLANES * (j + 1)] for j in range(x.shape[0] // SUBLANES)]


def _bcast_row(t, i):
    return jnp.broadcast_to(t[i:i + 1], t.shape)


def _tile_prefix(t, sub):
    p = t + jnp.where(sub >= 1, pltpu.roll(t, 1, 0), 0.0)
    p = p + jnp.where(sub >= 2, pltpu.roll(p, 2, 0), 0.0)
    return p + jnp.where(sub >= 4, pltpu.roll(p, 4, 0), 0.0)


def _hgrn_gates(fz, loglb, log1m, onem, valid):
    e = jnp.exp(-jnp.abs(fz))
    lsig = jnp.minimum(fz, 0.0) - jnp.log(1.0 + e)
    ct = log1m + lsig
    g = jnp.maximum(loglb, ct) + jnp.log(1.0 + jnp.exp(-jnp.abs(loglb - ct)))
    r = 1.0 / (1.0 + e)
    k = onem * jnp.where(fz > 0.0, e * r, r)
    if valid is not None:
        g = jnp.where(valid, g, 0.0)
        k = jnp.where(valid, k, 0.0)
    return g, k


def _hgrn_level_operands(m, qt, kt, gt, bt, sub):
    nv = len(qt)
    zero = jnp.zeros_like(qt[0])
    qm, km = [], []
    if m >= SUBLANES:
        mv = m // SUBLANES
        for j in range(nv):
            jr = (j // (2 * mv)) * (2 * mv) + mv - 1
            ref = _bcast_row(bt[jr], SUBLANES - 1)
            if j % (2 * mv) >= mv:
                qm.append(qt[j] * jnp.exp(bt[j] - ref))
                km.append(zero)
            else:
                qm.append(zero)
                km.append(kt[j] * jnp.exp(ref - bt[j]))
        return qm, km
    upper = (sub & m) != 0
    for j in range(nv):
        if m == 1:
            e = jnp.exp(gt[j])
            qm.append(jnp.where(upper, qt[j] * e, 0.0))
            km.append(jnp.where(upper, 0.0, kt[j]))
            continue
        if m == 4:
            ref = _bcast_row(bt[j], 3)
        else:
            ref = jnp.where(sub < 4, _bcast_row(bt[j], 1), _bcast_row(bt[j], 5))
        e = jnp.exp(-jnp.abs(bt[j] - ref))
        qm.append(jnp.where(upper, qt[j] * e, 0.0))
        km.append(jnp.where(upper, 0.0, kt[j] * e))
    return qm, km


def _hgrn_chunk_group(loads, params, sts, *, C, valid_len):
    nh = len(loads)
    sub = lax.broadcasted_iota(jnp.int32, (SUBLANES, LANES), 0)
    valid = None
    if valid_len < C:
        valid = lax.broadcasted_iota(jnp.int32, (C, LANES), 0) < valid_len
    ti = lax.broadcasted_iota(jnp.int32, (C, C), 0)
    si = lax.broadcasted_iota(jnp.int32, (C, C), 1)

    gk = [_hgrn_gates(loads[h][1], params[h][0], params[h][1], params[h][2], valid) for h in range(nh)]

    gts, bts = [], []
    for h in range(nh):
        gt = _tiles(gk[h][0])
        bt = []
        for j, t in enumerate(gt):
            p = _tile_prefix(t, sub)
            bt.append(p if j == 0 else p + _bcast_row(bt[j - 1], SUBLANES - 1))
        gts.append(gt)
        bts.append(bt)

    scores = []
    for h in range(nh):
        q, k = loads[h][0], gk[h][1]
        qt, kt = _tiles(q), _tiles(k)
        a = jnp.where(ti == si, _dot_nt(q.astype(BF16), k.astype(BF16)), 0.0)
        m = C // 2
        while m >= 1:
            qm, km = _hgrn_level_operands(m, qt, kt, gts[h], bts[h], sub)
            am = _dot_nt(jnp.concatenate(qm, axis=0).astype(BF16), jnp.concatenate(km, axis=0).astype(BF16))
            if 2 * m < C:
                am = jnp.where((ti & (-2 * m)) == (si & (-2 * m)), am, 0.0)
            a = a + am
            m //= 2
        scores.append(a)

    outs = []
    for h in range(nh):
        q, _, v, gz = loads[h]
        k = gk[h][1]
        ng = params[h][3]
        st = sts[h]
        b = jnp.concatenate(bts[h], axis=0)
        o = _dot(scores[h].astype(BF16), v.astype(BF16)) + _dot_nt((q * jnp.exp(b)).astype(BF16), st.astype(BF16))
        bl = b[C - 1:C]
        kb = (k * jnp.exp(bl - b)).astype(BF16)
        st_new = st * jnp.exp(bl) + _dot_tn(v.astype(BF16), kb)
        on = o * lax.rsqrt(jnp.mean(o * o, axis=-1, keepdims=True) + NORM_EPS) * ng
        outs.append((on * (gz * jax.nn.sigmoid(gz)), st_new))
    return outs


def _hgrn_kernel(*refs, C, NC, HB, valid_len, has_s0):
    if has_s0:
        (q_ref, f_ref, i_ref, g_ref, loglb_ref, log1m_ref, onem_ref, ng_ref, s0_ref,
         o_ref, s_ref, st_ref) = refs
    else:
        (q_ref, f_ref, i_ref, g_ref, loglb_ref, log1m_ref, onem_ref, ng_ref,
         o_ref, s_ref, st_ref) = refs
        s0_ref = None

    for h in range(HB):
        if has_s0:
            st_ref[h] = s0_ref[0, h].T
        else:
            st_ref[h] = jnp.zeros((HG_DV, HG_DK), F32)

    def body(c, carry):
        rows = pl.ds(pl.multiple_of(c * C, C), C)
        cols = [slice(h * LANES, (h + 1) * LANES) for h in range(HB)]
        loads = [(q_ref[rows, cs], f_ref[rows, cs], i_ref[rows, cs], g_ref[rows, cs]) for cs in cols]
        params = [(loglb_ref[:, cs], log1m_ref[:, cs], onem_ref[:, cs], ng_ref[:, cs]) for cs in cols]
        outs = _hgrn_chunk_group(loads, params, [st_ref[h] for h in range(HB)], C=C, valid_len=valid_len)
        for h in range(HB):
            st_ref[h] = outs[h][1]
            o_ref[rows, cols[h]] = outs[h][0].astype(o_ref.dtype)
        return carry

    if NC == 1:
        body(0, 0)
    else:
        lax.fori_loop(0, NC, body, 0)

    for h in range(HB):
        s_ref[0, h] = st_ref[h].T


def _hgrn_scan(p, nseq, L, C, HB, valid_len, lbp, ng, s0, out_dtype):
    H = ng.shape[1] // LANES
    nhb = H // HB
    W = HB * LANES
    NC = L // C
    loglb, log1m, onem = lbp
    has_s0 = s0 is not None

    def sect(s):
        return pl.BlockSpec((L, W), lambda b, hb, s=s: (b, s * nhb + hb))

    vec = pl.BlockSpec((1, W), lambda b, hb: (0, hb))
    in_specs = [sect(0), sect(1), sect(2), sect(3), vec, vec, vec, vec]
    args = [p, p, p, p, loglb, log1m, onem, ng]
    if has_s0:
        s0_all, s0_layer = s0
        in_specs.append(pl.BlockSpec((None, 1, HB, HG_DK, HG_DV), lambda b, hb: (s0_layer, b, hb, 0, 0)))
        args.append(s0_all)
    est = 2 * 4 * L * W * 4 + 2 * L * W * 4 + 6 * HB * HG_DK * HG_DV * 4 + (8 << 20)
    return pl.pallas_call(
        functools.partial(_hgrn_kernel, C=C, NC=NC, HB=HB, valid_len=valid_len, has_s0=has_s0),
        grid=(nseq, nhb),
        in_specs=in_specs,
        out_specs=[pl.BlockSpec((L, W), lambda b, hb: (b, hb)),
                   pl.BlockSpec((1, HB, HG_DK, HG_DV), lambda b, hb: (b, hb, 0, 0))],
        out_shape=[jax.ShapeDtypeStruct((nseq * L, H * LANES), out_dtype),
                   jax.ShapeDtypeStruct((nseq, H, HG_DK, HG_DV), F32)],
        scratch_shapes=[pltpu.VMEM((HB, HG_DV, HG_DK), F32)],
        compiler_params=pltpu.CompilerParams(
            dimension_semantics=("parallel", "parallel"),
            vmem_limit_bytes=_vmem_limit(est)),
        name="hgrn_scan",
    )(*args)


def _sink_softmax(scores, sink):
    m = sink
    for s in scores:
        m = jnp.maximum(m, jnp.max(s, axis=-1, keepdims=True))
    ps = [jnp.exp(s - m) for s in scores]
    denom = jnp.exp(sink - m)
    for p in ps:
        denom = denom + jnp.sum(p, axis=-1, keepdims=True)
    return [p / denom for p in ps]


def _attn_prompt_kernel(sinks_ref, q_ref, kc_ref, kp_ref, vc_ref, vp_ref, o_ref, *, n_kv, group):
    i = pl.program_id(1)
    W = WINDOW
    HD = HEAD_DIM
    j = lax.broadcasted_iota(jnp.int32, (2 * W, W), 0)
    r = lax.broadcasted_iota(jnp.int32, (2 * W, W), 1)
    mask = (j >= jnp.maximum(r, jnp.where(i > 0, 0, W))) & (j <= r + W)
    scale = HD ** -0.5
    zeros = jnp.zeros((HD, W), F32)

    s_all, vt_all = [], []
    for c in range(n_kv // 2):
        cs = slice(c * LANES, (c + 1) * LANES)
        ktile = jnp.concatenate([kp_ref[:, cs], kc_ref[:, cs]], axis=0).astype(BF16)
        vt_all.append(jnp.concatenate([vp_ref[:, cs], vc_ref[:, cs]], axis=0).T.astype(BF16))
        for par in range(2):
            kh = 2 * c + par
            rhs = []
            for pr in range(group // 2):
                c0 = (kh * group + 2 * pr) * HD
                pt = (q_ref[:, c0:c0 + LANES] * scale).T
                for half in range(2):
                    blk = pt[half * HD:(half + 1) * HD]
                    rhs.append(jnp.concatenate([blk, zeros] if par == 0 else [zeros, blk], axis=0))
            s_all.append(_dot(ktile, jnp.concatenate(rhs, axis=1).astype(BF16)))

    p_all = []
    for kh in range(n_kv):
        pn = []
        for hh in range(group):
            sink = sinks_ref[kh * group + hh]
            sm = jnp.where(mask, s_all[kh][:, hh * W:(hh + 1) * W], -jnp.inf)
            mx = jnp.maximum(jnp.max(sm, axis=0, keepdims=True), sink)
            p = jnp.exp(sm - mx)
            den = jnp.sum(p, axis=0, keepdims=True) + jnp.exp(sink - mx)
            pn.append((p * (1.0 / den)).astype(BF16))
        p_all.append(jnp.concatenate(pn, axis=1))

    for kh in range(n_kv):
        par = kh % 2
        o_t = _dot(vt_all[kh // 2], p_all[kh])[par * HD:(par + 1) * HD]
        for pr in range(group // 2):
            pair_t = jnp.concatenate([o_t[:, (2 * pr) * W:(2 * pr + 1) * W],
                                      o_t[:, (2 * pr + 1) * W:(2 * pr + 2) * W]], axis=0)
            c0 = (kh * group + 2 * pr) * HD
            o_ref[:, c0:c0 + LANES] = pair_t.T.astype(o_ref.dtype)


def _attn_prompt(p, sinks, B, L, n_q, n_kv):
    W = WINDOW
    nb = L // W
    QW = n_q * HEAD_DIM
    KW = n_kv * HEAD_DIM
    kblk = QW // KW
    vblk = (QW + KW) // KW
    est = 2 * (W * QW * 4 + 4 * W * KW * 4 + W * QW * 2) + (16 << 20)
    return pl.pallas_call(
        functools.partial(_attn_prompt_kernel, n_kv=n_kv, group=n_q // n_kv),
        grid=(B, nb),
        in_specs=[
            pl.BlockSpec(memory_space=pltpu.SMEM),
            pl.BlockSpec((W, QW), lambda b, i: (b * nb + i, 0)),
            pl.BlockSpec((W, KW), lambda b, i: (b * nb + i, kblk)),
            pl.BlockSpec((W, KW), lambda b, i: (b * nb + jnp.maximum(i - 1, 0), kblk)),
            pl.BlockSpec((W, KW), lambda b, i: (b * nb + i, vblk)),
            pl.BlockSpec((W, KW), lambda b, i: (b * nb + jnp.maximum(i - 1, 0), vblk)),
        ],
        out_specs=pl.BlockSpec((W, QW), lambda b, i: (b * nb + i, 0)),
        out_shape=jax.ShapeDtypeStruct((B * L, QW), BF16),
        compiler_params=pltpu.CompilerParams(
            dimension_semantics=("parallel", "parallel"),
            vmem_limit_bytes=_vmem_limit(est)),
        name="swa_prompt",
    )(sinks, p, p, p, p, p)


def _attn_sample_kernel(sinks_ref, qkv_ref, ck_ref, cv_ref, o_ref, *, n_q, n_kv, n_new):
    R = qkv_ref.shape[1]
    W = WINDOW
    HD = HEAD_DIM
    group = n_q // n_kv
    npair = group // 2
    QW = n_q * HD
    KW = n_kv * HD
    RR = npair * R
    scale = HD ** -0.5
    row = lax.broadcasted_iota(jnp.int32, (RR, 2 * W), 0)
    r = row & (R - 1)
    j = lax.broadcasted_iota(jnp.int32, (RR, 2 * W), 1)
    mask = (j >= r) & (j <= r + W) & (j < W + n_new)
    pair_of_row = lax.broadcasted_iota(jnp.int32, (RR, 1), 0) // R
    first = lax.broadcasted_iota(jnp.int32, (2 * W, LANES), 1) < HD
    zpad = jnp.zeros((W - R, LANES), F32)

    s_all, v_all = [], []
    for c in range(n_kv // 2):
        cs = slice(c * LANES, (c + 1) * LANES)
        kn = qkv_ref[0, :, QW + c * LANES:QW + (c + 1) * LANES]
        vn = qkv_ref[0, :, QW + KW + c * LANES:QW + KW + (c + 1) * LANES]
        ktile = jnp.concatenate([ck_ref[0, :, cs], kn, zpad], axis=0)
        vtile = jnp.concatenate([cv_ref[0, :, cs], vn, zpad], axis=0)
        kswap = pltpu.roll(ktile, HD, 1)
        vswap = pltpu.roll(vtile, HD, 1)
        for par in range(2):
            kh = 2 * c + par
            k_lo, k_hi = (ktile, kswap) if par == 0 else (kswap, ktile)
            v_lo, v_hi = (vtile, vswap) if par == 0 else (vswap, vtile)
            kcat = jnp.concatenate([jnp.where(first, k_lo, 0.0), jnp.where(first, 0.0, k_hi)], axis=0)
            vcat = jnp.concatenate([jnp.where(first, v_lo, 0.0), jnp.where(first, 0.0, v_hi)], axis=0)
            qs = jnp.concatenate(
                [qkv_ref[0, :, (kh * group + 2 * pr) * HD:(kh * group + 2 * pr) * HD + LANES]
                 for pr in range(npair)], axis=0) * scale
            s_all.append(_dot_nt(qs.astype(BF16), kcat.astype(BF16)))
            v_all.append(vcat.astype(BF16))

    p_all = []
    for kh in range(n_kv):
        halves = []
        for hf in range(2):
            sink = jnp.full((RR, 1), sinks_ref[kh * group + hf], F32)
            for pr in range(1, npair):
                sink = jnp.where(pair_of_row == pr, sinks_ref[kh * group + 2 * pr + hf], sink)
            sm = jnp.where(mask, s_all[kh][:, hf * 2 * W:(hf + 1) * 2 * W], -jnp.inf)
            (p,) = _sink_softmax([sm], sink)
            halves.append(p.astype(BF16))
        p_all.append(jnp.concatenate(halves, axis=1))

    for kh in range(n_kv):
        o = _dot(p_all[kh], v_all[kh])
        for pr in range(npair):
            c0 = (kh * group + 2 * pr) * HD
            o_ref[0, :, c0:c0 + LANES] = o[pr * R:(pr + 1) * R]


def _attn_sample(qkv, ck, cv, layer, sinks, n_q, n_kv, n_new):
    B, R, NW = qkv.shape
    QW = n_q * HEAD_DIM
    KW = n_kv * HEAD_DIM
    return pl.pallas_call(
        functools.partial(_attn_sample_kernel, n_q=n_q, n_kv=n_kv, n_new=n_new),
        grid=(B,),
        in_specs=[
            pl.BlockSpec(memory_space=pltpu.SMEM),
            pl.BlockSpec((1, R, NW), lambda b: (b, 0, 0)),
            pl.BlockSpec((None, 1, WINDOW, KW), lambda b: (layer, b, 0, 0)),
            pl.BlockSpec((None, 1, WINDOW, KW), lambda b: (layer, b, 0, 0)),
        ],
        out_specs=pl.BlockSpec((1, R, QW), lambda b: (b, 0, 0)),
        out_shape=jax.ShapeDtypeStruct((B, R, QW), F32),
        compiler_params=pltpu.CompilerParams(dimension_semantics=("parallel",)),
        name="swa_sample",
    )(sinks, qkv, ck, cv)


def _rope_tables(pos):
    half = HEAD_DIM // 2
    inv = jnp.power(ROPE_THETA, -jnp.arange(half, dtype=F32) * (2.0 / HEAD_DIM))
    ang = pos.astype(F32)[:, None] * inv[None, :]
    cos = jnp.cos(ang)
    sin = jnp.sin(ang)
    reps = LANES // HEAD_DIM
    cos_t = jnp.tile(jnp.concatenate([cos, cos], axis=1), (1, reps))
    sin_t = jnp.tile(jnp.concatenate([-sin, sin], axis=1), (1, reps))
    return cos_t, sin_t


def kernel(x_prompt, x_sample, state_hgrn, cache_k_win, cache_v_win, norm_gains, w_ffn_in, w_ffn_out,
           w_hgrn_in, hgrn_lb_logits, hgrn_norm_gain, w_hgrn_out, w_attn_in, attn_sinks, w_attn_out):
    B, L, D = x_prompt.shape
    SB, SL, _ = x_sample.shape
    depth = norm_gains.shape[0]
    TP, TS = B * L, SB * SL
    H = hgrn_norm_gain.shape[1] // HG_DV
    n_kv = cache_k_win.shape[3]
    n_q = w_attn_out.shape[1] // HEAD_DIM
    QW, KW = n_q * HEAD_DIM, n_kv * HEAD_DIM
    buf = cache_k_win.shape[2]

    x = jnp.concatenate([x_prompt.reshape(TP, D), x_sample.reshape(TS, D)], axis=0)

    lb_all = jnp.cumsum(jax.nn.softmax(hgrn_lb_logits.astype(F32), axis=0), axis=0)
    lb_all = lb_all - lb_all[0:1]

    pos = jnp.concatenate([jnp.tile(jnp.arange(L, dtype=jnp.int32), B),
                           jnp.tile(PAST_LEN + jnp.arange(SL, dtype=jnp.int32), SB)])
    rope = _rope_tables(pos)

    hg_p, hg_s, kp_l, vp_l, ks_l, vs_l = [], [], [], [], [], []
    for layer in range(depth):
        g = norm_gains[layer]
        x = _ffn_half(x, g[0:1], g[1:2], w_ffn_in, w_ffn_out, layer, 0)
        if layer % N_MIXERS == 0:
            a = layer // N_MIXERS
            p = _proj(x, g[2:3], w_hgrn_in, a, HGRN_PROJ_TILES)
            lb = lb_all[a][None, :]
            lbp = (jnp.log(lb), jnp.log1p(-lb), 1.0 - lb)
            ng = hgrn_norm_gain[a][None, :].astype(F32)
            o_p, s_p = _hgrn_scan(p, B, L, HG_CHUNK, HGRN_HB, HG_CHUNK, lbp, ng, None, BF16)
            ps = jnp.pad(p[TP:].reshape(SB, SL, -1), ((0, 0), (0, SAMPLE_PAD - SL), (0, 0)))
            o_s, s_s = _hgrn_scan(ps.reshape(SB * SAMPLE_PAD, -1), SB, SAMPLE_PAD, SAMPLE_PAD, H, SL,
                                  lbp, ng, (state_hgrn, a), F32)
            o_s = o_s.reshape(SB, SAMPLE_PAD, -1)[:, :SL].reshape(TS, -1).astype(BF16)
            x = _outproj(o_p, o_s, w_hgrn_out, x, g[3:4], a)
            hg_p.append(s_p.astype(state_hgrn.dtype))
            hg_s.append(s_s.astype(state_hgrn.dtype))
        else:
            bb = layer // N_MIXERS
            p = _proj(x, g[2:3], w_attn_in, bb, ATTN_PROJ_TILES, rope=rope, n_rope_cols=QW + KW)
            sinks = attn_sinks[bb].astype(F32)
            o_p = _attn_prompt(p, sinks, B, L, n_q, n_kv)
            pn = p[TP:].reshape(SB, SL, -1)
            ck = cache_k_win.reshape(-1, SB, buf, KW)
            cv = cache_v_win.reshape(-1, SB, buf, KW)
            o_s = _attn_sample(jnp.pad(pn, ((0, 0), (0, 8 - SL), (0, 0))), ck, cv, bb, sinks, n_q, n_kv, SL)
            x = _outproj(o_p, o_s[:, :SL].reshape(TS, QW).astype(BF16), w_attn_out, x, g[3:4], bb)
            tails = [p[(s + 1) * L - buf:(s + 1) * L, QW:] for s in range(B)]
            kp_l.append(jnp.stack([t[:, :KW] for t in tails]).reshape(B, buf, n_kv, HEAD_DIM))
            vp_l.append(jnp.stack([t[:, KW:] for t in tails]).reshape(B, buf, n_kv, HEAD_DIM))
            k_new = pn[:, :, QW:QW + KW].reshape(SB, SL, n_kv, HEAD_DIM)
            v_new = pn[:, :, QW + KW:].reshape(SB, SL, n_kv, HEAD_DIM)
            ks_l.append(jnp.concatenate([cache_k_win[bb], k_new], axis=1)[:, -buf:])
            vs_l.append(jnp.concatenate([cache_v_win[bb], v_new], axis=1)[:, -buf:])
        x = _ffn_half(x, g[4:5], g[5:6], w_ffn_in, w_ffn_out, layer, 1)

    yp = x[:TP].reshape(B, L, D)
    ys = x[TP:].reshape(SB, SL, D)
    return (yp, ys, jnp.stack(hg_p), jnp.stack(hg_s), jnp.stack(kp_l), jnp.stack(vp_l),
            jnp.stack(ks_l), jnp.stack(vs_l))
```

```python
import functools
import math

import numpy as np
import jax
import jax.numpy as jnp
from jax import lax
from jax.experimental import pallas as pl
from jax.experimental.pallas import tpu as pltpu

F32 = jnp.float32
BF16 = jnp.bfloat16

NORM_EPS = 1e-6
HEAD_DIM = 64
HG_DK = 128
HG_DV = 128
HG_CHUNK = 64
WINDOW = 128
PAST_LEN = 16384
ROPE_THETA = 10000.0
N_MIXERS = 2

LANES = 128
SUBLANES = 8
VMEM_BYTES_V7X = 64 * 1024 * 1024
VMEM_HEADROOM = 6 * 1024 * 1024

FFN_TM, FFN_TF = 1040, 256
HGRN_PROJ_TILES = (832, 1024)
ATTN_PROJ_TILES = (1040, 512)
OUT_TM = 512
SAMPLE_PAD = 16
HGRN_HB = 4


def _vmem_limit(nbytes):
    return int(min(VMEM_BYTES_V7X - VMEM_HEADROOM, max(nbytes * 5 // 4, 16 * 1024 * 1024)))


def _rms(x, g):
    return x * lax.rsqrt(jnp.mean(x * x, axis=-1, keepdims=True) + NORM_EPS) * g


def _dot(a, b):
    return jnp.dot(a, b, preferred_element_type=F32)


def _dot_nt(a, b):
    return lax.dot_general(a, b, (((1,), (1,)), ((), ())), preferred_element_type=F32)


def _dot_tn(a, b):
    return lax.dot_general(a, b, (((0,), (0,)), ((), ())), preferred_element_type=F32)


def _ffn_kernel(x_ref, gpre_ref, gpost_ref, wg_ref, wu_ref, wo_ref, o_ref, xn_ref, *, nj):
    j = pl.program_id(1)

    @pl.when(j == 0)
    def _():
        xn_ref[...] = _rms(x_ref[...], gpre_ref[...]).astype(BF16)
        o_ref[...] = jnp.zeros_like(o_ref)

    xn = xn_ref[...]
    hg = _dot(xn, wg_ref[...].astype(BF16))
    hu = _dot(xn, wu_ref[...].astype(BF16))
    act = (hg * jax.nn.sigmoid(hg) * hu).astype(BF16)
    o_ref[...] += _dot(act, wo_ref[...].astype(BF16))

    @pl.when(j == nj - 1)
    def _():
        o_ref[...] = x_ref[...] + 0.5 * _rms(o_ref[...], gpost_ref[...])


def _ffn_half(x, g_pre, g_post, w_in, w_out, layer, half):
    T, D = x.shape
    F = w_out.shape[2]
    TM, TF = FFN_TM, FFN_TF
    nj = F // TF
    est = (3 * TM * D * 4 + TM * D * 2 + 2 * 3 * D * TF * 4
           + 3 * D * TF * 2 + 6 * TM * TF * 4)
    return pl.pallas_call(
        functools.partial(_ffn_kernel, nj=nj),
        grid=(T // TM, nj),
        in_specs=[
            pl.BlockSpec((TM, D), lambda i, j: (i, 0), pipeline_mode=pl.Buffered(1)),
            pl.BlockSpec((1, D), lambda i, j: (0, 0)),
            pl.BlockSpec((1, D), lambda i, j: (0, 0)),
            pl.BlockSpec((None, None, D, TF), lambda i, j: (layer, half, 0, j)),
            pl.BlockSpec((None, None, D, TF), lambda i, j: (layer, half, 0, nj + j)),
            pl.BlockSpec((None, None, TF, D), lambda i, j: (layer, half, j, 0)),
        ],
        out_specs=pl.BlockSpec((TM, D), lambda i, j: (i, 0)),
        out_shape=jax.ShapeDtypeStruct((T, D), F32),
        scratch_shapes=[pltpu.VMEM((TM, D), BF16)],
        compiler_params=pltpu.CompilerParams(
            dimension_semantics=("parallel", "arbitrary"),
            vmem_limit_bytes=_vmem_limit(est)),
        name="ffn_half",
    )(x, g_pre, g_post, w_in, w_in, w_out)


def _rope_tile(y, cos, sin_signed):
    lane = lax.broadcasted_iota(jnp.int32, y.shape, 1)
    first_half = (lane & (HEAD_DIM - 1)) < (HEAD_DIM // 2)
    rot = jnp.where(first_half,
                    pltpu.roll(y, LANES - HEAD_DIM // 2, 1),
                    pltpu.roll(y, HEAD_DIM // 2, 1))
    return y * cos + rot * sin_signed


def _proj_kernel(x_ref, g_ref, w_ref, o_ref, xn_ref):
    @pl.when(pl.program_id(1) == 0)
    def _():
        xn_ref[...] = _rms(x_ref[...], g_ref[...]).astype(BF16)

    o_ref[...] = _dot(xn_ref[...], w_ref[...].astype(BF16))


def _proj_rope_kernel(x_ref, g_ref, w_ref, cos_ref, sin_ref, o_ref, xn_ref):
    @pl.when(pl.program_id(1) == 0)
    def _():
        xn_ref[...] = _rms(x_ref[...], g_ref[...]).astype(BF16)

    y = _dot(xn_ref[...], w_ref[...].astype(BF16))
    cos = cos_ref[...]
    sin = sin_ref[...]
    for c in range(y.shape[1] // LANES):
        cols = slice(c * LANES, (c + 1) * LANES)
        o_ref[:, cols] = _rope_tile(y[:, cols], cos, sin)


def _proj(x, g, w, layer, tiles, rope=None, n_rope_cols=0):
    T, D = x.shape
    N = w.shape[2]
    TM, TN = tiles
    assert T % TM == 0 and N % TN == 0 and n_rope_cols % TN == 0
    est = 2 * TM * D * 4 + TM * D * 2 + 2 * D * TN * 4 + D * TN * 2 + 4 * TM * TN * 4
    in_specs = [
        pl.BlockSpec((TM, D), lambda i, j: (i, 0)),
        pl.BlockSpec((1, D), lambda i, j: (0, 0)),
        pl.BlockSpec((None, D, TN), lambda i, j: (layer, 0, j)),
    ]
    args = [x, g, w]
    if rope is None:
        body = _proj_kernel
    else:
        body = _proj_rope_kernel
        n_rope_blocks = n_rope_cols // TN
        table = pl.BlockSpec((None, TM, LANES), lambda i, j: (jnp.where(j < n_rope_blocks, 0, 1), i, 0))
        in_specs += [table, table]
        args += list(rope)
    return pl.pallas_call(
        body,
        grid=(T // TM, N // TN),
        in_specs=in_specs,
        out_specs=pl.BlockSpec((TM, TN), lambda i, j: (i, j)),
        out_shape=jax.ShapeDtypeStruct((T, N), F32),
        scratch_shapes=[pltpu.VMEM((TM, D), BF16)],
        compiler_params=pltpu.CompilerParams(
            dimension_semantics=("parallel", "arbitrary"),
            vmem_limit_bytes=_vmem_limit(est)),
        name="mixer_in_proj",
    )(*args)


def _outproj_kernel(op_ref, os_ref, w_ref, x_ref, g_ref, y_ref, wbf_ref, *, n_prompt_tiles, ts):
    i = pl.program_id(0)

    @pl.when(i == 0)
    def _():
        wbf_ref[...] = w_ref[...].astype(BF16)

    @pl.when(i < n_prompt_tiles)
    def _():
        y_ref[...] = x_ref[...] + _rms(_dot(op_ref[...], wbf_ref[...]), g_ref[...])

    @pl.when(i == n_prompt_tiles)
    def _():
        y_ref[0:ts] = x_ref[0:ts] + _rms(_dot(os_ref[...], wbf_ref[...]), g_ref[...])


def _outproj(o_prompt, o_sample, w, x, g, layer):
    T, D = x.shape
    TP, K = o_prompt.shape
    TS = o_sample.shape[0]
    TM = OUT_TM
    assert TP % TM == 0 and TS <= TM and TS % SUBLANES == 0 and TP + TS == T
    npt = TP // TM
    est = K * D * (4 + 2) + 2 * TM * K * 2 + 2 * TS * K * 2 + 5 * TM * D * 4
    return pl.pallas_call(
        functools.partial(_outproj_kernel, n_prompt_tiles=npt, ts=TS),
        grid=(npt + 1,),
        in_specs=[
            pl.BlockSpec((TM, K), lambda i: (jnp.minimum(i, npt - 1), 0)),
            pl.BlockSpec((TS, K), lambda i: (0, 0)),
            pl.BlockSpec((None, K, D), lambda i: (layer, 0, 0), pipeline_mode=pl.Buffered(1)),
            pl.BlockSpec((TM, D), lambda i: (i, 0)),
            pl.BlockSpec((1, D), lambda i: (0, 0)),
        ],
        out_specs=pl.BlockSpec((TM, D), lambda i: (i, 0)),
        out_shape=jax.ShapeDtypeStruct((T, D), F32),
        scratch_shapes=[pltpu.VMEM((K, D), BF16)],
        compiler_params=pltpu.CompilerParams(
            dimension_semantics=("arbitrary",),
            vmem_limit_bytes=_vmem_limit(est)),
        name="mixer_out_proj",
    )(o_prompt, o_sample, w, x, g)


def _tiles(x):
    return [x[SUBLANES * j:SUBLANES * (j + 1)] for j in range(x.shape[0] // SUBLANES)]


def _bcast_row(t, i):
    return jnp.broadcast_to(t[i:i + 1], t.shape)


def _tile_prefix(t, sub):
    p = t + jnp.where(sub >= 1, pltpu.roll(t, 1, 0), 0.0)
    p = p + jnp.where(sub >= 2, pltpu.roll(p, 2, 0), 0.0)
    return p + jnp.where(sub >= 4, pltpu.roll(p, 4, 0), 0.0)


def _hgrn_gates(fz, loglb, log1m, onem, valid):
    e = jnp.exp(-jnp.abs(fz))
    lsig = jnp.minimum(fz, 0.0) - jnp.log(1.0 + e)
    ct = log1m + lsig
    g = jnp.maximum(loglb, ct) + jnp.log(1.0 + jnp.exp(-jnp.abs(loglb - ct)))
    r = 1.0 / (1.0 + e)
    k = onem * jnp.where(fz > 0.0, e * r, r)
    if valid is not None:
        g = jnp.where(valid, g, 0.0)
        k = jnp.where(valid, k, 0.0)
    return g, k


def _hgrn_level_operands(m, qt, kt, gt, bt, sub):
    nv = len(qt)
    zero = jnp.zeros_like(qt[0])
    qm, km = [], []
    if m >= SUBLANES:
        mv = m // SUBLANES
        for j in range(nv):
            jr = (j // (2 * mv)) * (2 * mv) + mv - 1
            ref = _bcast_row(bt[jr], SUBLANES - 1)
            if j % (2 * mv) >= mv:
                qm.append(qt[j] * jnp.exp(bt[j] - ref))
                km.append(zero)
            else:
                qm.append(zero)
                km.append(kt[j] * jnp.exp(ref - bt[j]))
        return qm, km
    upper = (sub & m) != 0
    for j in range(nv):
        if m == 1:
            e = jnp.exp(gt[j])
            qm.append(jnp.where(upper, qt[j] * e, 0.0))
            km.append(jnp.where(upper, 0.0, kt[j]))
            continue
        if m == 4:
            ref = _bcast_row(bt[j], 3)
        else:
            ref = jnp.where(sub < 4, _bcast_row(bt[j], 1), _bcast_row(bt[j], 5))
        e = jnp.exp(-jnp.abs(bt[j] - ref))
        qm.append(jnp.where(upper, qt[j] * e, 0.0))
        km.append(jnp.where(upper, 0.0, kt[j] * e))
    return qm, km


def _hgrn_chunk_group(loads, params, sts, *, C, valid_len):
    nh = len(loads)
    sub = lax.broadcasted_iota(jnp.int32, (SUBLANES, LANES), 0)
    valid = None
    if valid_len < C:
        valid = lax.broadcasted_iota(jnp.int32, (C, LANES), 0) < valid_len
    ti = lax.broadcasted_iota(jnp.int32, (C, C), 0)
    si = lax.broadcasted_iota(jnp.int32, (C, C), 1)

    gk = [_hgrn_gates(loads[h][1], params[h][0], params[h][1], params[h][2], valid) for h in range(nh)]

    gts, bts = [], []
    for h in range(nh):
        gt = _tiles(gk[h][0])
        bt = []
        for j, t in enumerate(gt):
            p = _tile_prefix(t, sub)
            bt.append(p if j == 0 else p + _bcast_row(bt[j - 1], SUBLANES - 1))
        gts.append(gt)
        bts.append(bt)

    scores = []
    for h in range(nh):
        q, k = loads[h][0], gk[h][1]
        qt, kt = _tiles(q), _tiles(k)
        a = jnp.where(ti == si, _dot_nt(q.astype(BF16), k.astype(BF16)), 0.0)
        m = C // 2
        while m >= 1:
            qm, km = _hgrn_level_operands(m, qt, kt, gts[h], bts[h], sub)
            am = _dot_nt(jnp.concatenate(qm, axis=0).astype(BF16), jnp.concatenate(km, axis=0).astype(BF16))
            if 2 * m < C:
                am = jnp.where((ti & (-2 * m)) == (si & (-2 * m)), am, 0.0)
            a = a + am
            m //= 2
        scores.append(a)

    outs = []
    for h in range(nh):
        q, _, v, gz = loads[h]
        k = gk[h][1]
        ng = params[h][3]
        st = sts[h]
        b = jnp.concatenate(bts[h], axis=0)
        o = _dot(scores[h].astype(BF16), v.astype(BF16)) + _dot_nt((q * jnp.exp(b)).astype(BF16), st.astype(BF16))
        bl = b[C - 1:C]
        kb = (k * jnp.exp(bl - b)).astype(BF16)
        st_new = st * jnp.exp(bl) + _dot_tn(v.astype(BF16), kb)
        on = o * lax.rsqrt(jnp.mean(o * o, axis=-1, keepdims=True) + NORM_EPS) * ng
        outs.append((on * (gz * jax.nn.sigmoid(gz)), st_new))
    return outs


def _hgrn_kernel(*refs, C, NC, HB, valid_len, has_s0):
    if has_s0:
        (q_ref, f_ref, i_ref, g_ref, loglb_ref, log1m_ref, onem_ref, ng_ref, s0_ref,
         o_ref, s_ref, st_ref) = refs
    else:
        (q_ref, f_ref, i_ref, g_ref, loglb_ref, log1m_ref, onem_ref, ng_ref,
         o_ref, s_ref, st_ref) = refs
        s0_ref = None

    for h in range(HB):
        if has_s0:
            st_ref[h] = s0_ref[0, h].T
        else:
            st_ref[h] = jnp.zeros((HG_DV, HG_DK), F32)

    def body(c, carry):
        rows = pl.ds(pl.multiple_of(c * C, C), C)
        cols = [slice(h * LANES, (h + 1) * LANES) for h in range(HB)]
        loads = [(q_ref[rows, cs], f_ref[rows, cs], i_ref[rows, cs], g_ref[rows, cs]) for cs in cols]
        params = [(loglb_ref[:, cs], log1m_ref[:, cs], onem_ref[:, cs], ng_ref[:, cs]) for cs in cols]
        outs = _hgrn_chunk_group(loads, params, [st_ref[h] for h in range(HB)], C=C, valid_len=valid_len)
        for h in range(HB):
            st_ref[h] = outs[h][1]
            o_ref[rows, cols[h]] = outs[h][0].astype(o_ref.dtype)
        return carry

    if NC == 1:
        body(0, 0)
    else:
        lax.fori_loop(0, NC, body, 0)

    for h in range(HB):
        s_ref[0, h] = st_ref[h].T


def _hgrn_scan(p, nseq, L, C, HB, valid_len, lbp, ng, s0, out_dtype):
    H = ng.shape[1] // LANES
    nhb = H // HB
    W = HB * LANES
    NC = L // C
    loglb, log1m, onem = lbp
    has_s0 = s0 is not None

    def sect(s):
        return pl.BlockSpec((L, W), lambda b, hb, s=s: (b, s * nhb + hb))

    vec = pl.BlockSpec((1, W), lambda b, hb: (0, hb))
    in_specs = [sect(0), sect(1), sect(2), sect(3), vec, vec, vec, vec]
    args = [p, p, p, p, loglb, log1m, onem, ng]
    if has_s0:
        s0_all, s0_layer = s0
        in_specs.append(pl.BlockSpec((None, 1, HB, HG_DK, HG_DV), lambda b, hb: (s0_layer, b, hb, 0, 0)))
        args.append(s0_all)
    est = 2 * 4 * L * W * 4 + 2 * L * W * 4 + 6 * HB * HG_DK * HG_DV * 4 + (8 << 20)
    return pl.pallas_call(
        functools.partial(_hgrn_kernel, C=C, NC=NC, HB=HB, valid_len=valid_len, has_s0=has_s0),
        grid=(nseq, nhb),
        in_specs=in_specs,
        out_specs=[pl.BlockSpec((L, W), lambda b, hb: (b, hb)),
                   pl.BlockSpec((1, HB, HG_DK, HG_DV), lambda b, hb: (b, hb, 0, 0))],
        out_shape=[jax.ShapeDtypeStruct((nseq * L, H * LANES), out_dtype),
                   jax.ShapeDtypeStruct((nseq, H, HG_DK, HG_DV), F32)],
        scratch_shapes=[pltpu.VMEM((HB, HG_DV, HG_DK), F32)],
        compiler_params=pltpu.CompilerParams(
            dimension_semantics=("parallel", "parallel"),
            vmem_limit_bytes=_vmem_limit(est)),
        name="hgrn_scan",
    )(*args)


def _sink_softmax(scores, sink):
    m = sink
    for s in scores:
        m = jnp.maximum(m, jnp.max(s, axis=-1, keepdims=True))
    ps = [jnp.exp(s - m) for s in scores]
    denom = jnp.exp(sink - m)
    for p in ps:
        denom = denom + jnp.sum(p, axis=-1, keepdims=True)
    return [p / denom for p in ps]


def _attn_prompt_kernel(sinks_ref, q_ref, kc_ref, kp_ref, vc_ref, vp_ref, o_ref, *, n_kv, group):
    i = pl.program_id(1)
    W = WINDOW
    HD = HEAD_DIM
    j = lax.broadcasted_iota(jnp.int32, (2 * W, W), 0)
    r = lax.broadcasted_iota(jnp.int32, (2 * W, W), 1)
    mask = (j >= jnp.maximum(r, jnp.where(i > 0, 0, W))) & (j <= r + W)
    scale = HD ** -0.5
    zeros = jnp.zeros((HD, W), F32)

    s_all, vt_all = [], []
    for c in range(n_kv // 2):
        cs = slice(c * LANES, (c + 1) * LANES)
        ktile = jnp.concatenate([kp_ref[:, cs], kc_ref[:, cs]], axis=0).astype(BF16)
        vt_all.append(jnp.concatenate([vp_ref[:, cs], vc_ref[:, cs]], axis=0).T.astype(BF16))
        for par in range(2):
            kh = 2 * c + par
            rhs = []
            for pr in range(group // 2):
                c0 = (kh * group + 2 * pr) * HD
                pt = (q_ref[:, c0:c0 + LANES] * scale).T
                for half in range(2):
                    blk = pt[half * HD:(half + 1) * HD]
                    rhs.append(jnp.concatenate([blk, zeros] if par == 0 else [zeros, blk], axis=0))
            s_all.append(_dot(ktile, jnp.concatenate(rhs, axis=1).astype(BF16)))

    p_all, inv_all = [], []
    for kh in range(n_kv):
        pn, inv = [], []
        for hh in range(group):
            sink = sinks_ref[kh * group + hh]
            sm = jnp.where(mask, s_all[kh][:, hh * W:(hh + 1) * W], -jnp.inf)
            mx = jnp.maximum(jnp.max(sm, axis=0, keepdims=True), sink)
            p = jnp.exp(sm - mx)
            inv.append(1.0 / (jnp.sum(p, axis=0, keepdims=True) + jnp.exp(sink - mx)))
            pn.append(p.astype(BF16))
        p_all.append(jnp.concatenate(pn, axis=1))
        inv_all.append(jnp.concatenate(inv, axis=1))

    for kh in range(n_kv):
        par = kh % 2
        o_t = _dot(vt_all[kh // 2], p_all[kh])[par * HD:(par + 1) * HD] * inv_all[kh]
        for pr in range(group // 2):
            pair_t = jnp.concatenate([o_t[:, (2 * pr) * W:(2 * pr + 1) * W],
                                      o_t[:, (2 * pr + 1) * W:(2 * pr + 2) * W]], axis=0)
            c0 = (kh * group + 2 * pr) * HD
            o_ref[:, c0:c0 + LANES] = pair_t.T.astype(o_ref.dtype)


def _attn_prompt(p, sinks, B, L, n_q, n_kv):
    W = WINDOW
    nb = L // W
    QW = n_q * HEAD_DIM
    KW = n_kv * HEAD_DIM
    kblk = QW // KW
    vblk = (QW + KW) // KW
    est = 2 * (W * QW * 4 + 4 * W * KW * 4 + W * QW * 2) + (16 << 20)
    return pl.pallas_call(
        functools.partial(_attn_prompt_kernel, n_kv=n_kv, group=n_q // n_kv),
        grid=(B, nb),
        in_specs=[
            pl.BlockSpec(memory_space=pltpu.SMEM),
            pl.BlockSpec((W, QW), lambda b, i: (b * nb + i, 0)),
            pl.BlockSpec((W, KW), lambda b, i: (b * nb + i, kblk)),
            pl.BlockSpec((W, KW), lambda b, i: (b * nb + jnp.maximum(i - 1, 0), kblk)),
            pl.BlockSpec((W, KW), lambda b, i: (b * nb + i, vblk)),
            pl.BlockSpec((W, KW), lambda b, i: (b * nb + jnp.maximum(i - 1, 0), vblk)),
        ],
        out_specs=pl.BlockSpec((W, QW), lambda b, i: (b * nb + i, 0)),
        out_shape=jax.ShapeDtypeStruct((B * L, QW), BF16),
        compiler_params=pltpu.CompilerParams(
            dimension_semantics=("parallel", "parallel"),
            vmem_limit_bytes=_vmem_limit(est)),
        name="swa_prompt",
    )(sinks, p, p, p, p, p)


def _attn_sample_kernel(sinks_ref, qkv_ref, ck_ref, cv_ref, o_ref, *, n_q, n_kv, n_new):
    R = qkv_ref.shape[1]
    W = WINDOW
    HD = HEAD_DIM
    group = n_q // n_kv
    npair = group // 2
    QW = n_q * HD
    KW = n_kv * HD
    RR = npair * R
    scale = HD ** -0.5
    row = lax.broadcasted_iota(jnp.int32, (RR, 2 * W), 0)
    r = row & (R - 1)
    j = lax.broadcasted_iota(jnp.int32, (RR, 2 * W), 1)
    mask = (j >= r) & (j <= r + W) & (j < W + n_new)
    pair_of_row = lax.broadcasted_iota(jnp.int32, (RR, 1), 0) // R
    first = lax.broadcasted_iota(jnp.int32, (2 * W, LANES), 1) < HD
    zpad = jnp.zeros((W - R, LANES), F32)

    s_all, v_all = [], []
    for c in range(n_kv // 2):
        cs = slice(c * LANES, (c + 1) * LANES)
        kn = qkv_ref[0, :, QW + c * LANES:QW + (c + 1) * LANES]
        vn = qkv_ref[0, :, QW + KW + c * LANES:QW + KW + (c + 1) * LANES]
        ktile = jnp.concatenate([ck_ref[0, :, cs], kn, zpad], axis=0)
        vtile = jnp.concatenate([cv_ref[0, :, cs], vn, zpad], axis=0)
        kswap = pltpu.roll(ktile, HD, 1)
        vswap = pltpu.roll(vtile, HD, 1)
        for par in range(2):
            kh = 2 * c + par
            k_lo, k_hi = (ktile, kswap) if par == 0 else (kswap, ktile)
            v_lo, v_hi = (vtile, vswap) if par == 0 else (vswap, vtile)
            kcat = jnp.concatenate([jnp.where(first, k_lo, 0.0), jnp.where(first, 0.0, k_hi)], axis=0)
            vcat = jnp.concatenate([jnp.where(first, v_lo, 0.0), jnp.where(first, 0.0, v_hi)], axis=0)
            qs = jnp.concatenate(
                [qkv_ref[0, :, (kh * group + 2 * pr) * HD:(kh * group + 2 * pr) * HD + LANES]
                 for pr in range(npair)], axis=0) * scale
            s_all.append(_dot_nt(qs.astype(BF16), kcat.astype(BF16)))
            v_all.append(vcat.astype(BF16))

    p_all = []
    for kh in range(n_kv):
        halves = []
        for hf in range(2):
            sink = jnp.full((RR, 1), sinks_ref[kh * group + hf], F32)
            for pr in range(1, npair):
                sink = jnp.where(pair_of_row == pr, sinks_ref[kh * group + 2 * pr + hf], sink)
            sm = jnp.where(mask, s_all[kh][:, hf * 2 * W:(hf + 1) * 2 * W], -jnp.inf)
            (p,) = _sink_softmax([sm], sink)
            halves.append(p.astype(BF16))
        p_all.append(jnp.concatenate(halves, axis=1))

    for kh in range(n_kv):
        o = _dot(p_all[kh], v_all[kh])
        for pr in range(npair):
            c0 = (kh * group + 2 * pr) * HD
            o_ref[0, :, c0:c0 + LANES] = o[pr * R:(pr + 1) * R]


def _attn_sample(qkv, ck, cv, layer, sinks, n_q, n_kv, n_new):
    B, R, NW = qkv.shape
    QW = n_q * HEAD_DIM
    KW = n_kv * HEAD_DIM
    return pl.pallas_call(
        functools.partial(_attn_sample_kernel, n_q=n_q, n_kv=n_kv, n_new=n_new),
        grid=(B,),
        in_specs=[
            pl.BlockSpec(memory_space=pltpu.SMEM),
            pl.BlockSpec((1, R, NW), lambda b: (b, 0, 0)),
            pl.BlockSpec((None, 1, WINDOW, KW), lambda b: (layer, b, 0, 0)),
            pl.BlockSpec((None, 1, WINDOW, KW), lambda b: (layer, b, 0, 0)),
        ],
        out_specs=pl.BlockSpec((1, R, QW), lambda b: (b, 0, 0)),
        out_shape=jax.ShapeDtypeStruct((B, R, QW), F32),
        compiler_params=pltpu.CompilerParams(dimension_semantics=("parallel",)),
        name="swa_sample",
    )(sinks, qkv, ck, cv)


def _rope_tables(pos):
    half = HEAD_DIM // 2
    inv = jnp.power(ROPE_THETA, -jnp.arange(half, dtype=F32) * (2.0 / HEAD_DIM))
    ang = pos.astype(F32)[:, None] * inv[None, :]
    cos = jnp.cos(ang)
    sin = jnp.sin(ang)
    reps = LANES // HEAD_DIM
    cos_t = jnp.tile(jnp.concatenate([cos, cos], axis=1), (1, reps))
    sin_t = jnp.tile(jnp.concatenate([-sin, sin], axis=1), (1, reps))
    return (jnp.stack([cos_t, jnp.ones_like(cos_t)]), jnp.stack([sin_t, jnp.zeros_like(sin_t)]))


def kernel(x_prompt, x_sample, state_hgrn, cache_k_win, cache_v_win, norm_gains, w_ffn_in, w_ffn_out,
           w_hgrn_in, hgrn_lb_logits, hgrn_norm_gain, w_hgrn_out, w_attn_in, attn_sinks, w_attn_out):
    B, L, D = x_prompt.shape
    SB, SL, _ = x_sample.shape
    depth = norm_gains.shape[0]
    TP, TS = B * L, SB * SL
    H = hgrn_norm_gain.shape[1] // HG_DV
    n_kv = cache_k_win.shape[3]
    n_q = w_attn_out.shape[1] // HEAD_DIM
    QW, KW = n_q * HEAD_DIM, n_kv * HEAD_DIM
    buf = cache_k_win.shape[2]

    x = jnp.concatenate([x_prompt.reshape(TP, D), x_sample.reshape(TS, D)], axis=0)

    lb_all = jnp.cumsum(jax.nn.softmax(hgrn_lb_logits.astype(F32), axis=0), axis=0)
    lb_all = lb_all - lb_all[0:1]

    pos = jnp.concatenate([jnp.tile(jnp.arange(L, dtype=jnp.int32), B),
                           jnp.tile(PAST_LEN + jnp.arange(SL, dtype=jnp.int32), SB)])
    rope = _rope_tables(pos)

    hg_p, hg_s, kp_l, vp_l, ks_l, vs_l = [], [], [], [], [], []
    for layer in range(depth):
        g = norm_gains[layer]
        x = _ffn_half(x, g[0:1], g[1:2], w_ffn_in, w_ffn_out, layer, 0)
        if layer % N_MIXERS == 0:
            a = layer // N_MIXERS
            p = _proj(x, g[2:3], w_hgrn_in, a, HGRN_PROJ_TILES)
            lb = lb_all[a][None, :]
            lbp = (jnp.log(lb), jnp.log1p(-lb), 1.0 - lb)
            ng = hgrn_norm_gain[a][None, :].astype(F32)
            o_p, s_p = _hgrn_scan(p, B, L, HG_CHUNK, HGRN_HB, HG_CHUNK, lbp, ng, None, BF16)
            ps = jnp.pad(p[TP:].reshape(SB, SL, -1), ((0, 0), (0, SAMPLE_PAD - SL), (0, 0)))
            o_s, s_s = _hgrn_scan(ps.reshape(SB * SAMPLE_PAD, -1), SB, SAMPLE_PAD, SAMPLE_PAD, H, SL,
                                  lbp, ng, (state_hgrn, a), F32)
            o_s = o_s.reshape(SB, SAMPLE_PAD, -1)[:, :SL].reshape(TS, -1).astype(BF16)
            x = _outproj(o_p, o_s, w_hgrn_out, x, g[3:4], a)
            hg_p.append(s_p.astype(state_hgrn.dtype))
            hg_s.append(s_s.astype(state_hgrn.dtype))
        else:
            bb = layer // N_MIXERS
            p = _proj(x, g[2:3], w_attn_in, bb, ATTN_PROJ_TILES, rope=rope, n_rope_cols=QW + KW)
            sinks = attn_sinks[bb].astype(F32)
            o_p = _attn_prompt(p, sinks, B, L, n_q, n_kv)
            pn = p[TP:].reshape(SB, SL, -1)
            ck = cache_k_win.reshape(-1, SB, buf, KW)
            cv = cache_v_win.reshape(-1, SB, buf, KW)
            o_s = _attn_sample(jnp.pad(pn, ((0, 0), (0, 8 - SL), (0, 0))), ck, cv, bb, sinks, n_q, n_kv, SL)
            x = _outproj(o_p, o_s[:, :SL].reshape(TS, QW).astype(BF16), w_attn_out, x, g[3:4], bb)
            tails = [p[(s + 1) * L - buf:(s + 1) * L, QW:] for s in range(B)]
            kp_l.append(jnp.stack([t[:, :KW] for t in tails]).reshape(B, buf, n_kv, HEAD_DIM))
            vp_l.append(jnp.stack([t[:, KW:] for t in tails]).reshape(B, buf, n_kv, HEAD_DIM))
            k_new = pn[:, :, QW:QW + KW].reshape(SB, SL, n_kv, HEAD_DIM)
            v_new = pn[:, :, QW + KW:].reshape(SB, SL, n_kv, HEAD_DIM)
            ks_l.append(jnp.concatenate([cache_k_win[bb], k_new], axis=1)[:, -buf:])
            vs_l.append(jnp.concatenate([cache_v_win[bb], v_new], axis=1)[:, -buf:])
        x = _ffn_half(x, g[4:5], g[5:6], w_ffn_in, w_ffn_out, layer, 1)

    yp = x[:TP].reshape(B, L, D)
    ys = x[TP:].reshape(SB, SL, D)
    return (yp, ys, jnp.stack(hg_p), jnp.stack(hg_s), jnp.stack(kp_l), jnp.stack(vp_l),
            jnp.stack(ks_l), jnp.stack(vs_l))
```

```python
import functools
import math

import numpy as np
import jax
import jax.numpy as jnp
from jax import lax
from jax.experimental import pallas as pl
from jax.experimental.pallas import tpu as pltpu

F32 = jnp.float32
BF16 = jnp.bfloat16

NORM_EPS = 1e-6
HEAD_DIM = 64
HG_DK = 128
HG_DV = 128
HG_CHUNK = 64
WINDOW = 128
PAST_LEN = 16384
ROPE_THETA = 10000.0
N_MIXERS = 2

LANES = 128
SUBLANES = 8
VMEM_BYTES_V7X = 64 * 1024 * 1024
VMEM_HEADROOM = 6 * 1024 * 1024

FFN_TM, FFN_TF = 1040, 256
HGRN_PROJ_TILES = (1040, 1024)
ATTN_PROJ_TILES = (1040, 512)
ROPE_ROW_CHUNK = 208
OUT_TM = 512
SAMPLE_PAD = 16
HGRN_HB = 4


def _vmem_limit(nbytes):
    return int(min(VMEM_BYTES_V7X - VMEM_HEADROOM, max(nbytes * 5 // 4, 16 * 1024 * 1024)))


def _rms(x, g):
    return x * lax.rsqrt(jnp.mean(x * x, axis=-1, keepdims=True) + NORM_EPS) * g


def _dot(a, b):
    return jnp.dot(a, b, preferred_element_type=F32)


def _dot_nt(a, b):
    return lax.dot_general(a, b, (((1,), (1,)), ((), ())), preferred_element_type=F32)


def _dot_tn(a, b):
    return lax.dot_general(a, b, (((0,), (0,)), ((), ())), preferred_element_type=F32)


def _ffn_kernel(x_ref, gpre_ref, gpost_ref, wg_ref, wu_ref, wo_ref, o_ref, xn_ref, *, nj):
    j = pl.program_id(1)

    @pl.when(j == 0)
    def _():
        xn_ref[...] = _rms(x_ref[...], gpre_ref[...]).astype(BF16)
        o_ref[...] = jnp.zeros_like(o_ref)

    xn = xn_ref[...]
    hg = _dot(xn, wg_ref[...].astype(BF16))
    hu = _dot(xn, wu_ref[...].astype(BF16))
    act = (hg * jax.nn.sigmoid(hg) * hu).astype(BF16)
    o_ref[...] += _dot(act, wo_ref[...].astype(BF16))

    @pl.when(j == nj - 1)
    def _():
        o_ref[...] = x_ref[...] + 0.5 * _rms(o_ref[...], gpost_ref[...])


def _ffn_half(x, g_pre, g_post, w_in, w_out, layer, half):
    T, D = x.shape
    F = w_out.shape[2]
    TM, TF = FFN_TM, FFN_TF
    nj = F // TF
    est = (3 * TM * D * 4 + TM * D * 2 + 2 * 3 * D * TF * 4
           + 3 * D * TF * 2 + 6 * TM * TF * 4)
    return pl.pallas_call(
        functools.partial(_ffn_kernel, nj=nj),
        grid=(T // TM, nj),
        in_specs=[
            pl.BlockSpec((TM, D), lambda i, j: (i, 0), pipeline_mode=pl.Buffered(1)),
            pl.BlockSpec((1, D), lambda i, j: (0, 0)),
            pl.BlockSpec((1, D), lambda i, j: (0, 0)),
            pl.BlockSpec((None, None, D, TF), lambda i, j: (layer, half, 0, j)),
            pl.BlockSpec((None, None, D, TF), lambda i, j: (layer, half, 0, nj + j)),
            pl.BlockSpec((None, None, TF, D), lambda i, j: (layer, half, j, 0)),
        ],
        out_specs=pl.BlockSpec((TM, D), lambda i, j: (i, 0)),
        out_shape=jax.ShapeDtypeStruct((T, D), F32),
        scratch_shapes=[pltpu.VMEM((TM, D), BF16)],
        compiler_params=pltpu.CompilerParams(
            dimension_semantics=("parallel", "arbitrary"),
            vmem_limit_bytes=_vmem_limit(est)),
        name="ffn_half",
    )(x, g_pre, g_post, w_in, w_in, w_out)


def _rope_tile(y, cos, sin_signed):
    lane = lax.broadcasted_iota(jnp.int32, y.shape, 1)
    first_half = (lane & (HEAD_DIM - 1)) < (HEAD_DIM // 2)
    rot = jnp.where(first_half,
                    pltpu.roll(y, LANES - HEAD_DIM // 2, 1),
                    pltpu.roll(y, HEAD_DIM // 2, 1))
    return y * cos + rot * sin_signed


def _proj_kernel(x_ref, g_ref, w_ref, o_ref, xn_ref):
    @pl.when(pl.program_id(1) == 0)
    def _():
        xn_ref[...] = _rms(x_ref[...], g_ref[...]).astype(BF16)

    o_ref[...] = _dot(xn_ref[...], w_ref[...].astype(BF16))


def _proj_rope_kernel(x_ref, g_ref, w_ref, cos_ref, sin_ref, o_ref, xn_ref, *, row_chunk):
    @pl.when(pl.program_id(1) == 0)
    def _():
        xn_ref[...] = _rms(x_ref[...], g_ref[...]).astype(BF16)

    wb = w_ref[...].astype(BF16)
    for r0 in range(0, xn_ref.shape[0], row_chunk):
        rows = slice(r0, r0 + row_chunk)
        y = _dot(xn_ref[rows], wb)
        cos = cos_ref[rows]
        sin = sin_ref[rows]
        for c in range(y.shape[1] // LANES):
            cols = slice(c * LANES, (c + 1) * LANES)
            o_ref[rows, cols] = _rope_tile(y[:, cols], cos, sin)


def _proj(x, g, w, layer, tiles, rope=None, n_rope_cols=0):
    T, D = x.shape
    N = w.shape[2]
    TM, TN = tiles
    assert T % TM == 0 and N % TN == 0 and n_rope_cols % TN == 0
    est = 2 * TM * D * 4 + TM * D * 2 + 2 * D * TN * 4 + D * TN * 2 + 4 * TM * TN * 4
    in_specs = [
        pl.BlockSpec((TM, D), lambda i, j: (i, 0)),
        pl.BlockSpec((1, D), lambda i, j: (0, 0)),
        pl.BlockSpec((None, D, TN), lambda i, j: (layer, 0, j)),
    ]
    args = [x, g, w]
    if rope is None:
        body = _proj_kernel
    else:
        assert TM % ROPE_ROW_CHUNK == 0
        body = functools.partial(_proj_rope_kernel, row_chunk=ROPE_ROW_CHUNK)
        n_rope_blocks = n_rope_cols // TN
        table = pl.BlockSpec((None, TM, LANES), lambda i, j: (jnp.where(j < n_rope_blocks, 0, 1), i, 0))
        in_specs += [table, table]
        args += list(rope)
    return pl.pallas_call(
        body,
        grid=(T // TM, N // TN),
        in_specs=in_specs,
        out_specs=pl.BlockSpec((TM, TN), lambda i, j: (i, j)),
        out_shape=jax.ShapeDtypeStruct((T, N), F32),
        scratch_shapes=[pltpu.VMEM((TM, D), BF16)],
        compiler_params=pltpu.CompilerParams(
            dimension_semantics=("parallel", "arbitrary"),
            vmem_limit_bytes=_vmem_limit(est)),
        name="mixer_in_proj",
    )(*args)


def _outproj_kernel(op_ref, os_ref, w_ref, x_ref, g_ref, y_ref, wbf_ref, *, n_prompt_tiles, ts):
    i = pl.program_id(0)

    @pl.when(i == 0)
    def _():
        wbf_ref[...] = w_ref[...].astype(BF16)

    @pl.when(i < n_prompt_tiles)
    def _():
        y_ref[...] = x_ref[...] + _rms(_dot(op_ref[...], wbf_ref[...]), g_ref[...])

    @pl.when(i == n_prompt_tiles)
    def _():
        y_ref[0:ts] = x_ref[0:ts] + _rms(_dot(os_ref[...], wbf_ref[...]), g_ref[...])


def _outproj(o_prompt, o_sample, w, x, g, layer):
    T, D = x.shape
    TP, K = o_prompt.shape
    TS = o_sample.shape[0]
    TM = OUT_TM
    assert TP % TM == 0 and TS <= TM and TS % SUBLANES == 0 and TP + TS == T
    npt = TP // TM
    est = K * D * (4 + 2) + 2 * TM * K * 2 + 2 * TS * K * 2 + 5 * TM * D * 4
    return pl.pallas_call(
        functools.partial(_outproj_kernel, n_prompt_tiles=npt, ts=TS),
        grid=(npt + 1,),
        in_specs=[
            pl.BlockSpec((TM, K), lambda i: (jnp.minimum(i, npt - 1), 0)),
            pl.BlockSpec((TS, K), lambda i: (0, 0)),
            pl.BlockSpec((None, K, D), lambda i: (layer, 0, 0), pipeline_mode=pl.Buffered(1)),
            pl.BlockSpec((TM, D), lambda i: (i, 0)),
            pl.BlockSpec((1, D), lambda i: (0, 0)),
        ],
        out_specs=pl.BlockSpec((TM, D), lambda i: (i, 0)),
        out_shape=jax.ShapeDtypeStruct((T, D), F32),
        scratch_shapes=[pltpu.VMEM((K, D), BF16)],
        compiler_params=pltpu.CompilerParams(
            dimension_semantics=("arbitrary",),
            vmem_limit_bytes=_vmem_limit(est)),
        name="mixer_out_proj",
    )(o_prompt, o_sample, w, x, g)


def _tiles(x):
    return [x[SUBLANES * j:SUBLANES * (j + 1)] for j in range(x.shape[0] // SUBLANES)]


def _bcast_row(t, i):
    return jnp.broadcast_to(t[i:i + 1], t.shape)


def _tile_prefix(t, sub):
    p = t + jnp.where(sub >= 1, pltpu.roll(t, 1, 0), 0.0)
    p = p + jnp.where(sub >= 2, pltpu.roll(p, 2, 0), 0.0)
    return p + jnp.where(sub >= 4, pltpu.roll(p, 4, 0), 0.0)


def _hgrn_gates(fz, loglb, log1m, onem, valid):
    e = jnp.exp(-jnp.abs(fz))
    lsig = jnp.minimum(fz, 0.0) - jnp.log(1.0 + e)
    ct = log1m + lsig
    g = jnp.maximum(loglb, ct) + jnp.log(1.0 + jnp.exp(-jnp.abs(loglb - ct)))
    r = 1.0 / (1.0 + e)
    k = onem * jnp.where(fz > 0.0, e * r, r)
    if valid is not None:
        g = jnp.where(valid, g, 0.0)
        k = jnp.where(valid, k, 0.0)
    return g, k


def _hgrn_level_operands(m, qt, kt, gt, bt, sub):
    nv = len(qt)
    zero = jnp.zeros_like(qt[0])
    qm, km = [], []
    if m >= SUBLANES:
        mv = m // SUBLANES
        for j in range(nv):
            jr = (j // (2 * mv)) * (2 * mv) + mv - 1
            ref = _bcast_row(bt[jr], SUBLANES - 1)
            if j % (2 * mv) >= mv:
                qm.append(qt[j] * jnp.exp(bt[j] - ref))
                km.append(zero)
            else:
                qm.append(zero)
                km.append(kt[j] * jnp.exp(ref - bt[j]))
        return qm, km
    upper = (sub & m) != 0
    for j in range(nv):
        if m == 1:
            e = jnp.exp(gt[j])
            qm.append(jnp.where(upper, qt[j] * e, 0.0))
            km.append(jnp.where(upper, 0.0, kt[j]))
            continue
        if m == 4:
            ref = _bcast_row(bt[j], 3)
        else:
            ref = jnp.where(sub < 4, _bcast_row(bt[j], 1), _bcast_row(bt[j], 5))
        e = jnp.exp(-jnp.abs(bt[j] - ref))
        qm.append(jnp.where(upper, qt[j] * e, 0.0))
        km.append(jnp.where(upper, 0.0, kt[j] * e))
    return qm, km


def _hgrn_chunk_group(loads, params, sts, *, C, valid_len):
    nh = len(loads)
    sub = lax.broadcasted_iota(jnp.int32, (SUBLANES, LANES), 0)
    valid = None
    if valid_len < C:
        valid = lax.broadcasted_iota(jnp.int32, (C, LANES), 0) < valid_len
    ti = lax.broadcasted_iota(jnp.int32, (C, C), 0)
    si = lax.broadcasted_iota(jnp.int32, (C, C), 1)

    gk = [_hgrn_gates(loads[h][1], params[h][0], params[h][1], params[h][2], valid) for h in range(nh)]

    gts, bts = [], []
    for h in range(nh):
        gt = _tiles(gk[h][0])
        bt = []
        for j, t in enumerate(gt):
            p = _tile_prefix(t, sub)
            bt.append(p if j == 0 else p + _bcast_row(bt[j - 1], SUBLANES - 1))
        gts.append(gt)
        bts.append(bt)

    scores = []
    for h in range(nh):
        q, k = loads[h][0], gk[h][1]
        qt, kt = _tiles(q), _tiles(k)
        a = jnp.where(ti == si, _dot_nt(q.astype(BF16), k.astype(BF16)), 0.0)
        m = C // 2
        while m >= 1:
            qm, km = _hgrn_level_operands(m, qt, kt, gts[h], bts[h], sub)
            am = _dot_nt(jnp.concatenate(qm, axis=0).astype(BF16), jnp.concatenate(km, axis=0).astype(BF16))
            if 2 * m < C:
                am = jnp.where((ti & (-2 * m)) == (si & (-2 * m)), am, 0.0)
            a = a + am
            m //= 2
        scores.append(a)

    outs = []
    for h in range(nh):
        q, _, v, gz = loads[h]
        k = gk[h][1]
        ng = params[h][3]
        st = sts[h]
        b = jnp.concatenate(bts[h], axis=0)
        o = _dot(scores[h].astype(BF16), v.astype(BF16)) + _dot_nt((q * jnp.exp(b)).astype(BF16), st.astype(BF16))
        bl = b[C - 1:C]
        kb = (k * jnp.exp(bl - b)).astype(BF16)
        st_new = st * jnp.exp(bl) + _dot_tn(v.astype(BF16), kb)
        on = o * lax.rsqrt(jnp.mean(o * o, axis=-1, keepdims=True) + NORM_EPS) * ng
        outs.append((on * (gz * jax.nn.sigmoid(gz)), st_new))
    return outs


def _hgrn_kernel(*refs, C, NC, HB, valid_len, has_s0):
    if has_s0:
        (q_ref, f_ref, i_ref, g_ref, loglb_ref, log1m_ref, onem_ref, ng_ref, s0_ref,
         o_ref, s_ref, st_ref) = refs
    else:
        (q_ref, f_ref, i_ref, g_ref, loglb_ref, log1m_ref, onem_ref, ng_ref,
         o_ref, s_ref, st_ref) = refs
        s0_ref = None

    for h in range(HB):
        if has_s0:
            st_ref[h] = s0_ref[0, h].T
        else:
            st_ref[h] = jnp.zeros((HG_DV, HG_DK), F32)

    def body(c, carry):
        rows = pl.ds(pl.multiple_of(c * C, C), C)
        cols = [slice(h * LANES, (h + 1) * LANES) for h in range(HB)]
        loads = [(q_ref[rows, cs], f_ref[rows, cs], i_ref[rows, cs], g_ref[rows, cs]) for cs in cols]
        params = [(loglb_ref[:, cs], log1m_ref[:, cs], onem_ref[:, cs], ng_ref[:, cs]) for cs in cols]
        outs = _hgrn_chunk_group(loads, params, [st_ref[h] for h in range(HB)], C=C, valid_len=valid_len)
        for h in range(HB):
            st_ref[h] = outs[h][1]
            o_ref[rows, cols[h]] = outs[h][0].astype(o_ref.dtype)
        return carry

    if NC == 1:
        body(0, 0)
    else:
        lax.fori_loop(0, NC, body, 0)

    for h in range(HB):
        s_ref[0, h] = st_ref[h].T


def _hgrn_scan(p, nseq, L, C, HB, valid_len, lbp, ng, s0, out_dtype):
    H = ng.shape[1] // LANES
    nhb = H // HB
    W = HB * LANES
    NC = L // C
    loglb, log1m, onem = lbp
    has_s0 = s0 is not None

    def sect(s):
        return pl.BlockSpec((L, W), lambda b, hb, s=s: (b, s * nhb + hb))

    vec = pl.BlockSpec((1, W), lambda b, hb: (0, hb))
    in_specs = [sect(0), sect(1), sect(2), sect(3), vec, vec, vec, vec]
    args = [p, p, p, p, loglb, log1m, onem, ng]
    if has_s0:
        s0_all, s0_layer = s0
        in_specs.append(pl.BlockSpec((None, 1, HB, HG_DK, HG_DV), lambda b, hb: (s0_layer, b, hb, 0, 0)))
        args.append(s0_all)
    est = 2 * 4 * L * W * 4 + 2 * L * W * 4 + 6 * HB * HG_DK * HG_DV * 4 + (8 << 20)
    return pl.pallas_call(
        functools.partial(_hgrn_kernel, C=C, NC=NC, HB=HB, valid_len=valid_len, has_s0=has_s0),
        grid=(nseq, nhb),
        in_specs=in_specs,
        out_specs=[pl.BlockSpec((L, W), lambda b, hb: (b, hb)),
                   pl.BlockSpec((1, HB, HG_DK, HG_DV), lambda b, hb: (b, hb, 0, 0))],
        out_shape=[jax.ShapeDtypeStruct((nseq * L, H * LANES), out_dtype),
                   jax.ShapeDtypeStruct((nseq, H, HG_DK, HG_DV), F32)],
        scratch_shapes=[pltpu.VMEM((HB, HG_DV, HG_DK), F32)],
        compiler_params=pltpu.CompilerParams(
            dimension_semantics=("parallel", "parallel"),
            vmem_limit_bytes=_vmem_limit(est)),
        name="hgrn_scan",
    )(*args)


def _sink_softmax(scores, sink):
    m = sink
    for s in scores:
        m = jnp.maximum(m, jnp.max(s, axis=-1, keepdims=True))
    ps = [jnp.exp(s - m) for s in scores]
    denom = jnp.exp(sink - m)
    for p in ps:
        denom = denom + jnp.sum(p, axis=-1, keepdims=True)
    return [p / denom for p in ps]


def _attn_prompt_kernel(sinks_ref, q_ref, kc_ref, kp_ref, vc_ref, vp_ref, o_ref, *, n_kv, group):
    i = pl.program_id(1)
    W = WINDOW
    HD = HEAD_DIM
    j = lax.broadcasted_iota(jnp.int32, (2 * W, W), 0)
    r = lax.broadcasted_iota(jnp.int32, (2 * W, W), 1)
    mask = (j >= jnp.maximum(r, jnp.where(i > 0, 0, W))) & (j <= r + W)
    scale = HD ** -0.5
    zeros = jnp.zeros((HD, W), F32)

    s_all, vt_all = [], []
    for c in range(n_kv // 2):
        cs = slice(c * LANES, (c + 1) * LANES)
        ktile = jnp.concatenate([kp_ref[:, cs], kc_ref[:, cs]], axis=0).astype(BF16)
        vt_all.append(jnp.concatenate([vp_ref[:, cs], vc_ref[:, cs]], axis=0).T.astype(BF16))
        for par in range(2):
            kh = 2 * c + par
            rhs = []
            for pr in range(group // 2):
                c0 = (kh * group + 2 * pr) * HD
                pt = (q_ref[:, c0:c0 + LANES] * scale).T
                for half in range(2):
                    blk = pt[half * HD:(half + 1) * HD]
                    rhs.append(jnp.concatenate([blk, zeros] if par == 0 else [zeros, blk], axis=0))
            s_all.append(_dot(ktile, jnp.concatenate(rhs, axis=1).astype(BF16)))

    p_all, inv_all = [], []
    for kh in range(n_kv):
        pn, inv = [], []
        for hh in range(group):
            sink = sinks_ref[kh * group + hh]
            sm = jnp.where(mask, s_all[kh][:, hh * W:(hh + 1) * W], -jnp.inf)
            mx = jnp.maximum(jnp.max(sm, axis=0, keepdims=True), sink)
            p = jnp.exp(sm - mx)
            inv.append(1.0 / (jnp.sum(p, axis=0, keepdims=True) + jnp.exp(sink - mx)))
            pn.append(p.astype(BF16))
        p_all.append(jnp.concatenate(pn, axis=1))
        inv_all.append(jnp.concatenate(inv, axis=1))

    for kh in range(n_kv):
        par = kh % 2
        o_t = _dot(vt_all[kh // 2], p_all[kh])[par * HD:(par + 1) * HD] * inv_all[kh]
        for pr in range(group // 2):
            pair_t = jnp.concatenate([o_t[:, (2 * pr) * W:(2 * pr + 1) * W],
                                      o_t[:, (2 * pr + 1) * W:(2 * pr + 2) * W]], axis=0)
            c0 = (kh * group + 2 * pr) * HD
            o_ref[:, c0:c0 + LANES] = pair_t.T.astype(o_ref.dtype)


def _attn_prompt(p, sinks, B, L, n_q, n_kv):
    W = WINDOW
    nb = L // W
    QW = n_q * HEAD_DIM
    KW = n_kv * HEAD_DIM
    kblk = QW // KW
    vblk = (QW + KW) // KW
    est = 2 * (W * QW * 4 + 4 * W * KW * 4 + W * QW * 2) + (16 << 20)
    return pl.pallas_call(
        functools.partial(_attn_prompt_kernel, n_kv=n_kv, group=n_q // n_kv),
        grid=(B, nb),
        in_specs=[
            pl.BlockSpec(memory_space=pltpu.SMEM),
            pl.BlockSpec((W, QW), lambda b, i: (b * nb + i, 0)),
            pl.BlockSpec((W, KW), lambda b, i: (b * nb + i, kblk)),
            pl.BlockSpec((W, KW), lambda b, i: (b * nb + jnp.maximum(i - 1, 0), kblk)),
            pl.BlockSpec((W, KW), lambda b, i: (b * nb + i, vblk)),
            pl.BlockSpec((W, KW), lambda b, i: (b * nb + jnp.maximum(i - 1, 0), vblk)),
        ],
        out_specs=pl.BlockSpec((W, QW), lambda b, i: (b * nb + i, 0)),
        out_shape=jax.ShapeDtypeStruct((B * L, QW), BF16),
        compiler_params=pltpu.CompilerParams(
            dimension_semantics=("parallel", "parallel"),
            vmem_limit_bytes=_vmem_limit(est)),
        name="swa_prompt",
    )(sinks, p, p, p, p, p)


def _attn_sample_kernel(sinks_ref, qkv_ref, ck_ref, cv_ref, o_ref, *, n_q, n_kv, n_new):
    R = qkv_ref.shape[1]
    W = WINDOW
    HD = HEAD_DIM
    group = n_q // n_kv
    npair = group // 2
    QW = n_q * HD
    KW = n_kv * HD
    RR = npair * R
    scale = HD ** -0.5
    row = lax.broadcasted_iota(jnp.int32, (RR, 2 * W), 0)
    r = row & (R - 1)
    j = lax.broadcasted_iota(jnp.int32, (RR, 2 * W), 1)
    mask = (j >= r) & (j <= r + W) & (j < W + n_new)
    pair_of_row = lax.broadcasted_iota(jnp.int32, (RR, 1), 0) // R
    first = lax.broadcasted_iota(jnp.int32, (2 * W, LANES), 1) < HD
    zpad = jnp.zeros((W - R, LANES), F32)

    s_all, v_all = [], []
    for c in range(n_kv // 2):
        cs = slice(c * LANES, (c + 1) * LANES)
        kn = qkv_ref[0, :, QW + c * LANES:QW + (c + 1) * LANES]
        vn = qkv_ref[0, :, QW + KW + c * LANES:QW + KW + (c + 1) * LANES]
        ktile = jnp.concatenate([ck_ref[0, :, cs], kn, zpad], axis=0)
        vtile = jnp.concatenate([cv_ref[0, :, cs], vn, zpad], axis=0)
        kswap = pltpu.roll(ktile, HD, 1)
        vswap = pltpu.roll(vtile, HD, 1)
        for par in range(2):
            kh = 2 * c + par
            k_lo, k_hi = (ktile, kswap) if par == 0 else (kswap, ktile)
            v_lo, v_hi = (vtile, vswap) if par == 0 else (vswap, vtile)
            kcat = jnp.concatenate([jnp.where(first, k_lo, 0.0), jnp.where(first, 0.0, k_hi)], axis=0)
            vcat = jnp.concatenate([jnp.where(first, v_lo, 0.0), jnp.where(first, 0.0, v_hi)], axis=0)
            qs = jnp.concatenate(
                [qkv_ref[0, :, (kh * group + 2 * pr) * HD:(kh * group + 2 * pr) * HD + LANES]
                 for pr in range(npair)], axis=0) * scale
            s_all.append(_dot_nt(qs.astype(BF16), kcat.astype(BF16)))
            v_all.append(vcat.astype(BF16))

    p_all = []
    for kh in range(n_kv):
        halves = []
        for hf in range(2):
            sink = jnp.full((RR, 1), sinks_ref[kh * group + hf], F32)
            for pr in range(1, npair):
                sink = jnp.where(pair_of_row == pr, sinks_ref[kh * group + 2 * pr + hf], sink)
            sm = jnp.where(mask, s_all[kh][:, hf * 2 * W:(hf + 1) * 2 * W], -jnp.inf)
            (p,) = _sink_softmax([sm], sink)
            halves.append(p.astype(BF16))
        p_all.append(jnp.concatenate(halves, axis=1))

    for kh in range(n_kv):
        o = _dot(p_all[kh], v_all[kh])
        for pr in range(npair):
            c0 = (kh * group + 2 * pr) * HD
            o_ref[0, :, c0:c0 + LANES] = o[pr * R:(pr + 1) * R]


def _attn_sample(qkv, ck, cv, layer, sinks, n_q, n_kv, n_new):
    B, R, NW = qkv.shape
    QW = n_q * HEAD_DIM
    KW = n_kv * HEAD_DIM
    return pl.pallas_call(
        functools.partial(_attn_sample_kernel, n_q=n_q, n_kv=n_kv, n_new=n_new),
        grid=(B,),
        in_specs=[
            pl.BlockSpec(memory_space=pltpu.SMEM),
            pl.BlockSpec((1, R, NW), lambda b: (b, 0, 0)),
            pl.BlockSpec((None, 1, WINDOW, KW), lambda b: (layer, b, 0, 0)),
            pl.BlockSpec((None, 1, WINDOW, KW), lambda b: (layer, b, 0, 0)),
        ],
        out_specs=pl.BlockSpec((1, R, QW), lambda b: (b, 0, 0)),
        out_shape=jax.ShapeDtypeStruct((B, R, QW), F32),
        compiler_params=pltpu.CompilerParams(dimension_semantics=("parallel",)),
        name="swa_sample",
    )(sinks, qkv, ck, cv)


def _rope_tables(pos):
    half = HEAD_DIM // 2
    inv = jnp.power(ROPE_THETA, -jnp.arange(half, dtype=F32) * (2.0 / HEAD_DIM))
    ang = pos.astype(F32)[:, None] * inv[None, :]
    cos = jnp.cos(ang)
    sin = jnp.sin(ang)
    reps = LANES // HEAD_DIM
    cos_t = jnp.tile(jnp.concatenate([cos, cos], axis=1), (1, reps))
    sin_t = jnp.tile(jnp.concatenate([-sin, sin], axis=1), (1, reps))
    return (jnp.stack([cos_t, jnp.ones_like(cos_t)]), jnp.stack([sin_t, jnp.zeros_like(sin_t)]))


def kernel(x_prompt, x_sample, state_hgrn, cache_k_win, cache_v_win, norm_gains, w_ffn_in, w_ffn_out,
           w_hgrn_in, hgrn_lb_logits, hgrn_norm_gain, w_hgrn_out, w_attn_in, attn_sinks, w_attn_out):
    B, L, D = x_prompt.shape
    SB, SL, _ = x_sample.shape
    depth = norm_gains.shape[0]
    TP, TS = B * L, SB * SL
    H = hgrn_norm_gain.shape[1] // HG_DV
    n_kv = cache_k_win.shape[3]
    n_q = w_attn_out.shape[1] // HEAD_DIM
    QW, KW = n_q * HEAD_DIM, n_kv * HEAD_DIM
    buf = cache_k_win.shape[2]

    x = jnp.concatenate([x_prompt.reshape(TP, D), x_sample.reshape(TS, D)], axis=0)

    lb_all = jnp.cumsum(jax.nn.softmax(hgrn_lb_logits.astype(F32), axis=0), axis=0)
    lb_all = lb_all - lb_all[0:1]

    pos = jnp.concatenate([jnp.tile(jnp.arange(L, dtype=jnp.int32), B),
                           jnp.tile(PAST_LEN + jnp.arange(SL, dtype=jnp.int32), SB)])
    rope = _rope_tables(pos)

    hg_p, hg_s, kp_l, vp_l, ks_l, vs_l = [], [], [], [], [], []
    for layer in range(depth):
        g = norm_gains[layer]
        x = _ffn_half(x, g[0:1], g[1:2], w_ffn_in, w_ffn_out, layer, 0)
        if layer % N_MIXERS == 0:
            a = layer // N_MIXERS
            p = _proj(x, g[2:3], w_hgrn_in, a, HGRN_PROJ_TILES)
            lb = lb_all[a][None, :]
            lbp = (jnp.log(lb), jnp.log1p(-lb), 1.0 - lb)
            ng = hgrn_norm_gain[a][None, :].astype(F32)
            o_p, s_p = _hgrn_scan(p, B, L, HG_CHUNK, HGRN_HB, HG_CHUNK, lbp, ng, None, BF16)
            ps = jnp.pad(p[TP:].reshape(SB, SL, -1), ((0, 0), (0, SAMPLE_PAD - SL), (0, 0)))
            o_s, s_s = _hgrn_scan(ps.reshape(SB * SAMPLE_PAD, -1), SB, SAMPLE_PAD, SAMPLE_PAD, H, SL,
                                  lbp, ng, (state_hgrn, a), F32)
            o_s = o_s.reshape(SB, SAMPLE_PAD, -1)[:, :SL].reshape(TS, -1).astype(BF16)
            x = _outproj(o_p, o_s, w_hgrn_out, x, g[3:4], a)
            hg_p.append(s_p.astype(state_hgrn.dtype))
            hg_s.append(s_s.astype(state_hgrn.dtype))
        else:
            bb = layer // N_MIXERS
            p = _proj(x, g[2:3], w_attn_in, bb, ATTN_PROJ_TILES, rope=rope, n_rope_cols=QW + KW)
            sinks = attn_sinks[bb].astype(F32)
            o_p = _attn_prompt(p, sinks, B, L, n_q, n_kv)
            pn = p[TP:].reshape(SB, SL, -1)
            ck = cache_k_win.reshape(-1, SB, buf, KW)
            cv = cache_v_win.reshape(-1, SB, buf, KW)
            o_s = _attn_sample(jnp.pad(pn, ((0, 0), (0, 8 - SL), (0, 0))), ck, cv, bb, sinks, n_q, n_kv, SL)
            x = _outproj(o_p, o_s[:, :SL].reshape(TS, QW).astype(BF16), w_attn_out, x, g[3:4], bb)
            tails = [p[(s + 1) * L - buf:(s + 1) * L, QW:] for s in range(B)]
            kp_l.append(jnp.stack([t[:, :KW] for t in tails]).reshape(B, buf, n_kv, HEAD_DIM))
            vp_l.append(jnp.stack([t[:, KW:] for t in tails]).reshape(B, buf, n_kv, HEAD_DIM))
            k_new = pn[:, :, QW:QW + KW].reshape(SB, SL, n_kv, HEAD_DIM)
            v_new = pn[:, :, QW + KW:].reshape(SB, SL, n_kv, HEAD_DIM)
            ks_l.append(jnp.concatenate([cache_k_win[bb], k_new], axis=1)[:, -buf:])
            vs_l.append(jnp.concatenate([cache_v_win[bb], v_new], axis=1)[:, -buf:])
        x = _ffn_half(x, g[4:5], g[5:6], w_ffn_in, w_ffn_out, layer, 1)

    yp = x[:TP].reshape(B, L, D)
    ys = x[TP:].reshape(SB, SL, D)
    return (yp, ys, jnp.stack(hg_p), jnp.stack(hg_s), jnp.stack(kp_l), jnp.stack(vp_l),
            jnp.stack(ks_l), jnp.stack(vs_l))
```

```python
import functools
import math

import numpy as np
import jax
import jax.numpy as jnp
from jax import lax
from jax.experimental import pallas as pl
from jax.experimental.pallas import tpu as pltpu

F32 = jnp.float32
BF16 = jnp.bfloat16

NORM_EPS = 1e-6
HEAD_DIM = 64
HG_DK = 128
HG_DV = 128
HG_CHUNK = 64
WINDOW = 128
PAST_LEN = 16384
ROPE_THETA = 10000.0
N_MIXERS = 2

LANES = 128
SUBLANES = 8
VMEM_BYTES_V7X = 64 * 1024 * 1024
VMEM_HEADROOM = 6 * 1024 * 1024

FFN_TM, FFN_TF = 1040, 256
FFN_RC = 208
HGRN_PROJ_TILES = (1040, 1024)
ATTN_PROJ_TILES = (1040, 512)
ROPE_ROW_CHUNK = 208
OUT_TM = 512
SAMPLE_PAD = 16
HGRN_HB = 4


def _vmem_limit(nbytes):
    return int(min(VMEM_BYTES_V7X - VMEM_HEADROOM, max(nbytes * 5 // 4, 16 * 1024 * 1024)))


def _rms(x, g):
    return x * lax.rsqrt(jnp.mean(x * x, axis=-1, keepdims=True) + NORM_EPS) * g


def _dot(a, b):
    return jnp.dot(a, b, preferred_element_type=F32)


def _dot_nt(a, b):
    return lax.dot_general(a, b, (((1,), (1,)), ((), ())), preferred_element_type=F32)


def _dot_tn(a, b):
    return lax.dot_general(a, b, (((0,), (0,)), ((), ())), preferred_element_type=F32)


def _ffn_kernel(x_hbm, gpre_ref, gpost_ref, wg_ref, wu_ref, wo_ref, y_hbm,
                acc_ref, xn_ref, xc_ref, yc_ref, in_sem, out_sem, *, nt, nj, tm, rc):
    i = pl.program_id(0)
    j = pl.program_id(1)
    nch = tm // rc
    slot = i % 2
    pslot = 1 - slot
    k = j - 1
    gpre = gpre_ref[...]
    gpost = gpost_ref[...]

    def in_copy(row0, b):
        return pltpu.make_async_copy(x_hbm.at[pl.ds(row0, rc)], xc_ref.at[b], in_sem.at[b])

    def out_copy(row0, b):
        return pltpu.make_async_copy(yc_ref.at[b], y_hbm.at[pl.ds(row0, rc)], out_sem.at[b])

    def task_row(t):
        return jnp.where(t < nch, (i - 1) * tm + t * rc, (i + 1) * tm + (t - nch) * rc)

    def task_live(t):
        return (t >= 0) & (t < 2 * nch) & jnp.where(t < nch, i >= 1, i <= nt - 2)

    @pl.when((i == 0) & (j == 0))
    def _():
        for c in range(nch):
            cp = in_copy(c * rc, c % 2)
            cp.start()
            cp.wait()
            xn_ref[0, c * rc:(c + 1) * rc] = _rms(xc_ref[c % 2], gpre).astype(BF16)

    @pl.when(task_live(k))
    def _():
        in_copy(0, k & 1).wait()

    @pl.when(task_live(k) & (k >= 2) & (k < nch))
    def _():
        out_copy(0, k & 1).wait()

    @pl.when((j == nch + 1) & (i >= 1))
    def _():
        out_copy(0, 0).wait()
        out_copy(0, 1).wait()

    @pl.when(task_live(j))
    def _():
        in_copy(task_row(j), j & 1).start()

    def accumulate(first):
        xn = xn_ref[slot]
        hg = _dot(xn, wg_ref[...].astype(BF16))
        hu = _dot(xn, wu_ref[...].astype(BF16))
        act = (hg * jax.nn.sigmoid(hg) * hu).astype(BF16)
        contrib = _dot(act, wo_ref[...].astype(BF16))
        if first:
            acc_ref[slot] = contrib
        else:
            acc_ref[slot] += contrib

    is_epi = (k >= 0) & (k < nch) & (i >= 1)
    is_pro = (k >= nch) & (k < 2 * nch) & (i <= nt - 2)

    @pl.when(j == 0)
    def _():
        accumulate(True)

    @pl.when(is_epi)
    def _():
        accumulate(False)
        b = k & 1
        rows = pl.ds(pl.multiple_of(k * rc, rc), rc)
        yc_ref[b] = xc_ref[b] + 0.5 * _rms(acc_ref[pslot, rows], gpost)
        out_copy((i - 1) * tm + k * rc, b).start()

    @pl.when(is_pro)
    def _():
        accumulate(False)
        rows = pl.ds(pl.multiple_of((k - nch) * rc, rc), rc)
        xn_ref[pslot, rows] = _rms(xc_ref[k & 1], gpre).astype(BF16)

    @pl.when((j >= 1) & jnp.logical_not(is_epi) & jnp.logical_not(is_pro))
    def _():
        accumulate(False)

    @pl.when((i == nt - 1) & (j == nj - 1))
    def _():
        last = (nt - 1) % 2
        for c in range(nch):
            b = c % 2
            row0 = (nt - 1) * tm + c * rc
            cp = in_copy(row0, b)
            cp.start()
            cp.wait()
            yc_ref[b] = xc_ref[b] + 0.5 * _rms(acc_ref[last, c * rc:(c + 1) * rc], gpost)
            oc = out_copy(row0, b)
            oc.start()
            oc.wait()


def _ffn_half(x, g_pre, g_post, w_in, w_out, layer, half):
    T, D = x.shape
    F = w_out.shape[2]
    TM, TF, RC = FFN_TM, FFN_TF, FFN_RC
    nj = F // TF
    nt = T // TM
    assert T % TM == 0 and F % TF == 0 and TM % RC == 0 and TM // RC >= 2 and nj >= 2 * (TM // RC) + 2
    est = (2 * TM * D * (4 + 2) + 4 * RC * D * 4 + 2 * 3 * D * TF * 4
           + 3 * D * TF * 2 + 4 * TM * TF * 4 + TM * D * 4)
    return pl.pallas_call(
        functools.partial(_ffn_kernel, nt=nt, nj=nj, tm=TM, rc=RC),
        grid=(nt, nj),
        in_specs=[
            pl.BlockSpec(memory_space=pl.ANY),
            pl.BlockSpec((1, D), lambda i, j: (0, 0)),
            pl.BlockSpec((1, D), lambda i, j: (0, 0)),
            pl.BlockSpec((None, None, D, TF), lambda i, j: (layer, half, 0, j)),
            pl.BlockSpec((None, None, D, TF), lambda i, j: (layer, half, 0, nj + j)),
            pl.BlockSpec((None, None, TF, D), lambda i, j: (layer, half, j, 0)),
        ],
        out_specs=pl.BlockSpec(memory_space=pl.ANY),
        out_shape=jax.ShapeDtypeStruct((T, D), F32),
        scratch_shapes=[pltpu.VMEM((2, TM, D), F32), pltpu.VMEM((2, TM, D), BF16),
                        pltpu.VMEM((2, RC, D), F32), pltpu.VMEM((2, RC, D), F32),
                        pltpu.SemaphoreType.DMA((2,)), pltpu.SemaphoreType.DMA((2,))],
        compiler_params=pltpu.CompilerParams(
            dimension_semantics=("arbitrary", "arbitrary"),
            vmem_limit_bytes=_vmem_limit(est)),
        name="ffn_half",
    )(x, g_pre, g_post, w_in, w_in, w_out)


def _rope_tile(y, cos, sin_signed):
    lane = lax.broadcasted_iota(jnp.int32, y.shape, 1)
    first_half = (lane & (HEAD_DIM - 1)) < (HEAD_DIM // 2)
    rot = jnp.where(first_half,
                    pltpu.roll(y, LANES - HEAD_DIM // 2, 1),
                    pltpu.roll(y, HEAD_DIM // 2, 1))
    return y * cos + rot * sin_signed


def _proj_kernel(x_ref, g_ref, w_ref, o_ref, xn_ref):
    @pl.when(pl.program_id(1) == 0)
    def _():
        xn_ref[...] = _rms(x_ref[...], g_ref[...]).astype(BF16)

    o_ref[...] = _dot(xn_ref[...], w_ref[...].astype(BF16))


def _proj_rope_kernel(x_ref, g_ref, w_ref, cos_ref, sin_ref, o_ref, xn_ref, *, row_chunk):
    @pl.when(pl.program_id(1) == 0)
    def _():
        xn_ref[...] = _rms(x_ref[...], g_ref[...]).astype(BF16)

    wb = w_ref[...].astype(BF16)
    for r0 in range(0, xn_ref.shape[0], row_chunk):
        rows = slice(r0, r0 + row_chunk)
        y = _dot(xn_ref[rows], wb)
        cos = cos_ref[rows]
        sin = sin_ref[rows]
        for c in range(y.shape[1] // LANES):
            cols = slice(c * LANES, (c + 1) * LANES)
            o_ref[rows, cols] = _rope_tile(y[:, cols], cos, sin)


def _proj(x, g, w, layer, tiles, rope=None, n_rope_cols=0):
    T, D = x.shape
    N = w.shape[2]
    TM, TN = tiles
    assert T % TM == 0 and N % TN == 0 and n_rope_cols % TN == 0
    est = 2 * TM * D * 4 + TM * D * 2 + 2 * D * TN * 4 + D * TN * 2 + 4 * TM * TN * 4
    in_specs = [
        pl.BlockSpec((TM, D), lambda i, j: (i, 0)),
        pl.BlockSpec((1, D), lambda i, j: (0, 0)),
        pl.BlockSpec((None, D, TN), lambda i, j: (layer, 0, j)),
    ]
    args = [x, g, w]
    if rope is None:
        body = _proj_kernel
    else:
        assert TM % ROPE_ROW_CHUNK == 0
        body = functools.partial(_proj_rope_kernel, row_chunk=ROPE_ROW_CHUNK)
        n_rope_blocks = n_rope_cols // TN
        table = pl.BlockSpec((None, TM, LANES), lambda i, j: (jnp.where(j < n_rope_blocks, 0, 1), i, 0))
        in_specs += [table, table]
        args += list(rope)
    return pl.pallas_call(
        body,
        grid=(T // TM, N // TN),
        in_specs=in_specs,
        out_specs=pl.BlockSpec((TM, TN), lambda i, j: (i, j)),
        out_shape=jax.ShapeDtypeStruct((T, N), F32),
        scratch_shapes=[pltpu.VMEM((TM, D), BF16)],
        compiler_params=pltpu.CompilerParams(
            dimension_semantics=("parallel", "arbitrary"),
            vmem_limit_bytes=_vmem_limit(est)),
        name="mixer_in_proj",
    )(*args)


def _outproj_kernel(op_ref, os_ref, w_ref, x_ref, g_ref, y_ref, wbf_ref, *, n_prompt_tiles, ts):
    i = pl.program_id(0)

    @pl.when(i == 0)
    def _():
        wbf_ref[...] = w_ref[...].astype(BF16)

    @pl.when(i < n_prompt_tiles)
    def _():
        y_ref[...] = x_ref[...] + _rms(_dot(op_ref[...], wbf_ref[...]), g_ref[...])

    @pl.when(i == n_prompt_tiles)
    def _():
        y_ref[0:ts] = x_ref[0:ts] + _rms(_dot(os_ref[...], wbf_ref[...]), g_ref[...])


def _outproj(o_prompt, o_sample, w, x, g, layer):
    T, D = x.shape
    TP, K = o_prompt.shape
    TS = o_sample.shape[0]
    TM = OUT_TM
    assert TP % TM == 0 and TS <= TM and TS % SUBLANES == 0 and TP + TS == T
    npt = TP // TM
    est = K * D * (4 + 2) + 2 * TM * K * 2 + 2 * TS * K * 2 + 5 * TM * D * 4
    return pl.pallas_call(
        functools.partial(_outproj_kernel, n_prompt_tiles=npt, ts=TS),
        grid=(npt + 1,),
        in_specs=[
            pl.BlockSpec((TM, K), lambda i: (jnp.minimum(i, npt - 1), 0)),
            pl.BlockSpec((TS, K), lambda i: (0, 0)),
            pl.BlockSpec((None, K, D), lambda i: (layer, 0, 0), pipeline_mode=pl.Buffered(1)),
            pl.BlockSpec((TM, D), lambda i: (i, 0)),
            pl.BlockSpec((1, D), lambda i: (0, 0)),
        ],
        out_specs=pl.BlockSpec((TM, D), lambda i: (i, 0)),
        out_shape=jax.ShapeDtypeStruct((T, D), F32),
        scratch_shapes=[pltpu.VMEM((K, D), BF16)],
        compiler_params=pltpu.CompilerParams(
            dimension_semantics=("arbitrary",),
            vmem_limit_bytes=_vmem_limit(est)),
        name="mixer_out_proj",
    )(o_prompt, o_sample, w, x, g)


def _tiles(x):
    return [x[SUBLANES * j:SUBLANES * (j + 1)] for j in range(x.shape[0] // SUBLANES)]


def _bcast_row(t, i):
    return jnp.broadcast_to(t[i:i + 1], t.shape)


def _tile_prefix(t, sub):
    p = t + jnp.where(sub >= 1, pltpu.roll(t, 1, 0), 0.0)
    p = p + jnp.where(sub >= 2, pltpu.roll(p, 2, 0), 0.0)
    return p + jnp.where(sub >= 4, pltpu.roll(p, 4, 0), 0.0)


def _hgrn_gates(fz, loglb, log1m, onem, valid):
    e = jnp.exp(-jnp.abs(fz))
    lsig = jnp.minimum(fz, 0.0) - jnp.log(1.0 + e)
    ct = log1m + lsig
    g = jnp.maximum(loglb, ct) + jnp.log(1.0 + jnp.exp(-jnp.abs(loglb - ct)))
    r = 1.0 / (1.0 + e)
    k = onem * jnp.where(fz > 0.0, e * r, r)
    if valid is not None:
        g = jnp.where(valid, g, 0.0)
        k = jnp.where(valid, k, 0.0)
    return g, k


def _hgrn_level_operands(m, qt, kt, gt, bt, sub):
    nv = len(qt)
    zero = jnp.zeros_like(qt[0])
    qm, km = [], []
    if m >= SUBLANES:
        mv = m // SUBLANES
        for j in range(nv):
            jr = (j // (2 * mv)) * (2 * mv) + mv - 1
            ref = _bcast_row(bt[jr], SUBLANES - 1)
            if j % (2 * mv) >= mv:
                qm.append(qt[j] * jnp.exp(bt[j] - ref))
                km.append(zero)
            else:
                qm.append(zero)
                km.append(kt[j] * jnp.exp(ref - bt[j]))
        return qm, km
    upper = (sub & m) != 0
    for j in range(nv):
        if m == 1:
            e = jnp.exp(gt[j])
            qm.append(jnp.where(upper, qt[j] * e, 0.0))
            km.append(jnp.where(upper, 0.0, kt[j]))
            continue
        if m == 4:
            ref = _bcast_row(bt[j], 3)
        else:
            ref = jnp.where(sub < 4, _bcast_row(bt[j], 1), _bcast_row(bt[j], 5))
        e = jnp.exp(-jnp.abs(bt[j] - ref))
        qm.append(jnp.where(upper, qt[j] * e, 0.0))
        km.append(jnp.where(upper, 0.0, kt[j] * e))
    return qm, km


def _hgrn_chunk_group(loads, params, sts, *, C, valid_len):
    nh = len(loads)
    sub = lax.broadcasted_iota(jnp.int32, (SUBLANES, LANES), 0)
    valid = None
    if valid_len < C:
        valid = lax.broadcasted_iota(jnp.int32, (C, LANES), 0) < valid_len
    ti = lax.broadcasted_iota(jnp.int32, (C, C), 0)
    si = lax.broadcasted_iota(jnp.int32, (C, C), 1)

    gk = [_hgrn_gates(loads[h][1], params[h][0], params[h][1], params[h][2], valid) for h in range(nh)]

    gts, bts = [], []
    for h in range(nh):
        gt = _tiles(gk[h][0])
        bt = []
        for j, t in enumerate(gt):
            p = _tile_prefix(t, sub)
            bt.append(p if j == 0 else p + _bcast_row(bt[j - 1], SUBLANES - 1))
        gts.append(gt)
        bts.append(bt)

    scores = []
    for h in range(nh):
        q, k = loads[h][0], gk[h][1]
        qt, kt = _tiles(q), _tiles(k)
        a = jnp.where(ti == si, _dot_nt(q.astype(BF16), k.astype(BF16)), 0.0)
        m = C // 2
        while m >= 1:
            qm, km = _hgrn_level_operands(m, qt, kt, gts[h], bts[h], sub)
            am = _dot_nt(jnp.concatenate(qm, axis=0).astype(BF16), jnp.concatenate(km, axis=0).astype(BF16))
            if 2 * m < C:
                am = jnp.where((ti & (-2 * m)) == (si & (-2 * m)), am, 0.0)
            a = a + am
            m //= 2
        scores.append(a)

    outs = []
    for h in range(nh):
        q, _, v, gz = loads[h]
        k = gk[h][1]
        ng = params[h][3]
        st = sts[h]
        b = jnp.concatenate(bts[h], axis=0)
        o = _dot(scores[h].astype(BF16), v.astype(BF16)) + _dot_nt((q * jnp.exp(b)).astype(BF16), st.astype(BF16))
        bl = b[C - 1:C]
        kb = (k * jnp.exp(bl - b)).astype(BF16)
        st_new = st * jnp.exp(bl) + _dot_tn(v.astype(BF16), kb)
        on = o * lax.rsqrt(jnp.mean(o * o, axis=-1, keepdims=True) + NORM_EPS) * ng
        outs.append((on * (gz * jax.nn.sigmoid(gz)), st_new))
    return outs


def _hgrn_kernel(*refs, C, NC, HB, valid_len, has_s0):
    if has_s0:
        (q_ref, f_ref, i_ref, g_ref, loglb_ref, log1m_ref, onem_ref, ng_ref, s0_ref,
         o_ref, s_ref, st_ref) = refs
    else:
        (q_ref, f_ref, i_ref, g_ref, loglb_ref, log1m_ref, onem_ref, ng_ref,
         o_ref, s_ref, st_ref) = refs
        s0_ref = None

    for h in range(HB):
        if has_s0:
            st_ref[h] = s0_ref[0, h].T
        else:
            st_ref[h] = jnp.zeros((HG_DV, HG_DK), F32)

    def body(c, carry):
        rows = pl.ds(pl.multiple_of(c * C, C), C)
        cols = [slice(h * LANES, (h + 1) * LANES) for h in range(HB)]
        loads = [(q_ref[rows, cs], f_ref[rows, cs], i_ref[rows, cs], g_ref[rows, cs]) for cs in cols]
        params = [(loglb_ref[:, cs], log1m_ref[:, cs], onem_ref[:, cs], ng_ref[:, cs]) for cs in cols]
        outs = _hgrn_chunk_group(loads, params, [st_ref[h] for h in range(HB)], C=C, valid_len=valid_len)
        for h in range(HB):
            st_ref[h] = outs[h][1]
            o_ref[rows, cols[h]] = outs[h][0].astype(o_ref.dtype)
        return carry

    if NC == 1:
        body(0, 0)
    else:
        lax.fori_loop(0, NC, body, 0)

    for h in range(HB):
        s_ref[0, h] = st_ref[h].T


def _hgrn_scan(p, nseq, L, C, HB, valid_len, lbp, ng, s0, out_dtype):
    H = ng.shape[1] // LANES
    nhb = H // HB
    W = HB * LANES
    NC = L // C
    loglb, log1m, onem = lbp
    has_s0 = s0 is not None

    def sect(s):
        return pl.BlockSpec((L, W), lambda b, hb, s=s: (b, s * nhb + hb))

    vec = pl.BlockSpec((1, W), lambda b, hb: (0, hb))
    in_specs = [sect(0), sect(1), sect(2), sect(3), vec, vec, vec, vec]
    args = [p, p, p, p, loglb, log1m, onem, ng]
    if has_s0:
        s0_all, s0_layer = s0
        in_specs.append(pl.BlockSpec((None, 1, HB, HG_DK, HG_DV), lambda b, hb: (s0_layer, b, hb, 0, 0)))
        args.append(s0_all)
    est = 2 * 4 * L * W * 4 + 2 * L * W * 4 + 6 * HB * HG_DK * HG_DV * 4 + (8 << 20)
    return pl.pallas_call(
        functools.partial(_hgrn_kernel, C=C, NC=NC, HB=HB, valid_len=valid_len, has_s0=has_s0),
        grid=(nseq, nhb),
        in_specs=in_specs,
        out_specs=[pl.BlockSpec((L, W), lambda b, hb: (b, hb)),
                   pl.BlockSpec((1, HB, HG_DK, HG_DV), lambda b, hb: (b, hb, 0, 0))],
        out_shape=[jax.ShapeDtypeStruct((nseq * L, H * LANES), out_dtype),
                   jax.ShapeDtypeStruct((nseq, H, HG_DK, HG_DV), F32)],
        scratch_shapes=[pltpu.VMEM((HB, HG_DV, HG_DK), F32)],
        compiler_params=pltpu.CompilerParams(
            dimension_semantics=("parallel", "parallel"),
            vmem_limit_bytes=_vmem_limit(est)),
        name="hgrn_scan",
    )(*args)


def _sink_softmax(scores, sink):
    m = sink
    for s in scores:
        m = jnp.maximum(m, jnp.max(s, axis=-1, keepdims=True))
    ps = [jnp.exp(s - m) for s in scores]
    denom = jnp.exp(sink - m)
    for p in ps:
        denom = denom + jnp.sum(p, axis=-1, keepdims=True)
    return [p / denom for p in ps]


def _attn_prompt_kernel(sinks_ref, q_ref, kc_ref, kp_ref, vc_ref, vp_ref, o_ref, *, n_kv, group):
    i = pl.program_id(1)
    W = WINDOW
    HD = HEAD_DIM
    j = lax.broadcasted_iota(jnp.int32, (2 * W, W), 0)
    r = lax.broadcasted_iota(jnp.int32, (2 * W, W), 1)
    mask = (j >= jnp.maximum(r, jnp.where(i > 0, 0, W))) & (j <= r + W)
    scale = HD ** -0.5
    zeros = jnp.zeros((HD, W), F32)

    s_all, vt_all = [], []
    for c in range(n_kv // 2):
        cs = slice(c * LANES, (c + 1) * LANES)
        ktile = jnp.concatenate([kp_ref[:, cs], kc_ref[:, cs]], axis=0).astype(BF16)
        vt_all.append(jnp.concatenate([vp_ref[:, cs], vc_ref[:, cs]], axis=0).T.astype(BF16))
        for par in range(2):
            kh = 2 * c + par
            rhs = []
            for pr in range(group // 2):
                c0 = (kh * group + 2 * pr) * HD
                pt = (q_ref[:, c0:c0 + LANES] * scale).T
                for half in range(2):
                    blk = pt[half * HD:(half + 1) * HD]
                    rhs.append(jnp.concatenate([blk, zeros] if par == 0 else [zeros, blk], axis=0))
            s_all.append(_dot(ktile, jnp.concatenate(rhs, axis=1).astype(BF16)))

    p_all, inv_all = [], []
    for kh in range(n_kv):
        pn, inv = [], []
        for hh in range(group):
            sink = sinks_ref[kh * group + hh]
            sm = jnp.where(mask, s_all[kh][:, hh * W:(hh + 1) * W], -jnp.inf)
            mx = jnp.maximum(jnp.max(sm, axis=0, keepdims=True), sink)
            p = jnp.exp(sm - mx)
            inv.append(1.0 / (jnp.sum(p, axis=0, keepdims=True) + jnp.exp(sink - mx)))
            pn.append(p.astype(BF16))
        p_all.append(jnp.concatenate(pn, axis=1))
        inv_all.append(jnp.concatenate(inv, axis=1))

    for kh in range(n_kv):
        par = kh % 2
        o_t = _dot(vt_all[kh // 2], p_all[kh])[par * HD:(par + 1) * HD] * inv_all[kh]
        for pr in range(group // 2):
            pair_t = jnp.concatenate([o_t[:, (2 * pr) * W:(2 * pr + 1) * W],
                                      o_t[:, (2 * pr + 1) * W:(2 * pr + 2) * W]], axis=0)
            c0 = (kh * group + 2 * pr) * HD
            o_ref[:, c0:c0 + LANES] = pair_t.T.astype(o_ref.dtype)


def _attn_prompt(p, sinks, B, L, n_q, n_kv):
    W = WINDOW
    nb = L // W
    QW = n_q * HEAD_DIM
    KW = n_kv * HEAD_DIM
    kblk = QW // KW
    vblk = (QW + KW) // KW
    est = 2 * (W * QW * 4 + 4 * W * KW * 4 + W * QW * 2) + (16 << 20)
    return pl.pallas_call(
        functools.partial(_attn_prompt_kernel, n_kv=n_kv, group=n_q // n_kv),
        grid=(B, nb),
        in_specs=[
            pl.BlockSpec(memory_space=pltpu.SMEM),
            pl.BlockSpec((W, QW), lambda b, i: (b * nb + i, 0)),
            pl.BlockSpec((W, KW), lambda b, i: (b * nb + i, kblk)),
            pl.BlockSpec((W, KW), lambda b, i: (b * nb + jnp.maximum(i - 1, 0), kblk)),
            pl.BlockSpec((W, KW), lambda b, i: (b * nb + i, vblk)),
            pl.BlockSpec((W, KW), lambda b, i: (b * nb + jnp.maximum(i - 1, 0), vblk)),
        ],
        out_specs=pl.BlockSpec((W, QW), lambda b, i: (b * nb + i, 0)),
        out_shape=jax.ShapeDtypeStruct((B * L, QW), BF16),
        compiler_params=pltpu.CompilerParams(
            dimension_semantics=("parallel", "parallel"),
            vmem_limit_bytes=_vmem_limit(est)),
        name="swa_prompt",
    )(sinks, p, p, p, p, p)


def _attn_sample_kernel(sinks_ref, qkv_ref, ck_ref, cv_ref, o_ref, *, n_q, n_kv, n_new):
    R = qkv_ref.shape[1]
    W = WINDOW
    HD = HEAD_DIM
    group = n_q // n_kv
    npair = group // 2
    QW = n_q * HD
    KW = n_kv * HD
    RR = npair * R
    scale = HD ** -0.5
    row = lax.broadcasted_iota(jnp.int32, (RR, 2 * W), 0)
    r = row & (R - 1)
    j = lax.broadcasted_iota(jnp.int32, (RR, 2 * W), 1)
    mask = (j >= r) & (j <= r + W) & (j < W + n_new)
    pair_of_row = lax.broadcasted_iota(jnp.int32, (RR, 1), 0) // R
    first = lax.broadcasted_iota(jnp.int32, (2 * W, LANES), 1) < HD
    zpad = jnp.zeros((W - R, LANES), F32)

    s_all, v_all = [], []
    for c in range(n_kv // 2):
        cs = slice(c * LANES, (c + 1) * LANES)
        kn = qkv_ref[0, :, QW + c * LANES:QW + (c + 1) * LANES]
        vn = qkv_ref[0, :, QW + KW + c * LANES:QW + KW + (c + 1) * LANES]
        ktile = jnp.concatenate([ck_ref[0, :, cs], kn, zpad], axis=0)
        vtile = jnp.concatenate([cv_ref[0, :, cs], vn, zpad], axis=0)
        kswap = pltpu.roll(ktile, HD, 1)
        vswap = pltpu.roll(vtile, HD, 1)
        for par in range(2):
            kh = 2 * c + par
            k_lo, k_hi = (ktile, kswap) if par == 0 else (kswap, ktile)
            v_lo, v_hi = (vtile, vswap) if par == 0 else (vswap, vtile)
            kcat = jnp.concatenate([jnp.where(first, k_lo, 0.0), jnp.where(first, 0.0, k_hi)], axis=0)
            vcat = jnp.concatenate([jnp.where(first, v_lo, 0.0), jnp.where(first, 0.0, v_hi)], axis=0)
            qs = jnp.concatenate(
                [qkv_ref[0, :, (kh * group + 2 * pr) * HD:(kh * group + 2 * pr) * HD + LANES]
                 for pr in range(npair)], axis=0) * scale
            s_all.append(_dot_nt(qs.astype(BF16), kcat.astype(BF16)))
            v_all.append(vcat.astype(BF16))

    p_all = []
    for kh in range(n_kv):
        halves = []
        for hf in range(2):
            sink = jnp.full((RR, 1), sinks_ref[kh * group + hf], F32)
            for pr in range(1, npair):
                sink = jnp.where(pair_of_row == pr, sinks_ref[kh * group + 2 * pr + hf], sink)
            sm = jnp.where(mask, s_all[kh][:, hf * 2 * W:(hf + 1) * 2 * W], -jnp.inf)
            (p,) = _sink_softmax([sm], sink)
            halves.append(p.astype(BF16))
        p_all.append(jnp.concatenate(halves, axis=1))

    for kh in range(n_kv):
        o = _dot(p_all[kh], v_all[kh])
        for pr in range(npair):
            c0 = (kh * group + 2 * pr) * HD
            o_ref[0, :, c0:c0 + LANES] = o[pr * R:(pr + 1) * R]


def _attn_sample(qkv, ck, cv, layer, sinks, n_q, n_kv, n_new):
    B, R, NW = qkv.shape
    QW = n_q * HEAD_DIM
    KW = n_kv * HEAD_DIM
    return pl.pallas_call(
        functools.partial(_attn_sample_kernel, n_q=n_q, n_kv=n_kv, n_new=n_new),
        grid=(B,),
        in_specs=[
            pl.BlockSpec(memory_space=pltpu.SMEM),
            pl.BlockSpec((1, R, NW), lambda b: (b, 0, 0)),
            pl.BlockSpec((None, 1, WINDOW, KW), lambda b: (layer, b, 0, 0)),
            pl.BlockSpec((None, 1, WINDOW, KW), lambda b: (layer, b, 0, 0)),
        ],
        out_specs=pl.BlockSpec((1, R, QW), lambda b: (b, 0, 0)),
        out_shape=jax.ShapeDtypeStruct((B, R, QW), F32),
        compiler_params=pltpu.CompilerParams(dimension_semantics=("parallel",)),
        name="swa_sample",
    )(sinks, qkv, ck, cv)


def _rope_tables(pos):
    half = HEAD_DIM // 2
    inv = jnp.power(ROPE_THETA, -jnp.arange(half, dtype=F32) * (2.0 / HEAD_DIM))
    ang = pos.astype(F32)[:, None] * inv[None, :]
    cos = jnp.cos(ang)
    sin = jnp.sin(ang)
    reps = LANES // HEAD_DIM
    cos_t = jnp.tile(jnp.concatenate([cos, cos], axis=1), (1, reps))
    sin_t = jnp.tile(jnp.concatenate([-sin, sin], axis=1), (1, reps))
    return (jnp.stack([cos_t, jnp.ones_like(cos_t)]), jnp.stack([sin_t, jnp.zeros_like(sin_t)]))


def kernel(x_prompt, x_sample, state_hgrn, cache_k_win, cache_v_win, norm_gains, w_ffn_in, w_ffn_out,
           w_hgrn_in, hgrn_lb_logits, hgrn_norm_gain, w_hgrn_out, w_attn_in, attn_sinks, w_attn_out):
    B, L, D = x_prompt.shape
    SB, SL, _ = x_sample.shape
    depth = norm_gains.shape[0]
    TP, TS = B * L, SB * SL
    H = hgrn_norm_gain.shape[1] // HG_DV
    n_kv = cache_k_win.shape[3]
    n_q = w_attn_out.shape[1] // HEAD_DIM
    QW, KW = n_q * HEAD_DIM, n_kv * HEAD_DIM
    buf = cache_k_win.shape[2]

    x = jnp.concatenate([x_prompt.reshape(TP, D), x_sample.reshape(TS, D)], axis=0)

    lb_all = jnp.cumsum(jax.nn.softmax(hgrn_lb_logits.astype(F32), axis=0), axis=0)
    lb_all = lb_all - lb_all[0:1]

    pos = jnp.concatenate([jnp.tile(jnp.arange(L, dtype=jnp.int32), B),
                           jnp.tile(PAST_LEN + jnp.arange(SL, dtype=jnp.int32), SB)])
    rope = _rope_tables(pos)

    hg_p, hg_s, kp_l, vp_l, ks_l, vs_l = [], [], [], [], [], []
    for layer in range(depth):
        g = norm_gains[layer]
        x = _ffn_half(x, g[0:1], g[1:2], w_ffn_in, w_ffn_out, layer, 0)
        if layer % N_MIXERS == 0:
            a = layer // N_MIXERS
            p = _proj(x, g[2:3], w_hgrn_in, a, HGRN_PROJ_TILES)
            lb = lb_all[a][None, :]
            lbp = (jnp.log(lb), jnp.log1p(-lb), 1.0 - lb)
            ng = hgrn_norm_gain[a][None, :].astype(F32)
            o_p, s_p = _hgrn_scan(p, B, L, HG_CHUNK, HGRN_HB, HG_CHUNK, lbp, ng, None, BF16)
            ps = jnp.pad(p[TP:].reshape(SB, SL, -1), ((0, 0), (0, SAMPLE_PAD - SL), (0, 0)))
            o_s, s_s = _hgrn_scan(ps.reshape(SB * SAMPLE_PAD, -1), SB, SAMPLE_PAD, SAMPLE_PAD, H, SL,
                                  lbp, ng, (state_hgrn, a), F32)
            o_s = o_s.reshape(SB, SAMPLE_PAD, -1)[:, :SL].reshape(TS, -1).astype(BF16)
            x = _outproj(o_p, o_s, w_hgrn_out, x, g[3:4], a)
            hg_p.append(s_p.astype(state_hgrn.dtype))
            hg_s.append(s_s.astype(state_hgrn.dtype))
        else:
            bb = layer // N_MIXERS
            p = _proj(x, g[2:3], w_attn_in, bb, ATTN_PROJ_TILES, rope=rope, n_rope_cols=QW + KW)
            sinks = attn_sinks[bb].astype(F32)
            o_p = _attn_prompt(p, sinks, B, L, n_q, n_kv)
            pn = p[TP:].reshape(SB, SL, -1)
            ck = cache_k_win.reshape(-1, SB, buf, KW)
            cv = cache_v_win.reshape(-1, SB, buf, KW)
            o_s = _attn_sample(jnp.pad(pn, ((0, 0), (0, 8 - SL), (0, 0))), ck, cv, bb, sinks, n_q, n_kv, SL)
            x = _outproj(o_p, o_s[:, :SL].reshape(TS, QW).astype(BF16), w_attn_out, x, g[3:4], bb)
            tails = [p[(s + 1) * L - buf:(s + 1) * L, QW:] for s in range(B)]
            kp_l.append(jnp.stack([t[:, :KW] for t in tails]).reshape(B, buf, n_kv, HEAD_DIM))
            vp_l.append(jnp.stack([t[:, KW:] for t in tails]).reshape(B, buf, n_kv, HEAD_DIM))
            k_new = pn[:, :, QW:QW + KW].reshape(SB, SL, n_kv, HEAD_DIM)
            v_new = pn[:, :, QW + KW:].reshape(SB, SL, n_kv, HEAD_DIM)
            ks_l.append(jnp.concatenate([cache_k_win[bb], k_new], axis=1)[:, -buf:])
            vs_l.append(jnp.concatenate([cache_v_win[bb], v_new], axis=1)[:, -buf:])
        x = _ffn_half(x, g[4:5], g[5:6], w_ffn_in, w_ffn_out, layer, 1)

    yp = x[:TP].reshape(B, L, D)
    ys = x[TP:].reshape(SB, SL, D)
    return (yp, ys, jnp.stack(hg_p), jnp.stack(hg_s), jnp.stack(kp_l), jnp.stack(vp_l),
            jnp.stack(ks_l), jnp.stack(vs_l))
```

```python
import functools
import math

import numpy as np
import jax
import jax.numpy as jnp
from jax import lax
from jax.experimental import pallas as pl
from jax.experimental.pallas import tpu as pltpu

F32 = jnp.float32
BF16 = jnp.bfloat16

NORM_EPS = 1e-6
HEAD_DIM = 64
HG_DK = 128
HG_DV = 128
HG_CHUNK = 64
WINDOW = 128
PAST_LEN = 16384
ROPE_THETA = 10000.0
N_MIXERS = 2

LANES = 128
SUBLANES = 8
VMEM_BYTES_V7X = 64 * 1024 * 1024
VMEM_HEADROOM = 6 * 1024 * 1024

FFN_TM, FFN_TF = 1040, 256
FFN_RC = 208
HGRN_SAMPLE_PROJ_TN = 1024
FUSED_HB, FUSED_NSEQ, FUSED_RB = 2, 2, 256
ATTN_PROJ_TILES = (1040, 512)
ROPE_ROW_CHUNK = 208
OUT_TM = 512
SAMPLE_PAD = 16


def _vmem_limit(nbytes):
    return int(min(VMEM_BYTES_V7X - VMEM_HEADROOM, max(nbytes * 5 // 4, 16 * 1024 * 1024)))


def _rms(x, g):
    return x * lax.rsqrt(jnp.mean(x * x, axis=-1, keepdims=True) + NORM_EPS) * g


def _dot(a, b):
    return jnp.dot(a, b, preferred_element_type=F32)


def _dot_nt(a, b):
    return lax.dot_general(a, b, (((1,), (1,)), ((), ())), preferred_element_type=F32)


def _dot_tn(a, b):
    return lax.dot_general(a, b, (((0,), (0,)), ((), ())), preferred_element_type=F32)


def _ffn_kernel(x_hbm, gpre_ref, gpost_ref, wg_ref, wu_ref, wo_ref, y_hbm,
                acc_ref, xn_ref, xc_ref, yc_ref, in_sem, out_sem, *, nt, nj, tm, rc):
    i = pl.program_id(0)
    j = pl.program_id(1)
    nch = tm // rc
    slot = i % 2
    pslot = 1 - slot
    k = j - 1
    gpre = gpre_ref[...]
    gpost = gpost_ref[...]

    def in_copy(row0, b):
        return pltpu.make_async_copy(x_hbm.at[pl.ds(row0, rc)], xc_ref.at[b], in_sem.at[b])

    def out_copy(row0, b):
        return pltpu.make_async_copy(yc_ref.at[b], y_hbm.at[pl.ds(row0, rc)], out_sem.at[b])

    def task_row(t):
        return jnp.where(t < nch, (i - 1) * tm + t * rc, (i + 1) * tm + (t - nch) * rc)

    def task_live(t):
        return (t >= 0) & (t < 2 * nch) & jnp.where(t < nch, i >= 1, i <= nt - 2)

    @pl.when((i == 0) & (j == 0))
    def _():
        for c in range(nch):
            cp = in_copy(c * rc, c % 2)
            cp.start()
            cp.wait()
            xn_ref[0, c * rc:(c + 1) * rc] = _rms(xc_ref[c % 2], gpre).astype(BF16)

    @pl.when(task_live(k))
    def _():
        in_copy(0, k & 1).wait()

    @pl.when(task_live(k) & (k >= 2) & (k < nch))
    def _():
        out_copy(0, k & 1).wait()

    @pl.when((j == nch + 1) & (i >= 1))
    def _():
        out_copy(0, 0).wait()
        out_copy(0, 1).wait()

    @pl.when(task_live(j))
    def _():
        in_copy(task_row(j), j & 1).start()

    def accumulate(first):
        xn = xn_ref[slot]
        hg = _dot(xn, wg_ref[...].astype(BF16))
        hu = _dot(xn, wu_ref[...].astype(BF16))
        act = (hg * jax.nn.sigmoid(hg) * hu).astype(BF16)
        contrib = _dot(act, wo_ref[...].astype(BF16))
        if first:
            acc_ref[slot] = contrib
        else:
            acc_ref[slot] += contrib

    is_epi = (k >= 0) & (k < nch) & (i >= 1)
    is_pro = (k >= nch) & (k < 2 * nch) & (i <= nt - 2)

    @pl.when(j == 0)
    def _():
        accumulate(True)

    @pl.when(is_epi)
    def _():
        accumulate(False)
        b = k & 1
        rows = pl.ds(pl.multiple_of(k * rc, rc), rc)
        yc_ref[b] = xc_ref[b] + 0.5 * _rms(acc_ref[pslot, rows], gpost)
        out_copy((i - 1) * tm + k * rc, b).start()

    @pl.when(is_pro)
    def _():
        accumulate(False)
        rows = pl.ds(pl.multiple_of((k - nch) * rc, rc), rc)
        xn_ref[pslot, rows] = _rms(xc_ref[k & 1], gpre).astype(BF16)

    @pl.when((j >= 1) & jnp.logical_not(is_epi) & jnp.logical_not(is_pro))
    def _():
        accumulate(False)

    @pl.when((i == nt - 1) & (j == nj - 1))
    def _():
        last = (nt - 1) % 2
        for c in range(nch):
            b = c % 2
            row0 = (nt - 1) * tm + c * rc
            cp = in_copy(row0, b)
            cp.start()
            cp.wait()
            yc_ref[b] = xc_ref[b] + 0.5 * _rms(acc_ref[last, c * rc:(c + 1) * rc], gpost)
            oc = out_copy(row0, b)
            oc.start()
            oc.wait()


def _ffn_half(x, g_pre, g_post, w_in, w_out, layer, half):
    T, D = x.shape
    F = w_out.shape[2]
    TM, TF, RC = FFN_TM, FFN_TF, FFN_RC
    nj = F // TF
    nt = T // TM
    assert T % TM == 0 and F % TF == 0 and TM % RC == 0 and TM // RC >= 2 and nj >= 2 * (TM // RC) + 2
    est = (2 * TM * D * (4 + 2) + 4 * RC * D * 4 + 2 * 3 * D * TF * 4
           + 3 * D * TF * 2 + 4 * TM * TF * 4 + TM * D * 4)
    return pl.pallas_call(
        functools.partial(_ffn_kernel, nt=nt, nj=nj, tm=TM, rc=RC),
        grid=(nt, nj),
        in_specs=[
            pl.BlockSpec(memory_space=pl.ANY),
            pl.BlockSpec((1, D), lambda i, j: (0, 0)),
            pl.BlockSpec((1, D), lambda i, j: (0, 0)),
            pl.BlockSpec((None, None, D, TF), lambda i, j: (layer, half, 0, j)),
            pl.BlockSpec((None, None, D, TF), lambda i, j: (layer, half, 0, nj + j)),
            pl.BlockSpec((None, None, TF, D), lambda i, j: (layer, half, j, 0)),
        ],
        out_specs=pl.BlockSpec(memory_space=pl.ANY),
        out_shape=jax.ShapeDtypeStruct((T, D), F32),
        scratch_shapes=[pltpu.VMEM((2, TM, D), F32), pltpu.VMEM((2, TM, D), BF16),
                        pltpu.VMEM((2, RC, D), F32), pltpu.VMEM((2, RC, D), F32),
                        pltpu.SemaphoreType.DMA((2,)), pltpu.SemaphoreType.DMA((2,))],
        compiler_params=pltpu.CompilerParams(
            dimension_semantics=("arbitrary", "arbitrary"),
            vmem_limit_bytes=_vmem_limit(est)),
        name="ffn_half",
    )(x, g_pre, g_post, w_in, w_in, w_out)


def _rope_tile(y, cos, sin_signed):
    lane = lax.broadcasted_iota(jnp.int32, y.shape, 1)
    first_half = (lane & (HEAD_DIM - 1)) < (HEAD_DIM // 2)
    rot = jnp.where(first_half,
                    pltpu.roll(y, LANES - HEAD_DIM // 2, 1),
                    pltpu.roll(y, HEAD_DIM // 2, 1))
    return y * cos + rot * sin_signed


def _proj_kernel(x_ref, g_ref, w_ref, o_ref, xn_ref):
    @pl.when(pl.program_id(1) == 0)
    def _():
        xn_ref[...] = _rms(x_ref[...], g_ref[...]).astype(BF16)

    o_ref[...] = _dot(xn_ref[...], w_ref[...].astype(BF16))


def _proj_rope_kernel(x_ref, g_ref, w_ref, cos_ref, sin_ref, o_ref, xn_ref, *, row_chunk):
    @pl.when(pl.program_id(1) == 0)
    def _():
        xn_ref[...] = _rms(x_ref[...], g_ref[...]).astype(BF16)

    wb = w_ref[...].astype(BF16)
    for r0 in range(0, xn_ref.shape[0], row_chunk):
        rows = slice(r0, r0 + row_chunk)
        y = _dot(xn_ref[rows], wb)
        cos = cos_ref[rows]
        sin = sin_ref[rows]
        for c in range(y.shape[1] // LANES):
            cols = slice(c * LANES, (c + 1) * LANES)
            o_ref[rows, cols] = _rope_tile(y[:, cols], cos, sin)


def _proj(x, g, w, layer, tiles, rope=None, n_rope_cols=0):
    T, D = x.shape
    N = w.shape[2]
    TM, TN = tiles
    assert T % TM == 0 and N % TN == 0 and n_rope_cols % TN == 0
    est = 2 * TM * D * 4 + TM * D * 2 + 2 * D * TN * 4 + D * TN * 2 + 4 * TM * TN * 4
    in_specs = [
        pl.BlockSpec((TM, D), lambda i, j: (i, 0)),
        pl.BlockSpec((1, D), lambda i, j: (0, 0)),
        pl.BlockSpec((None, D, TN), lambda i, j: (layer, 0, j)),
    ]
    args = [x, g, w]
    if rope is None:
        body = _proj_kernel
    else:
        assert TM % ROPE_ROW_CHUNK == 0
        body = functools.partial(_proj_rope_kernel, row_chunk=ROPE_ROW_CHUNK)
        n_rope_blocks = n_rope_cols // TN
        table = pl.BlockSpec((None, TM, LANES), lambda i, j: (jnp.where(j < n_rope_blocks, 0, 1), i, 0))
        in_specs += [table, table]
        args += list(rope)
    return pl.pallas_call(
        body,
        grid=(T // TM, N // TN),
        in_specs=in_specs,
        out_specs=pl.BlockSpec((TM, TN), lambda i, j: (i, j)),
        out_shape=jax.ShapeDtypeStruct((T, N), F32),
        scratch_shapes=[pltpu.VMEM((TM, D), BF16)],
        compiler_params=pltpu.CompilerParams(
            dimension_semantics=("parallel", "arbitrary"),
            vmem_limit_bytes=_vmem_limit(est)),
        name="mixer_in_proj",
    )(*args)


def _outproj_kernel(op_ref, os_ref, w_ref, x_ref, g_ref, y_ref, wbf_ref, *, n_prompt_tiles, ts):
    i = pl.program_id(0)

    @pl.when(i == 0)
    def _():
        wbf_ref[...] = w_ref[...].astype(BF16)

    @pl.when(i < n_prompt_tiles)
    def _():
        y_ref[...] = x_ref[...] + _rms(_dot(op_ref[...], wbf_ref[...]), g_ref[...])

    @pl.when(i == n_prompt_tiles)
    def _():
        y_ref[0:ts] = x_ref[0:ts] + _rms(_dot(os_ref[...], wbf_ref[...]), g_ref[...])


def _outproj(o_prompt, o_sample, w, x, g, layer):
    T, D = x.shape
    TP, K = o_prompt.shape
    TS = o_sample.shape[0]
    TM = OUT_TM
    assert TP % TM == 0 and TS <= TM and TS % SUBLANES == 0 and TP + TS == T
    npt = TP // TM
    est = K * D * (4 + 2) + 2 * TM * K * 2 + 2 * TS * K * 2 + 5 * TM * D * 4
    return pl.pallas_call(
        functools.partial(_outproj_kernel, n_prompt_tiles=npt, ts=TS),
        grid=(npt + 1,),
        in_specs=[
            pl.BlockSpec((TM, K), lambda i: (jnp.minimum(i, npt - 1), 0)),
            pl.BlockSpec((TS, K), lambda i: (0, 0)),
            pl.BlockSpec((None, K, D), lambda i: (layer, 0, 0), pipeline_mode=pl.Buffered(1)),
            pl.BlockSpec((TM, D), lambda i: (i, 0)),
            pl.BlockSpec((1, D), lambda i: (0, 0)),
        ],
        out_specs=pl.BlockSpec((TM, D), lambda i: (i, 0)),
        out_shape=jax.ShapeDtypeStruct((T, D), F32),
        scratch_shapes=[pltpu.VMEM((K, D), BF16)],
        compiler_params=pltpu.CompilerParams(
            dimension_semantics=("arbitrary",),
            vmem_limit_bytes=_vmem_limit(est)),
        name="mixer_out_proj",
    )(o_prompt, o_sample, w, x, g)


def _tiles(x):
    return [x[SUBLANES * j:SUBLANES * (j + 1)] for j in range(x.shape[0] // SUBLANES)]


def _bcast_row(t, i):
    return jnp.broadcast_to(t[i:i + 1], t.shape)


def _tile_prefix(t, sub):
    p = t + jnp.where(sub >= 1, pltpu.roll(t, 1, 0), 0.0)
    p = p + jnp.where(sub >= 2, pltpu.roll(p, 2, 0), 0.0)
    return p + jnp.where(sub >= 4, pltpu.roll(p, 4, 0), 0.0)


def _hgrn_gates(fz, loglb, log1m, onem, valid):
    e = jnp.exp(-jnp.abs(fz))
    lsig = jnp.minimum(fz, 0.0) - jnp.log(1.0 + e)
    ct = log1m + lsig
    g = jnp.maximum(loglb, ct) + jnp.log(1.0 + jnp.exp(-jnp.abs(loglb - ct)))
    r = 1.0 / (1.0 + e)
    k = onem * jnp.where(fz > 0.0, e * r, r)
    if valid is not None:
        g = jnp.where(valid, g, 0.0)
        k = jnp.where(valid, k, 0.0)
    return g, k


def _hgrn_level_operands(m, qt, kt, gt, bt, sub):
    nv = len(qt)
    zero = jnp.zeros_like(qt[0])
    qm, km = [], []
    if m >= SUBLANES:
        mv = m // SUBLANES
        for j in range(nv):
            jr = (j // (2 * mv)) * (2 * mv) + mv - 1
            ref = _bcast_row(bt[jr], SUBLANES - 1)
            if j % (2 * mv) >= mv:
                qm.append(qt[j] * jnp.exp(bt[j] - ref))
                km.append(zero)
            else:
                qm.append(zero)
                km.append(kt[j] * jnp.exp(ref - bt[j]))
        return qm, km
    upper = (sub & m) != 0
    for j in range(nv):
        if m == 1:
            e = jnp.exp(gt[j])
            qm.append(jnp.where(upper, qt[j] * e, 0.0))
            km.append(jnp.where(upper, 0.0, kt[j]))
            continue
        if m == 4:
            ref = _bcast_row(bt[j], 3)
        else:
            ref = jnp.where(sub < 4, _bcast_row(bt[j], 1), _bcast_row(bt[j], 5))
        e = jnp.exp(-jnp.abs(bt[j] - ref))
        qm.append(jnp.where(upper, qt[j] * e, 0.0))
        km.append(jnp.where(upper, 0.0, kt[j] * e))
    return qm, km


def _hgrn_chunk_group(loads, params, sts, *, C, valid_len):
    nh = len(loads)
    sub = lax.broadcasted_iota(jnp.int32, (SUBLANES, LANES), 0)
    valid = None
    if valid_len < C:
        valid = lax.broadcasted_iota(jnp.int32, (C, LANES), 0) < valid_len
    ti = lax.broadcasted_iota(jnp.int32, (C, C), 0)
    si = lax.broadcasted_iota(jnp.int32, (C, C), 1)

    gk = [_hgrn_gates(loads[h][1], params[h][0], params[h][1], params[h][2], valid) for h in range(nh)]

    gts, bts = [], []
    for h in range(nh):
        gt = _tiles(gk[h][0])
        bt = []
        for j, t in enumerate(gt):
            p = _tile_prefix(t, sub)
            bt.append(p if j == 0 else p + _bcast_row(bt[j - 1], SUBLANES - 1))
        gts.append(gt)
        bts.append(bt)

    scores = []
    for h in range(nh):
        q, k = loads[h][0], gk[h][1]
        qt, kt = _tiles(q), _tiles(k)
        a = jnp.where(ti == si, _dot_nt(q.astype(BF16), k.astype(BF16)), 0.0)
        m = C // 2
        while m >= 1:
            qm, km = _hgrn_level_operands(m, qt, kt, gts[h], bts[h], sub)
            am = _dot_nt(jnp.concatenate(qm, axis=0).astype(BF16), jnp.concatenate(km, axis=0).astype(BF16))
            if 2 * m < C:
                am = jnp.where((ti & (-2 * m)) == (si & (-2 * m)), am, 0.0)
            a = a + am
            m //= 2
        scores.append(a)

    outs = []
    for h in range(nh):
        q, _, v, gz = loads[h]
        k = gk[h][1]
        ng = params[h][3]
        st = sts[h]
        b = jnp.concatenate(bts[h], axis=0)
        o = _dot(scores[h].astype(BF16), v.astype(BF16)) + _dot_nt((q * jnp.exp(b)).astype(BF16), st.astype(BF16))
        bl = b[C - 1:C]
        kb = (k * jnp.exp(bl - b)).astype(BF16)
        st_new = st * jnp.exp(bl) + _dot_tn(v.astype(BF16), kb)
        on = o * lax.rsqrt(jnp.mean(o * o, axis=-1, keepdims=True) + NORM_EPS) * ng
        outs.append((on * (gz * jax.nn.sigmoid(gz)), st_new))
    return outs


def _hgrn_kernel(*refs, C, NC, HB, valid_len, has_s0):
    if has_s0:
        (q_ref, f_ref, i_ref, g_ref, loglb_ref, log1m_ref, onem_ref, ng_ref, s0_ref,
         o_ref, s_ref, st_ref) = refs
    else:
        (q_ref, f_ref, i_ref, g_ref, loglb_ref, log1m_ref, onem_ref, ng_ref,
         o_ref, s_ref, st_ref) = refs
        s0_ref = None

    for h in range(HB):
        if has_s0:
            st_ref[h] = s0_ref[0, h].T
        else:
            st_ref[h] = jnp.zeros((HG_DV, HG_DK), F32)

    def body(c, carry):
        rows = pl.ds(pl.multiple_of(c * C, C), C)
        cols = [slice(h * LANES, (h + 1) * LANES) for h in range(HB)]
        loads = [(q_ref[rows, cs], f_ref[rows, cs], i_ref[rows, cs], g_ref[rows, cs]) for cs in cols]
        params = [(loglb_ref[:, cs], log1m_ref[:, cs], onem_ref[:, cs], ng_ref[:, cs]) for cs in cols]
        outs = _hgrn_chunk_group(loads, params, [st_ref[h] for h in range(HB)], C=C, valid_len=valid_len)
        for h in range(HB):
            st_ref[h] = outs[h][1]
            o_ref[rows, cols[h]] = outs[h][0].astype(o_ref.dtype)
        return carry

    if NC == 1:
        body(0, 0)
    else:
        lax.fori_loop(0, NC, body, 0)

    for h in range(HB):
        s_ref[0, h] = st_ref[h].T


def _hgrn_scan(p, nseq, L, C, HB, valid_len, lbp, ng, s0, out_dtype):
    H = ng.shape[1] // LANES
    nhb = H // HB
    W = HB * LANES
    NC = L // C
    loglb, log1m, onem = lbp
    has_s0 = s0 is not None

    def sect(s):
        return pl.BlockSpec((L, W), lambda b, hb, s=s: (b, s * nhb + hb))

    vec = pl.BlockSpec((1, W), lambda b, hb: (0, hb))
    in_specs = [sect(0), sect(1), sect(2), sect(3), vec, vec, vec, vec]
    args = [p, p, p, p, loglb, log1m, onem, ng]
    if has_s0:
        s0_all, s0_layer = s0
        in_specs.append(pl.BlockSpec((None, 1, HB, HG_DK, HG_DV), lambda b, hb: (s0_layer, b, hb, 0, 0)))
        args.append(s0_all)
    est = 2 * 4 * L * W * 4 + 2 * L * W * 4 + 6 * HB * HG_DK * HG_DV * 4 + (8 << 20)
    return pl.pallas_call(
        functools.partial(_hgrn_kernel, C=C, NC=NC, HB=HB, valid_len=valid_len, has_s0=has_s0),
        grid=(nseq, nhb),
        in_specs=in_specs,
        out_specs=[pl.BlockSpec((L, W), lambda b, hb: (b, hb)),
                   pl.BlockSpec((1, HB, HG_DK, HG_DV), lambda b, hb: (b, hb, 0, 0))],
        out_shape=[jax.ShapeDtypeStruct((nseq * L, H * LANES), out_dtype),
                   jax.ShapeDtypeStruct((nseq, H, HG_DK, HG_DV), F32)],
        scratch_shapes=[pltpu.VMEM((HB, HG_DV, HG_DK), F32)],
        compiler_params=pltpu.CompilerParams(
            dimension_semantics=("parallel", "parallel"),
            vmem_limit_bytes=_vmem_limit(est)),
        name="hgrn_scan",
    )(*args)


def _hgrn_fused_kernel(x_hbm, g_ref, wq_ref, wf_ref, wi_ref, wg_ref, loglb_ref, log1m_ref, onem_ref, ng_ref,
                       o_ref, s_ref, xn_ref, xc_ref, proj_ref, st_ref, in_sem, *, L, C, RB, NSEQ, HB):
    sg = pl.program_id(0)
    hb = pl.program_id(1)
    D = xn_ref.shape[1]
    WC = HB * LANES
    NB = L // RB
    CPB = RB // C
    w_refs = (wq_ref, wf_ref, wi_ref, wg_ref)
    assert CPB == len(w_refs)

    @pl.when(hb == 0)
    def _():
        g = g_ref[...]
        nchunk = NSEQ * NB

        def copy(c):
            return pltpu.make_async_copy(x_hbm.at[pl.ds(sg * (NSEQ * L) + c * RB, RB)],
                                         xc_ref.at[c % 2], in_sem.at[c % 2])
        copy(0).start()
        for c in range(nchunk):
            copy(c).wait()
            if c + 1 < nchunk:
                copy(c + 1).start()
            xn_ref[c * RB:(c + 1) * RB] = _rms(xc_ref[c % 2], g).astype(BF16)

    def project(rb, section, slot):
        rows = [xn_ref[pl.ds(pl.multiple_of(s * L + rb * RB, RB), RB)] for s in range(NSEQ)]
        y = _dot(jnp.concatenate(rows, axis=0), w_refs[section][...].astype(BF16))
        proj_ref[slot, :, section * WC:(section + 1) * WC] = y

    for ch in range(NSEQ * HB):
        st_ref[ch] = jnp.zeros((HG_DV, HG_DK), F32)
    for section in range(CPB):
        project(0, section, 0)

    params = [(loglb_ref[:, h * LANES:(h + 1) * LANES], log1m_ref[:, h * LANES:(h + 1) * LANES],
               onem_ref[:, h * LANES:(h + 1) * LANES], ng_ref[:, h * LANES:(h + 1) * LANES])
              for _ in range(NSEQ) for h in range(HB)]

    def block(rb, project_next):
        slot = rb % 2
        for ci in range(CPB):
            loads = []
            for s in range(NSEQ):
                rows = slice(s * RB + ci * C, s * RB + (ci + 1) * C)
                for h in range(HB):
                    loads.append(tuple(proj_ref[slot, rows, sec * WC + h * LANES:sec * WC + (h + 1) * LANES]
                                       for sec in range(4)))
            if project_next:
                project(rb + 1, ci, 1 - slot)
            outs = _hgrn_chunk_group(loads, params, [st_ref[ch] for ch in range(NSEQ * HB)], C=C, valid_len=C)
            orow = pl.ds(pl.multiple_of(rb * RB + ci * C, C), C)
            for s in range(NSEQ):
                for h in range(HB):
                    ch = s * HB + h
                    st_ref[ch] = outs[ch][1]
                    o_ref[s, orow, h * LANES:(h + 1) * LANES] = outs[ch][0].astype(o_ref.dtype)

    def body(rb, carry):
        block(rb, True)
        return carry

    lax.fori_loop(0, NB - 1, body, 0)
    block(NB - 1, False)

    for s in range(NSEQ):
        for h in range(HB):
            s_ref[s, h] = st_ref[s * HB + h].T


def _hgrn_fused(x, g, w, layer, B, L, lbp, ng):
    D = x.shape[1]
    H = ng.shape[1] // LANES
    HB, NSEQ, RB, C = FUSED_HB, FUSED_NSEQ, FUSED_RB, HG_CHUNK
    WC = HB * LANES
    nhb = H // HB
    assert B % NSEQ == 0 and H % HB == 0 and L % RB == 0 and RB // C == 4
    loglb, log1m, onem = lbp

    def sect(s):
        return pl.BlockSpec((None, D, WC), lambda sg, hb, s=s: (layer, 0, s * nhb + hb))

    vec = pl.BlockSpec((1, WC), lambda sg, hb: (0, hb))
    est = (NSEQ * L * D * 2 + 2 * RB * D * 4 + 2 * NSEQ * RB * 4 * WC * 4 + 2 * 4 * D * WC * 4
           + 2 * NSEQ * L * WC * 2 + (6 << 20))
    return pl.pallas_call(
        functools.partial(_hgrn_fused_kernel, L=L, C=C, RB=RB, NSEQ=NSEQ, HB=HB),
        grid=(B // NSEQ, nhb),
        in_specs=[pl.BlockSpec(memory_space=pl.ANY), pl.BlockSpec((1, D), lambda sg, hb: (0, 0)),
                  sect(0), sect(1), sect(2), sect(3), vec, vec, vec, vec],
        out_specs=[pl.BlockSpec((NSEQ, L, WC), lambda sg, hb: (sg, 0, hb)),
                   pl.BlockSpec((NSEQ, HB, HG_DK, HG_DV), lambda sg, hb: (sg, hb, 0, 0))],
        out_shape=[jax.ShapeDtypeStruct((B, L, H * LANES), BF16),
                   jax.ShapeDtypeStruct((B, H, HG_DK, HG_DV), F32)],
        scratch_shapes=[pltpu.VMEM((NSEQ * L, D), BF16), pltpu.VMEM((2, RB, D), F32),
                        pltpu.VMEM((2, NSEQ * RB, 4 * WC), F32), pltpu.VMEM((NSEQ * HB, HG_DV, HG_DK), F32),
                        pltpu.SemaphoreType.DMA((2,))],
        compiler_params=pltpu.CompilerParams(
            dimension_semantics=("arbitrary", "arbitrary"),
            vmem_limit_bytes=_vmem_limit(est)),
        name="hgrn_fused",
    )(x, g, w, w, w, w, loglb, log1m, onem, ng)


def _sink_softmax(scores, sink):
    m = sink
    for s in scores:
        m = jnp.maximum(m, jnp.max(s, axis=-1, keepdims=True))
    ps = [jnp.exp(s - m) for s in scores]
    denom = jnp.exp(sink - m)
    for p in ps:
        denom = denom + jnp.sum(p, axis=-1, keepdims=True)
    return [p / denom for p in ps]


def _attn_prompt_kernel(sinks_ref, q_ref, kc_ref, kp_ref, vc_ref, vp_ref, o_ref, *, n_kv, group):
    i = pl.program_id(1)
    W = WINDOW
    HD = HEAD_DIM
    j = lax.broadcasted_iota(jnp.int32, (2 * W, W), 0)
    r = lax.broadcasted_iota(jnp.int32, (2 * W, W), 1)
    mask = (j >= jnp.maximum(r, jnp.where(i > 0, 0, W))) & (j <= r + W)
    scale = HD ** -0.5
    zeros = jnp.zeros((HD, W), F32)

    s_all, vt_all = [], []
    for c in range(n_kv // 2):
        cs = slice(c * LANES, (c + 1) * LANES)
        ktile = jnp.concatenate([kp_ref[:, cs], kc_ref[:, cs]], axis=0).astype(BF16)
        vt_all.append(jnp.concatenate([vp_ref[:, cs], vc_ref[:, cs]], axis=0).T.astype(BF16))
        for par in range(2):
            kh = 2 * c + par
            rhs = []
            for pr in range(group // 2):
                c0 = (kh * group + 2 * pr) * HD
                pt = (q_ref[:, c0:c0 + LANES] * scale).T
                for half in range(2):
                    blk = pt[half * HD:(half + 1) * HD]
                    rhs.append(jnp.concatenate([blk, zeros] if par == 0 else [zeros, blk], axis=0))
            s_all.append(_dot(ktile, jnp.concatenate(rhs, axis=1).astype(BF16)))

    p_all, inv_all = [], []
    for kh in range(n_kv):
        pn, inv = [], []
        for hh in range(group):
            sink = sinks_ref[kh * group + hh]
            sm = jnp.where(mask, s_all[kh][:, hh * W:(hh + 1) * W], -jnp.inf)
            mx = jnp.maximum(jnp.max(sm, axis=0, keepdims=True), sink)
            p = jnp.exp(sm - mx)
            inv.append(1.0 / (jnp.sum(p, axis=0, keepdims=True) + jnp.exp(sink - mx)))
            pn.append(p.astype(BF16))
        p_all.append(jnp.concatenate(pn, axis=1))
        inv_all.append(jnp.concatenate(inv, axis=1))

    for kh in range(n_kv):
        par = kh % 2
        o_t = _dot(vt_all[kh // 2], p_all[kh])[par * HD:(par + 1) * HD] * inv_all[kh]
        for pr in range(group // 2):
            pair_t = jnp.concatenate([o_t[:, (2 * pr) * W:(2 * pr + 1) * W],
                                      o_t[:, (2 * pr + 1) * W:(2 * pr + 2) * W]], axis=0)
            c0 = (kh * group + 2 * pr) * HD
            o_ref[:, c0:c0 + LANES] = pair_t.T.astype(o_ref.dtype)


def _attn_prompt(p, sinks, B, L, n_q, n_kv):
    W = WINDOW
    nb = L // W
    QW = n_q * HEAD_DIM
    KW = n_kv * HEAD_DIM
    kblk = QW // KW
    vblk = (QW + KW) // KW
    est = 2 * (W * QW * 4 + 4 * W * KW * 4 + W * QW * 2) + (16 << 20)
    return pl.pallas_call(
        functools.partial(_attn_prompt_kernel, n_kv=n_kv, group=n_q // n_kv),
        grid=(B, nb),
        in_specs=[
            pl.BlockSpec(memory_space=pltpu.SMEM),
            pl.BlockSpec((W, QW), lambda b, i: (b * nb + i, 0)),
            pl.BlockSpec((W, KW), lambda b, i: (b * nb + i, kblk)),
            pl.BlockSpec((W, KW), lambda b, i: (b * nb + jnp.maximum(i - 1, 0), kblk)),
            pl.BlockSpec((W, KW), lambda b, i: (b * nb + i, vblk)),
            pl.BlockSpec((W, KW), lambda b, i: (b * nb + jnp.maximum(i - 1, 0), vblk)),
        ],
        out_specs=pl.BlockSpec((W, QW), lambda b, i: (b * nb + i, 0)),
        out_shape=jax.ShapeDtypeStruct((B * L, QW), BF16),
        compiler_params=pltpu.CompilerParams(
            dimension_semantics=("parallel", "parallel"),
            vmem_limit_bytes=_vmem_limit(est)),
        name="swa_prompt",
    )(sinks, p, p, p, p, p)


def _attn_sample_kernel(sinks_ref, qkv_ref, ck_ref, cv_ref, o_ref, *, n_q, n_kv, n_new):
    R = qkv_ref.shape[1]
    W = WINDOW
    HD = HEAD_DIM
    group = n_q // n_kv
    npair = group // 2
    QW = n_q * HD
    KW = n_kv * HD
    RR = npair * R
    scale = HD ** -0.5
    row = lax.broadcasted_iota(jnp.int32, (RR, 2 * W), 0)
    r = row & (R - 1)
    j = lax.broadcasted_iota(jnp.int32, (RR, 2 * W), 1)
    mask = (j >= r) & (j <= r + W) & (j < W + n_new)
    pair_of_row = lax.broadcasted_iota(jnp.int32, (RR, 1), 0) // R
    first = lax.broadcasted_iota(jnp.int32, (2 * W, LANES), 1) < HD
    zpad = jnp.zeros((W - R, LANES), F32)

    s_all, v_all = [], []
    for c in range(n_kv // 2):
        cs = slice(c * LANES, (c + 1) * LANES)
        kn = qkv_ref[0, :, QW + c * LANES:QW + (c + 1) * LANES]
        vn = qkv_ref[0, :, QW + KW + c * LANES:QW + KW + (c + 1) * LANES]
        ktile = jnp.concatenate([ck_ref[0, :, cs], kn, zpad], axis=0)
        vtile = jnp.concatenate([cv_ref[0, :, cs], vn, zpad], axis=0)
        kswap = pltpu.roll(ktile, HD, 1)
        vswap = pltpu.roll(vtile, HD, 1)
        for par in range(2):
            kh = 2 * c + par
            k_lo, k_hi = (ktile, kswap) if par == 0 else (kswap, ktile)
            v_lo, v_hi = (vtile, vswap) if par == 0 else (vswap, vtile)
            kcat = jnp.concatenate([jnp.where(first, k_lo, 0.0), jnp.where(first, 0.0, k_hi)], axis=0)
            vcat = jnp.concatenate([jnp.where(first, v_lo, 0.0), jnp.where(first, 0.0, v_hi)], axis=0)
            qs = jnp.concatenate(
                [qkv_ref[0, :, (kh * group + 2 * pr) * HD:(kh * group + 2 * pr) * HD + LANES]
                 for pr in range(npair)], axis=0) * scale
            s_all.append(_dot_nt(qs.astype(BF16), kcat.astype(BF16)))
            v_all.append(vcat.astype(BF16))

    p_all = []
    for kh in range(n_kv):
        halves = []
        for hf in range(2):
            sink = jnp.full((RR, 1), sinks_ref[kh * group + hf], F32)
            for pr in range(1, npair):
                sink = jnp.where(pair_of_row == pr, sinks_ref[kh * group + 2 * pr + hf], sink)
            sm = jnp.where(mask, s_all[kh][:, hf * 2 * W:(hf + 1) * 2 * W], -jnp.inf)
            (p,) = _sink_softmax([sm], sink)
            halves.append(p.astype(BF16))
        p_all.append(jnp.concatenate(halves, axis=1))

    for kh in range(n_kv):
        o = _dot(p_all[kh], v_all[kh])
        for pr in range(npair):
            c0 = (kh * group + 2 * pr) * HD
            o_ref[0, :, c0:c0 + LANES] = o[pr * R:(pr + 1) * R]


def _attn_sample(qkv, ck, cv, layer, sinks, n_q, n_kv, n_new):
    B, R, NW = qkv.shape
    QW = n_q * HEAD_DIM
    KW = n_kv * HEAD_DIM
    return pl.pallas_call(
        functools.partial(_attn_sample_kernel, n_q=n_q, n_kv=n_kv, n_new=n_new),
        grid=(B,),
        in_specs=[
            pl.BlockSpec(memory_space=pltpu.SMEM),
            pl.BlockSpec((1, R, NW), lambda b: (b, 0, 0)),
            pl.BlockSpec((None, 1, WINDOW, KW), lambda b: (layer, b, 0, 0)),
            pl.BlockSpec((None, 1, WINDOW, KW), lambda b: (layer, b, 0, 0)),
        ],
        out_specs=pl.BlockSpec((1, R, QW), lambda b: (b, 0, 0)),
        out_shape=jax.ShapeDtypeStruct((B, R, QW), F32),
        compiler_params=pltpu.CompilerParams(dimension_semantics=("parallel",)),
        name="swa_sample",
    )(sinks, qkv, ck, cv)


def _rope_tables(pos):
    half = HEAD_DIM // 2
    inv = jnp.power(ROPE_THETA, -jnp.arange(half, dtype=F32) * (2.0 / HEAD_DIM))
    ang = pos.astype(F32)[:, None] * inv[None, :]
    cos = jnp.cos(ang)
    sin = jnp.sin(ang)
    reps = LANES // HEAD_DIM
    cos_t = jnp.tile(jnp.concatenate([cos, cos], axis=1), (1, reps))
    sin_t = jnp.tile(jnp.concatenate([-sin, sin], axis=1), (1, reps))
    return (jnp.stack([cos_t, jnp.ones_like(cos_t)]), jnp.stack([sin_t, jnp.zeros_like(sin_t)]))


def kernel(x_prompt, x_sample, state_hgrn, cache_k_win, cache_v_win, norm_gains, w_ffn_in, w_ffn_out,
           w_hgrn_in, hgrn_lb_logits, hgrn_norm_gain, w_hgrn_out, w_attn_in, attn_sinks, w_attn_out):
    B, L, D = x_prompt.shape
    SB, SL, _ = x_sample.shape
    depth = norm_gains.shape[0]
    TP, TS = B * L, SB * SL
    H = hgrn_norm_gain.shape[1] // HG_DV
    n_kv = cache_k_win.shape[3]
    n_q = w_attn_out.shape[1] // HEAD_DIM
    QW, KW = n_q * HEAD_DIM, n_kv * HEAD_DIM
    buf = cache_k_win.shape[2]

    x = jnp.concatenate([x_prompt.reshape(TP, D), x_sample.reshape(TS, D)], axis=0)

    lb_all = jnp.cumsum(jax.nn.softmax(hgrn_lb_logits.astype(F32), axis=0), axis=0)
    lb_all = lb_all - lb_all[0:1]

    pos = jnp.concatenate([jnp.tile(jnp.arange(L, dtype=jnp.int32), B),
                           jnp.tile(PAST_LEN + jnp.arange(SL, dtype=jnp.int32), SB)])
    rope = _rope_tables(pos)

    hg_p, hg_s, kp_l, vp_l, ks_l, vs_l = [], [], [], [], [], []
    for layer in range(depth):
        g = norm_gains[layer]
        x = _ffn_half(x, g[0:1], g[1:2], w_ffn_in, w_ffn_out, layer, 0)
        if layer % N_MIXERS == 0:
            a = layer // N_MIXERS
            lb = lb_all[a][None, :]
            lbp = (jnp.log(lb), jnp.log1p(-lb), 1.0 - lb)
            ng = hgrn_norm_gain[a][None, :].astype(F32)
            o_p, s_p = _hgrn_fused(x, g[2:3], w_hgrn_in, a, B, L, lbp, ng)
            o_p = o_p.reshape(TP, -1)
            p_s = _proj(x[TP:], g[2:3], w_hgrn_in, a, (TS, HGRN_SAMPLE_PROJ_TN))
            ps = jnp.pad(p_s.reshape(SB, SL, -1), ((0, 0), (0, SAMPLE_PAD - SL), (0, 0)))
            o_s, s_s = _hgrn_scan(ps.reshape(SB * SAMPLE_PAD, -1), SB, SAMPLE_PAD, SAMPLE_PAD, H, SL,
                                  lbp, ng, (state_hgrn, a), F32)
            o_s = o_s.reshape(SB, SAMPLE_PAD, -1)[:, :SL].reshape(TS, -1).astype(BF16)
            x = _outproj(o_p, o_s, w_hgrn_out, x, g[3:4], a)
            hg_p.append(s_p.astype(state_hgrn.dtype))
            hg_s.append(s_s.astype(state_hgrn.dtype))
        else:
            bb = layer // N_MIXERS
            p = _proj(x, g[2:3], w_attn_in, bb, ATTN_PROJ_TILES, rope=rope, n_rope_cols=QW + KW)
            sinks = attn_sinks[bb].astype(F32)
            o_p = _attn_prompt(p, sinks, B, L, n_q, n_kv)
            pn = p[TP:].reshape(SB, SL, -1)
            ck = cache_k_win.reshape(-1, SB, buf, KW)
            cv = cache_v_win.reshape(-1, SB, buf, KW)
            o_s = _attn_sample(jnp.pad(pn, ((0, 0), (0, 8 - SL), (0, 0))), ck, cv, bb, sinks, n_q, n_kv, SL)
            x = _outproj(o_p, o_s[:, :SL].reshape(TS, QW).astype(BF16), w_attn_out, x, g[3:4], bb)
            tails = [p[(s + 1) * L - buf:(s + 1) * L, QW:] for s in range(B)]
            kp_l.append(jnp.stack([t[:, :KW] for t in tails]).reshape(B, buf, n_kv, HEAD_DIM))
            vp_l.append(jnp.stack([t[:, KW:] for t in tails]).reshape(B, buf, n_kv, HEAD_DIM))
            k_new = pn[:, :, QW:QW + KW].reshape(SB, SL, n_kv, HEAD_DIM)
            v_new = pn[:, :, QW + KW:].reshape(SB, SL, n_kv, HEAD_DIM)
            ks_l.append(jnp.concatenate([cache_k_win[bb], k_new], axis=1)[:, -buf:])
            vs_l.append(jnp.concatenate([cache_v_win[bb], v_new], axis=1)[:, -buf:])
        x = _ffn_half(x, g[4:5], g[5:6], w_ffn_in, w_ffn_out, layer, 1)

    yp = x[:TP].reshape(B, L, D)
    ys = x[TP:].reshape(SB, SL, D)
    return (yp, ys, jnp.stack(hg_p), jnp.stack(hg_s), jnp.stack(kp_l), jnp.stack(vp_l),
            jnp.stack(ks_l), jnp.stack(vs_l))
```

```python
import functools
import math

import numpy as np
import jax
import jax.numpy as jnp
from jax import lax
from jax.experimental import pallas as pl
from jax.experimental.pallas import tpu as pltpu

F32 = jnp.float32
BF16 = jnp.bfloat16

NORM_EPS = 1e-6
HEAD_DIM = 64
HG_DK = 128
HG_DV = 128
HG_CHUNK = 64
WINDOW = 128
PAST_LEN = 16384
ROPE_THETA = 10000.0
N_MIXERS = 2

LANES = 128
SUBLANES = 8
VMEM_BYTES_V7X = 64 * 1024 * 1024
VMEM_HEADROOM = 6 * 1024 * 1024

FFN_TM, FFN_TF = 1040, 256
FFN_RC = 208
HGRN_SAMPLE_PROJ_TN = 1024
FUSED_HB, FUSED_NSEQ, FUSED_RB = 2, 2, 256
ATTN_PROJ_TILES = (1040, 512)
ROPE_ROW_CHUNK = 208
OUT_TM = 512
SAMPLE_PAD = 16


def _vmem_limit(nbytes):
    return int(min(VMEM_BYTES_V7X - VMEM_HEADROOM, max(nbytes * 5 // 4, 16 * 1024 * 1024)))


def _rms(x, g):
    return x * lax.rsqrt(jnp.mean(x * x, axis=-1, keepdims=True) + NORM_EPS) * g


def _dot(a, b):
    return jnp.dot(a, b, preferred_element_type=F32)


def _dot_nt(a, b):
    return lax.dot_general(a, b, (((1,), (1,)), ((), ())), preferred_element_type=F32)


def _dot_tn(a, b):
    return lax.dot_general(a, b, (((0,), (0,)), ((), ())), preferred_element_type=F32)


def _ffn_kernel(*refs, nt, nj, tm, rc, in_split, out_split):
    refs = list(refs)
    x_refs = [refs.pop(0) for _ in range(1 if in_split is None else 2)]
    gpre_ref, gpost_ref, wg_ref, wu_ref, wo_ref = (refs.pop(0) for _ in range(5))
    y_refs = [refs.pop(0) for _ in range(1 if out_split is None else 2)]
    acc_ref, xn_ref, xc_ref, yc_ref, in_sem, out_sem = refs

    i = pl.program_id(0)
    j = pl.program_id(1)
    nch = tm // rc
    slot = i % 2
    pslot = 1 - slot
    k = j - 1
    gpre = gpre_ref[...]
    gpost = gpost_ref[...]

    def in_copy(row0, b):
        return pltpu.make_async_copy(x_refs[0].at[pl.ds(row0, rc)], xc_ref.at[b], in_sem.at[b])

    def out_copy(row0, b):
        return pltpu.make_async_copy(yc_ref.at[b], y_refs[0].at[pl.ds(row0, rc)], out_sem.at[b])

    def crossing_row(split):
        if split is None:
            return None
        assert split % rc != 0 and nt * tm - split < rc, "exactly one chunk must cross the split point"
        return (split // rc) * rc

    in_cross = crossing_row(in_split)
    out_cross = crossing_row(out_split)
    assert out_cross is None or out_cross >= (nt - 1) * tm, "only the last row tile may cross the output split"

    def in_copies_crossing(b):
        na = in_split - in_cross
        return [pltpu.make_async_copy(x_refs[0].at[pl.ds(in_cross, na)], xc_ref.at[b, pl.ds(0, na)], in_sem.at[b]),
                pltpu.make_async_copy(x_refs[1].at[pl.ds(0, rc - na)], xc_ref.at[b, pl.ds(na, rc - na)],
                                      in_sem.at[b])]

    def out_copies_crossing(b):
        na = out_split - out_cross
        return [pltpu.make_async_copy(yc_ref.at[b, pl.ds(0, na)], y_refs[0].at[pl.ds(out_cross, na)], out_sem.at[b]),
                pltpu.make_async_copy(yc_ref.at[b, pl.ds(na, rc - na)], y_refs[1].at[pl.ds(0, rc - na)],
                                      out_sem.at[b])]

    def static_in_copies(row0, b):
        return in_copies_crossing(b) if row0 == in_cross else [in_copy(row0, b)]

    def static_out_copies(row0, b):
        return out_copies_crossing(b) if row0 == out_cross else [out_copy(row0, b)]

    def task_row(t):
        return jnp.where(t < nch, (i - 1) * tm + t * rc, (i + 1) * tm + (t - nch) * rc)

    def task_live(t):
        return (t >= 0) & (t < 2 * nch) & jnp.where(t < nch, i >= 1, i <= nt - 2)

    def task_crosses(t):
        return False if in_cross is None else task_row(t) == in_cross

    @pl.when((i == 0) & (j == 0))
    def _():
        for c in range(nch):
            cps = static_in_copies(c * rc, c % 2)
            for cp in cps:
                cp.start()
            for cp in cps:
                cp.wait()
            xn_ref[0, c * rc:(c + 1) * rc] = _rms(xc_ref[c % 2], gpre).astype(BF16)

    @pl.when(task_live(k) & jnp.logical_not(task_crosses(k)))
    def _():
        in_copy(0, k & 1).wait()

    @pl.when(task_live(k) & (k >= 2) & (k < nch))
    def _():
        out_copy(0, k & 1).wait()

    @pl.when((j == nch + 1) & (i >= 1))
    def _():
        out_copy(0, 0).wait()
        out_copy(0, 1).wait()

    @pl.when(task_live(j) & jnp.logical_not(task_crosses(j)))
    def _():
        in_copy(task_row(j), j & 1).start()

    if in_cross is not None:
        @pl.when(task_live(k) & task_crosses(k))
        def _():
            for cp in in_copies_crossing(k & 1):
                cp.wait()

        @pl.when(task_live(j) & task_crosses(j))
        def _():
            for cp in in_copies_crossing(j & 1):
                cp.start()

    def accumulate(first):
        xn = xn_ref[slot]
        hg = _dot(xn, wg_ref[...].astype(BF16))
        hu = _dot(xn, wu_ref[...].astype(BF16))
        act = (hg * jax.nn.sigmoid(hg) * hu).astype(BF16)
        contrib = _dot(act, wo_ref[...].astype(BF16))
        if first:
            acc_ref[slot] = contrib
        else:
            acc_ref[slot] += contrib

    is_epi = (k >= 0) & (k < nch) & (i >= 1)
    is_pro = (k >= nch) & (k < 2 * nch) & (i <= nt - 2)

    @pl.when(j == 0)
    def _():
        accumulate(True)

    @pl.when(is_epi)
    def _():
        accumulate(False)
        b = k & 1
        rows = pl.ds(pl.multiple_of(k * rc, rc), rc)
        yc_ref[b] = xc_ref[b] + 0.5 * _rms(acc_ref[pslot, rows], gpost)
        out_copy((i - 1) * tm + k * rc, b).start()

    @pl.when(is_pro)
    def _():
        accumulate(False)
        rows = pl.ds(pl.multiple_of((k - nch) * rc, rc), rc)
        xn_ref[pslot, rows] = _rms(xc_ref[k & 1], gpre).astype(BF16)

    @pl.when((j >= 1) & jnp.logical_not(is_epi) & jnp.logical_not(is_pro))
    def _():
        accumulate(False)

    @pl.when((i == nt - 1) & (j == nj - 1))
    def _():
        last = (nt - 1) % 2
        for c in range(nch):
            b = c % 2
            row0 = (nt - 1) * tm + c * rc
            cps = static_in_copies(row0, b)
            for cp in cps:
                cp.start()
            for cp in cps:
                cp.wait()
            yc_ref[b] = xc_ref[b] + 0.5 * _rms(acc_ref[last, c * rc:(c + 1) * rc], gpost)
            ocs = static_out_copies(row0, b)
            for oc in ocs:
                oc.start()
            for oc in ocs:
                oc.wait()


def _ffn_half(x, g_pre, g_post, w_in, w_out, layer, half, out_split=None):
    xs = x if isinstance(x, (tuple, list)) else (x,)
    in_split = xs[0].shape[0] if len(xs) == 2 else None
    T = sum(a.shape[0] for a in xs)
    D = xs[0].shape[1]
    F = w_out.shape[2]
    TM, TF, RC = FFN_TM, FFN_TF, FFN_RC
    nj = F // TF
    nt = T // TM
    assert T % TM == 0 and F % TF == 0 and TM % RC == 0 and TM // RC >= 2 and nj >= 2 * (TM // RC) + 2
    est = (2 * TM * D * (4 + 2) + 4 * RC * D * 4 + 2 * 3 * D * TF * 4
           + 3 * D * TF * 2 + 4 * TM * TF * 4 + TM * D * 4)
    any_spec = pl.BlockSpec(memory_space=pl.ANY)
    if out_split is None:
        out_specs, out_shape = any_spec, jax.ShapeDtypeStruct((T, D), F32)
    else:
        out_specs = [any_spec, any_spec]
        out_shape = [jax.ShapeDtypeStruct((out_split, D), F32), jax.ShapeDtypeStruct((T - out_split, D), F32)]
    return pl.pallas_call(
        functools.partial(_ffn_kernel, nt=nt, nj=nj, tm=TM, rc=RC, in_split=in_split, out_split=out_split),
        grid=(nt, nj),
        in_specs=[any_spec] * len(xs) + [
            pl.BlockSpec((1, D), lambda i, j: (0, 0)),
            pl.BlockSpec((1, D), lambda i, j: (0, 0)),
            pl.BlockSpec((None, None, D, TF), lambda i, j: (layer, half, 0, j)),
            pl.BlockSpec((None, None, D, TF), lambda i, j: (layer, half, 0, nj + j)),
            pl.BlockSpec((None, None, TF, D), lambda i, j: (layer, half, j, 0)),
        ],
        out_specs=out_specs,
        out_shape=out_shape,
        scratch_shapes=[pltpu.VMEM((2, TM, D), F32), pltpu.VMEM((2, TM, D), BF16),
                        pltpu.VMEM((2, RC, D), F32), pltpu.VMEM((2, RC, D), F32),
                        pltpu.SemaphoreType.DMA((2,)), pltpu.SemaphoreType.DMA((2,))],
        compiler_params=pltpu.CompilerParams(
            dimension_semantics=("arbitrary", "arbitrary"),
            vmem_limit_bytes=_vmem_limit(est)),
        name="ffn_half",
    )(*xs, g_pre, g_post, w_in, w_in, w_out)


def _rope_tile(y, cos, sin_signed):
    lane = lax.broadcasted_iota(jnp.int32, y.shape, 1)
    first_half = (lane & (HEAD_DIM - 1)) < (HEAD_DIM // 2)
    rot = jnp.where(first_half,
                    pltpu.roll(y, LANES - HEAD_DIM // 2, 1),
                    pltpu.roll(y, HEAD_DIM // 2, 1))
    return y * cos + rot * sin_signed


def _proj_kernel(x_ref, g_ref, w_ref, o_ref, xn_ref):
    @pl.when(pl.program_id(1) == 0)
    def _():
        xn_ref[...] = _rms(x_ref[...], g_ref[...]).astype(BF16)

    o_ref[...] = _dot(xn_ref[...], w_ref[...].astype(BF16))


def _proj_rope_kernel(x_ref, g_ref, w_ref, cos_ref, sin_ref, o_ref, xn_ref, *, row_chunk):
    @pl.when(pl.program_id(1) == 0)
    def _():
        xn_ref[...] = _rms(x_ref[...], g_ref[...]).astype(BF16)

    wb = w_ref[...].astype(BF16)
    for r0 in range(0, xn_ref.shape[0], row_chunk):
        rows = slice(r0, r0 + row_chunk)
        y = _dot(xn_ref[rows], wb)
        cos = cos_ref[rows]
        sin = sin_ref[rows]
        for c in range(y.shape[1] // LANES):
            cols = slice(c * LANES, (c + 1) * LANES)
            o_ref[rows, cols] = _rope_tile(y[:, cols], cos, sin)


def _proj(x, g, w, layer, tiles, rope=None, n_rope_cols=0):
    T, D = x.shape
    N = w.shape[2]
    TM, TN = tiles
    assert T % TM == 0 and N % TN == 0 and n_rope_cols % TN == 0
    est = 2 * TM * D * 4 + TM * D * 2 + 2 * D * TN * 4 + D * TN * 2 + 4 * TM * TN * 4
    in_specs = [
        pl.BlockSpec((TM, D), lambda i, j: (i, 0)),
        pl.BlockSpec((1, D), lambda i, j: (0, 0)),
        pl.BlockSpec((None, D, TN), lambda i, j: (layer, 0, j)),
    ]
    args = [x, g, w]
    if rope is None:
        body = _proj_kernel
    else:
        assert TM % ROPE_ROW_CHUNK == 0
        body = functools.partial(_proj_rope_kernel, row_chunk=ROPE_ROW_CHUNK)
        n_rope_blocks = n_rope_cols // TN
        table = pl.BlockSpec((None, TM, LANES), lambda i, j: (jnp.where(j < n_rope_blocks, 0, 1), i, 0))
        in_specs += [table, table]
        args += list(rope)
    return pl.pallas_call(
        body,
        grid=(T // TM, N // TN),
        in_specs=in_specs,
        out_specs=pl.BlockSpec((TM, TN), lambda i, j: (i, j)),
        out_shape=jax.ShapeDtypeStruct((T, N), F32),
        scratch_shapes=[pltpu.VMEM((TM, D), BF16)],
        compiler_params=pltpu.CompilerParams(
            dimension_semantics=("parallel", "arbitrary"),
            vmem_limit_bytes=_vmem_limit(est)),
        name="mixer_in_proj",
    )(*args)


def _outproj_kernel(op_ref, os_ref, w_ref, x_ref, g_ref, y_ref, wbf_ref, *, n_prompt_tiles, ts):
    i = pl.program_id(0)

    @pl.when(i == 0)
    def _():
        wbf_ref[...] = w_ref[...].astype(BF16)

    @pl.when(i < n_prompt_tiles)
    def _():
        y_ref[...] = x_ref[...] + _rms(_dot(op_ref[...], wbf_ref[...]), g_ref[...])

    @pl.when(i == n_prompt_tiles)
    def _():
        y_ref[0:ts] = x_ref[0:ts] + _rms(_dot(os_ref[...], wbf_ref[...]), g_ref[...])


def _outproj(o_prompt, o_sample, w, x, g, layer):
    T, D = x.shape
    TP, K = o_prompt.shape
    TS = o_sample.shape[0]
    TM = OUT_TM
    assert TP % TM == 0 and TS <= TM and TS % SUBLANES == 0 and TP + TS == T
    npt = TP // TM
    est = K * D * (4 + 2) + 2 * TM * K * 2 + 2 * TS * K * 2 + 5 * TM * D * 4
    return pl.pallas_call(
        functools.partial(_outproj_kernel, n_prompt_tiles=npt, ts=TS),
        grid=(npt + 1,),
        in_specs=[
            pl.BlockSpec((TM, K), lambda i: (jnp.minimum(i, npt - 1), 0)),
            pl.BlockSpec((TS, K), lambda i: (0, 0)),
            pl.BlockSpec((None, K, D), lambda i: (layer, 0, 0), pipeline_mode=pl.Buffered(1)),
            pl.BlockSpec((TM, D), lambda i: (i, 0)),
            pl.BlockSpec((1, D), lambda i: (0, 0)),
        ],
        out_specs=pl.BlockSpec((TM, D), lambda i: (i, 0)),
        out_shape=jax.ShapeDtypeStruct((T, D), F32),
        scratch_shapes=[pltpu.VMEM((K, D), BF16)],
        compiler_params=pltpu.CompilerParams(
            dimension_semantics=("arbitrary",),
            vmem_limit_bytes=_vmem_limit(est)),
        name="mixer_out_proj",
    )(o_prompt, o_sample, w, x, g)


def _tiles(x):
    return [x[SUBLANES * j:SUBLANES * (j + 1)] for j in range(x.shape[0] // SUBLANES)]


def _bcast_row(t, i):
    return jnp.broadcast_to(t[i:i + 1], t.shape)


def _tile_prefix(t, sub):
    p = t + jnp.where(sub >= 1, pltpu.roll(t, 1, 0), 0.0)
    p = p + jnp.where(sub >= 2, pltpu.roll(p, 2, 0), 0.0)
    return p + jnp.where(sub >= 4, pltpu.roll(p, 4, 0), 0.0)


def _hgrn_gates(fz, loglb, log1m, onem, valid):
    e = jnp.exp(-jnp.abs(fz))
    lsig = jnp.minimum(fz, 0.0) - jnp.log(1.0 + e)
    ct = log1m + lsig
    g = jnp.maximum(loglb, ct) + jnp.log(1.0 + jnp.exp(-jnp.abs(loglb - ct)))
    r = 1.0 / (1.0 + e)
    k = onem * jnp.where(fz > 0.0, e * r, r)
    if valid is not None:
        g = jnp.where(valid, g, 0.0)
        k = jnp.where(valid, k, 0.0)
    return g, k


def _hgrn_level_operands(m, qt, kt, gt, bt, sub):
    nv = len(qt)
    zero = jnp.zeros_like(qt[0])
    qm, km = [], []
    if m >= SUBLANES:
        mv = m // SUBLANES
        for j in range(nv):
            jr = (j // (2 * mv)) * (2 * mv) + mv - 1
            ref = _bcast_row(bt[jr], SUBLANES - 1)
            if j % (2 * mv) >= mv:
                qm.append(qt[j] * jnp.exp(bt[j] - ref))
                km.append(zero)
            else:
                qm.append(zero)
                km.append(kt[j] * jnp.exp(ref - bt[j]))
        return qm, km
    upper = (sub & m) != 0
    for j in range(nv):
        if m == 1:
            e = jnp.exp(gt[j])
            qm.append(jnp.where(upper, qt[j] * e, 0.0))
            km.append(jnp.where(upper, 0.0, kt[j]))
            continue
        if m == 4:
            ref = _bcast_row(bt[j], 3)
        else:
            ref = jnp.where(sub < 4, _bcast_row(bt[j], 1), _bcast_row(bt[j], 5))
        e = jnp.exp(-jnp.abs(bt[j] - ref))
        qm.append(jnp.where(upper, qt[j] * e, 0.0))
        km.append(jnp.where(upper, 0.0, kt[j] * e))
    return qm, km


def _hgrn_chunk_group(loads, params, sts, *, C, valid_len):
    nh = len(loads)
    sub = lax.broadcasted_iota(jnp.int32, (SUBLANES, LANES), 0)
    valid = None
    if valid_len < C:
        valid = lax.broadcasted_iota(jnp.int32, (C, LANES), 0) < valid_len
    ti = lax.broadcasted_iota(jnp.int32, (C, C), 0)
    si = lax.broadcasted_iota(jnp.int32, (C, C), 1)

    gk = [_hgrn_gates(loads[h][1], params[h][0], params[h][1], params[h][2], valid) for h in range(nh)]

    gts, bts = [], []
    for h in range(nh):
        gt = _tiles(gk[h][0])
        bt = []
        for j, t in enumerate(gt):
            p = _tile_prefix(t, sub)
            bt.append(p if j == 0 else p + _bcast_row(bt[j - 1], SUBLANES - 1))
        gts.append(gt)
        bts.append(bt)

    scores = []
    for h in range(nh):
        q, k = loads[h][0], gk[h][1]
        qt, kt = _tiles(q), _tiles(k)
        a = jnp.where(ti == si, _dot_nt(q.astype(BF16), k.astype(BF16)), 0.0)
        m = C // 2
        while m >= 1:
            qm, km = _hgrn_level_operands(m, qt, kt, gts[h], bts[h], sub)
            am = _dot_nt(jnp.concatenate(qm, axis=0).astype(BF16), jnp.concatenate(km, axis=0).astype(BF16))
            if 2 * m < C:
                am = jnp.where((ti & (-2 * m)) == (si & (-2 * m)), am, 0.0)
            a = a + am
            m //= 2
        scores.append(a)

    outs = []
    for h in range(nh):
        q, _, v, gz = loads[h]
        k = gk[h][1]
        ng = params[h][3]
        st = sts[h]
        b = jnp.concatenate(bts[h], axis=0)
        o = _dot(scores[h].astype(BF16), v.astype(BF16)) + _dot_nt((q * jnp.exp(b)).astype(BF16), st.astype(BF16))
        bl = b[C - 1:C]
        kb = (k * jnp.exp(bl - b)).astype(BF16)
        st_new = st * jnp.exp(bl) + _dot_tn(v.astype(BF16), kb)
        on = o * lax.rsqrt(jnp.mean(o * o, axis=-1, keepdims=True) + NORM_EPS) * ng
        outs.append((on * (gz * jax.nn.sigmoid(gz)), st_new))
    return outs


def _hgrn_kernel(*refs, C, NC, HB, valid_len, has_s0):
    if has_s0:
        (q_ref, f_ref, i_ref, g_ref, loglb_ref, log1m_ref, onem_ref, ng_ref, s0_ref,
         o_ref, s_ref, st_ref) = refs
    else:
        (q_ref, f_ref, i_ref, g_ref, loglb_ref, log1m_ref, onem_ref, ng_ref,
         o_ref, s_ref, st_ref) = refs
        s0_ref = None

    for h in range(HB):
        if has_s0:
            st_ref[h] = s0_ref[0, h].T
        else:
            st_ref[h] = jnp.zeros((HG_DV, HG_DK), F32)

    def body(c, carry):
        rows = pl.ds(pl.multiple_of(c * C, C), C)
        cols = [slice(h * LANES, (h + 1) * LANES) for h in range(HB)]
        loads = [(q_ref[rows, cs], f_ref[rows, cs], i_ref[rows, cs], g_ref[rows, cs]) for cs in cols]
        params = [(loglb_ref[:, cs], log1m_ref[:, cs], onem_ref[:, cs], ng_ref[:, cs]) for cs in cols]
        outs = _hgrn_chunk_group(loads, params, [st_ref[h] for h in range(HB)], C=C, valid_len=valid_len)
        for h in range(HB):
            st_ref[h] = outs[h][1]
            o_ref[rows, cols[h]] = outs[h][0].astype(o_ref.dtype)
        return carry

    if NC == 1:
        body(0, 0)
    else:
        lax.fori_loop(0, NC, body, 0)

    for h in range(HB):
        s_ref[0, h] = st_ref[h].T


def _hgrn_scan(p, nseq, L, C, HB, valid_len, lbp, ng, s0, out_dtype):
    H = ng.shape[1] // LANES
    nhb = H // HB
    W = HB * LANES
    NC = L // C
    loglb, log1m, onem = lbp
    has_s0 = s0 is not None

    def sect(s):
        return pl.BlockSpec((L, W), lambda b, hb, s=s: (b, s * nhb + hb))

    vec = pl.BlockSpec((1, W), lambda b, hb: (0, hb))
    in_specs = [sect(0), sect(1), sect(2), sect(3), vec, vec, vec, vec]
    args = [p, p, p, p, loglb, log1m, onem, ng]
    if has_s0:
        s0_all, s0_layer = s0
        in_specs.append(pl.BlockSpec((None, 1, HB, HG_DK, HG_DV), lambda b, hb: (s0_layer, b, hb, 0, 0)))
        args.append(s0_all)
    est = 2 * 4 * L * W * 4 + 2 * L * W * 4 + 6 * HB * HG_DK * HG_DV * 4 + (8 << 20)
    return pl.pallas_call(
        functools.partial(_hgrn_kernel, C=C, NC=NC, HB=HB, valid_len=valid_len, has_s0=has_s0),
        grid=(nseq, nhb),
        in_specs=in_specs,
        out_specs=[pl.BlockSpec((L, W), lambda b, hb: (b, hb)),
                   pl.BlockSpec((1, HB, HG_DK, HG_DV), lambda b, hb: (b, hb, 0, 0))],
        out_shape=[jax.ShapeDtypeStruct((nseq * L, H * LANES), out_dtype),
                   jax.ShapeDtypeStruct((nseq, H, HG_DK, HG_DV), F32)],
        scratch_shapes=[pltpu.VMEM((HB, HG_DV, HG_DK), F32)],
        compiler_params=pltpu.CompilerParams(
            dimension_semantics=("parallel", "parallel"),
            vmem_limit_bytes=_vmem_limit(est)),
        name="hgrn_scan",
    )(*args)


def _hgrn_fused_kernel(x_hbm, g_ref, wq_ref, wf_ref, wi_ref, wg_ref, loglb_ref, log1m_ref, onem_ref, ng_ref,
                       o_ref, s_ref, xn_ref, xc_ref, proj_ref, st_ref, in_sem, *, L, C, RB, NSEQ, HB):
    sg = pl.program_id(0)
    hb = pl.program_id(1)
    D = xn_ref.shape[1]
    WC = HB * LANES
    NB = L // RB
    CPB = RB // C
    w_refs = (wq_ref, wf_ref, wi_ref, wg_ref)
    assert CPB == len(w_refs)

    @pl.when(hb == 0)
    def _():
        g = g_ref[...]
        nchunk = NSEQ * NB

        def copy(c):
            return pltpu.make_async_copy(x_hbm.at[pl.ds(sg * (NSEQ * L) + c * RB, RB)],
                                         xc_ref.at[c % 2], in_sem.at[c % 2])
        copy(0).start()
        for c in range(nchunk):
            copy(c).wait()
            if c + 1 < nchunk:
                copy(c + 1).start()
            xn_ref[c * RB:(c + 1) * RB] = _rms(xc_ref[c % 2], g).astype(BF16)

    def project(rb, section, slot):
        rows = [xn_ref[pl.ds(pl.multiple_of(s * L + rb * RB, RB), RB)] for s in range(NSEQ)]
        y = _dot(jnp.concatenate(rows, axis=0), w_refs[section][...].astype(BF16))
        proj_ref[slot, :, section * WC:(section + 1) * WC] = y

    for ch in range(NSEQ * HB):
        st_ref[ch] = jnp.zeros((HG_DV, HG_DK), F32)
    for section in range(CPB):
        project(0, section, 0)

    params = [(loglb_ref[:, h * LANES:(h + 1) * LANES], log1m_ref[:, h * LANES:(h + 1) * LANES],
               onem_ref[:, h * LANES:(h + 1) * LANES], ng_ref[:, h * LANES:(h + 1) * LANES])
              for _ in range(NSEQ) for h in range(HB)]

    def block(rb, project_next):
        slot = rb % 2
        for ci in range(CPB):
            loads = []
            for s in range(NSEQ):
                rows = slice(s * RB + ci * C, s * RB + (ci + 1) * C)
                for h in range(HB):
                    loads.append(tuple(proj_ref[slot, rows, sec * WC + h * LANES:sec * WC + (h + 1) * LANES]
                                       for sec in range(4)))
            if project_next:
                project(rb + 1, ci, 1 - slot)
            outs = _hgrn_chunk_group(loads, params, [st_ref[ch] for ch in range(NSEQ * HB)], C=C, valid_len=C)
            orow = pl.ds(pl.multiple_of(rb * RB + ci * C, C), C)
            for s in range(NSEQ):
                for h in range(HB):
                    ch = s * HB + h
                    st_ref[ch] = outs[ch][1]
                    o_ref[s, orow, h * LANES:(h + 1) * LANES] = outs[ch][0].astype(o_ref.dtype)

    def body(rb, carry):
        block(rb, True)
        return carry

    lax.fori_loop(0, NB - 1, body, 0)
    block(NB - 1, False)

    for s in range(NSEQ):
        for h in range(HB):
            s_ref[s, h] = st_ref[s * HB + h].T


def _hgrn_fused(x, g, w, layer, B, L, lbp, ng):
    D = x.shape[1]
    H = ng.shape[1] // LANES
    HB, NSEQ, RB, C = FUSED_HB, FUSED_NSEQ, FUSED_RB, HG_CHUNK
    WC = HB * LANES
    nhb = H // HB
    assert B % NSEQ == 0 and H % HB == 0 and L % RB == 0 and RB // C == 4
    loglb, log1m, onem = lbp

    def sect(s):
        return pl.BlockSpec((None, D, WC), lambda sg, hb, s=s: (layer, 0, s * nhb + hb))

    vec = pl.BlockSpec((1, WC), lambda sg, hb: (0, hb))
    est = (NSEQ * L * D * 2 + 2 * RB * D * 4 + 2 * NSEQ * RB * 4 * WC * 4 + 2 * 4 * D * WC * 4
           + 2 * NSEQ * L * WC * 2 + (6 << 20))
    return pl.pallas_call(
        functools.partial(_hgrn_fused_kernel, L=L, C=C, RB=RB, NSEQ=NSEQ, HB=HB),
        grid=(B // NSEQ, nhb),
        in_specs=[pl.BlockSpec(memory_space=pl.ANY), pl.BlockSpec((1, D), lambda sg, hb: (0, 0)),
                  sect(0), sect(1), sect(2), sect(3), vec, vec, vec, vec],
        out_specs=[pl.BlockSpec((NSEQ, L, WC), lambda sg, hb: (sg, 0, hb)),
                   pl.BlockSpec((NSEQ, HB, HG_DK, HG_DV), lambda sg, hb: (sg, hb, 0, 0))],
        out_shape=[jax.ShapeDtypeStruct((B, L, H * LANES), BF16),
                   jax.ShapeDtypeStruct((B, H, HG_DK, HG_DV), F32)],
        scratch_shapes=[pltpu.VMEM((NSEQ * L, D), BF16), pltpu.VMEM((2, RB, D), F32),
                        pltpu.VMEM((2, NSEQ * RB, 4 * WC), F32), pltpu.VMEM((NSEQ * HB, HG_DV, HG_DK), F32),
                        pltpu.SemaphoreType.DMA((2,))],
        compiler_params=pltpu.CompilerParams(
            dimension_semantics=("arbitrary", "arbitrary"),
            vmem_limit_bytes=_vmem_limit(est)),
        name="hgrn_fused",
    )(x, g, w, w, w, w, loglb, log1m, onem, ng)


def _sink_softmax(scores, sink):
    m = sink
    for s in scores:
        m = jnp.maximum(m, jnp.max(s, axis=-1, keepdims=True))
    ps = [jnp.exp(s - m) for s in scores]
    denom = jnp.exp(sink - m)
    for p in ps:
        denom = denom + jnp.sum(p, axis=-1, keepdims=True)
    return [p / denom for p in ps]


def _attn_prompt_kernel(sinks_ref, q_ref, kc_ref, kp_ref, vc_ref, vp_ref, o_ref, *, n_kv, group):
    i = pl.program_id(1)
    W = WINDOW
    HD = HEAD_DIM
    j = lax.broadcasted_iota(jnp.int32, (2 * W, W), 0)
    r = lax.broadcasted_iota(jnp.int32, (2 * W, W), 1)
    mask = (j >= jnp.maximum(r, jnp.where(i > 0, 0, W))) & (j <= r + W)
    scale = HD ** -0.5
    zeros = jnp.zeros((HD, W), F32)

    s_all, vt_all = [], []
    for c in range(n_kv // 2):
        cs = slice(c * LANES, (c + 1) * LANES)
        ktile = jnp.concatenate([kp_ref[:, cs], kc_ref[:, cs]], axis=0).astype(BF16)
        vt_all.append(jnp.concatenate([vp_ref[:, cs], vc_ref[:, cs]], axis=0).T.astype(BF16))
        for par in range(2):
            kh = 2 * c + par
            rhs = []
            for pr in range(group // 2):
                c0 = (kh * group + 2 * pr) * HD
                pt = (q_ref[:, c0:c0 + LANES] * scale).T
                for half in range(2):
                    blk = pt[half * HD:(half + 1) * HD]
                    rhs.append(jnp.concatenate([blk, zeros] if par == 0 else [zeros, blk], axis=0))
            s_all.append(_dot(ktile, jnp.concatenate(rhs, axis=1).astype(BF16)))

    p_all, inv_all = [], []
    for kh in range(n_kv):
        pn, inv = [], []
        for hh in range(group):
            sink = sinks_ref[kh * group + hh]
            sm = jnp.where(mask, s_all[kh][:, hh * W:(hh + 1) * W], -jnp.inf)
            mx = jnp.maximum(jnp.max(sm, axis=0, keepdims=True), sink)
            p = jnp.exp(sm - mx)
            inv.append(1.0 / (jnp.sum(p, axis=0, keepdims=True) + jnp.exp(sink - mx)))
            pn.append(p.astype(BF16))
        p_all.append(jnp.concatenate(pn, axis=1))
        inv_all.append(jnp.concatenate(inv, axis=1))

    for kh in range(n_kv):
        par = kh % 2
        o_t = _dot(vt_all[kh // 2], p_all[kh])[par * HD:(par + 1) * HD] * inv_all[kh]
        for pr in range(group // 2):
            pair_t = jnp.concatenate([o_t[:, (2 * pr) * W:(2 * pr + 1) * W],
                                      o_t[:, (2 * pr + 1) * W:(2 * pr + 2) * W]], axis=0)
            c0 = (kh * group + 2 * pr) * HD
            o_ref[:, c0:c0 + LANES] = pair_t.T.astype(o_ref.dtype)


def _attn_prompt(p, sinks, B, L, n_q, n_kv):
    W = WINDOW
    nb = L // W
    QW = n_q * HEAD_DIM
    KW = n_kv * HEAD_DIM
    kblk = QW // KW
    vblk = (QW + KW) // KW
    est = 2 * (W * QW * 4 + 4 * W * KW * 4 + W * QW * 2) + (16 << 20)
    return pl.pallas_call(
        functools.partial(_attn_prompt_kernel, n_kv=n_kv, group=n_q // n_kv),
        grid=(B, nb),
        in_specs=[
            pl.BlockSpec(memory_space=pltpu.SMEM),
            pl.BlockSpec((W, QW), lambda b, i: (b * nb + i, 0)),
            pl.BlockSpec((W, KW), lambda b, i: (b * nb + i, kblk)),
            pl.BlockSpec((W, KW), lambda b, i: (b * nb + jnp.maximum(i - 1, 0), kblk)),
            pl.BlockSpec((W, KW), lambda b, i: (b * nb + i, vblk)),
            pl.BlockSpec((W, KW), lambda b, i: (b * nb + jnp.maximum(i - 1, 0), vblk)),
        ],
        out_specs=pl.BlockSpec((W, QW), lambda b, i: (b * nb + i, 0)),
        out_shape=jax.ShapeDtypeStruct((B * L, QW), BF16),
        compiler_params=pltpu.CompilerParams(
            dimension_semantics=("parallel", "parallel"),
            vmem_limit_bytes=_vmem_limit(est)),
        name="swa_prompt",
    )(sinks, p, p, p, p, p)


def _attn_sample_kernel(sinks_ref, qkv_ref, ck_ref, cv_ref, o_ref, *, n_q, n_kv, n_new):
    R = qkv_ref.shape[1]
    W = WINDOW
    HD = HEAD_DIM
    group = n_q // n_kv
    npair = group // 2
    QW = n_q * HD
    KW = n_kv * HD
    RR = npair * R
    scale = HD ** -0.5
    row = lax.broadcasted_iota(jnp.int32, (RR, 2 * W), 0)
    r = row & (R - 1)
    j = lax.broadcasted_iota(jnp.int32, (RR, 2 * W), 1)
    mask = (j >= r) & (j <= r + W) & (j < W + n_new)
    pair_of_row = lax.broadcasted_iota(jnp.int32, (RR, 1), 0) // R
    first = lax.broadcasted_iota(jnp.int32, (2 * W, LANES), 1) < HD
    zpad = jnp.zeros((W - R, LANES), F32)

    s_all, v_all = [], []
    for c in range(n_kv // 2):
        cs = slice(c * LANES, (c + 1) * LANES)
        kn = qkv_ref[0, :, QW + c * LANES:QW + (c + 1) * LANES]
        vn = qkv_ref[0, :, QW + KW + c * LANES:QW + KW + (c + 1) * LANES]
        ktile = jnp.concatenate([ck_ref[0, :, cs], kn, zpad], axis=0)
        vtile = jnp.concatenate([cv_ref[0, :, cs], vn, zpad], axis=0)
        kswap = pltpu.roll(ktile, HD, 1)
        vswap = pltpu.roll(vtile, HD, 1)
        for par in range(2):
            kh = 2 * c + par
            k_lo, k_hi = (ktile, kswap) if par == 0 else (kswap, ktile)
            v_lo, v_hi = (vtile, vswap) if par == 0 else (vswap, vtile)
            kcat = jnp.concatenate([jnp.where(first, k_lo, 0.0), jnp.where(first, 0.0, k_hi)], axis=0)
            vcat = jnp.concatenate([jnp.where(first, v_lo, 0.0), jnp.where(first, 0.0, v_hi)], axis=0)
            qs = jnp.concatenate(
                [qkv_ref[0, :, (kh * group + 2 * pr) * HD:(kh * group + 2 * pr) * HD + LANES]
                 for pr in range(npair)], axis=0) * scale
            s_all.append(_dot_nt(qs.astype(BF16), kcat.astype(BF16)))
            v_all.append(vcat.astype(BF16))

    p_all = []
    for kh in range(n_kv):
        halves = []
        for hf in range(2):
            sink = jnp.full((RR, 1), sinks_ref[kh * group + hf], F32)
            for pr in range(1, npair):
                sink = jnp.where(pair_of_row == pr, sinks_ref[kh * group + 2 * pr + hf], sink)
            sm = jnp.where(mask, s_all[kh][:, hf * 2 * W:(hf + 1) * 2 * W], -jnp.inf)
            (p,) = _sink_softmax([sm], sink)
            halves.append(p.astype(BF16))
        p_all.append(jnp.concatenate(halves, axis=1))

    for kh in range(n_kv):
        o = _dot(p_all[kh], v_all[kh])
        for pr in range(npair):
            c0 = (kh * group + 2 * pr) * HD
            o_ref[0, :, c0:c0 + LANES] = o[pr * R:(pr + 1) * R]


def _attn_sample(qkv, ck, cv, layer, sinks, n_q, n_kv, n_new):
    B, R, NW = qkv.shape
    QW = n_q * HEAD_DIM
    KW = n_kv * HEAD_DIM
    return pl.pallas_call(
        functools.partial(_attn_sample_kernel, n_q=n_q, n_kv=n_kv, n_new=n_new),
        grid=(B,),
        in_specs=[
            pl.BlockSpec(memory_space=pltpu.SMEM),
            pl.BlockSpec((1, R, NW), lambda b: (b, 0, 0)),
            pl.BlockSpec((None, 1, WINDOW, KW), lambda b: (layer, b, 0, 0)),
            pl.BlockSpec((None, 1, WINDOW, KW), lambda b: (layer, b, 0, 0)),
        ],
        out_specs=pl.BlockSpec((1, R, QW), lambda b: (b, 0, 0)),
        out_shape=jax.ShapeDtypeStruct((B, R, QW), F32),
        compiler_params=pltpu.CompilerParams(dimension_semantics=("parallel",)),
        name="swa_sample",
    )(sinks, qkv, ck, cv)


def _rope_tables(pos):
    half = HEAD_DIM // 2
    inv = jnp.power(ROPE_THETA, -jnp.arange(half, dtype=F32) * (2.0 / HEAD_DIM))
    ang = pos.astype(F32)[:, None] * inv[None, :]
    cos = jnp.cos(ang)
    sin = jnp.sin(ang)
    reps = LANES // HEAD_DIM
    cos_t = jnp.tile(jnp.concatenate([cos, cos], axis=1), (1, reps))
    sin_t = jnp.tile(jnp.concatenate([-sin, sin], axis=1), (1, reps))
    return (jnp.stack([cos_t, jnp.ones_like(cos_t)]), jnp.stack([sin_t, jnp.zeros_like(sin_t)]))


def kernel(x_prompt, x_sample, state_hgrn, cache_k_win, cache_v_win, norm_gains, w_ffn_in, w_ffn_out,
           w_hgrn_in, hgrn_lb_logits, hgrn_norm_gain, w_hgrn_out, w_attn_in, attn_sinks, w_attn_out):
    B, L, D = x_prompt.shape
    SB, SL, _ = x_sample.shape
    depth = norm_gains.shape[0]
    TP, TS = B * L, SB * SL
    H = hgrn_norm_gain.shape[1] // HG_DV
    n_kv = cache_k_win.shape[3]
    n_q = w_attn_out.shape[1] // HEAD_DIM
    QW, KW = n_q * HEAD_DIM, n_kv * HEAD_DIM
    buf = cache_k_win.shape[2]

    x = (x_prompt.reshape(TP, D), x_sample.reshape(TS, D))

    lb_all = jnp.cumsum(jax.nn.softmax(hgrn_lb_logits.astype(F32), axis=0), axis=0)
    lb_all = lb_all - lb_all[0:1]

    pos = jnp.concatenate([jnp.tile(jnp.arange(L, dtype=jnp.int32), B),
                           jnp.tile(PAST_LEN + jnp.arange(SL, dtype=jnp.int32), SB)])
    rope = _rope_tables(pos)

    hg_p, hg_s, kp_l, vp_l, ks_l, vs_l = [], [], [], [], [], []
    for layer in range(depth):
        g = norm_gains[layer]
        x = _ffn_half(x, g[0:1], g[1:2], w_ffn_in, w_ffn_out, layer, 0)
        if layer % N_MIXERS == 0:
            a = layer // N_MIXERS
            lb = lb_all[a][None, :]
            lbp = (jnp.log(lb), jnp.log1p(-lb), 1.0 - lb)
            ng = hgrn_norm_gain[a][None, :].astype(F32)
            o_p, s_p = _hgrn_fused(x, g[2:3], w_hgrn_in, a, B, L, lbp, ng)
            o_p = o_p.reshape(TP, -1)
            p_s = _proj(x[TP:], g[2:3], w_hgrn_in, a, (TS, HGRN_SAMPLE_PROJ_TN))
            ps = jnp.pad(p_s.reshape(SB, SL, -1), ((0, 0), (0, SAMPLE_PAD - SL), (0, 0)))
            o_s, s_s = _hgrn_scan(ps.reshape(SB * SAMPLE_PAD, -1), SB, SAMPLE_PAD, SAMPLE_PAD, H, SL,
                                  lbp, ng, (state_hgrn, a), F32)
            o_s = o_s.reshape(SB, SAMPLE_PAD, -1)[:, :SL].reshape(TS, -1).astype(BF16)
            x = _outproj(o_p, o_s, w_hgrn_out, x, g[3:4], a)
            hg_p.append(s_p.astype(state_hgrn.dtype))
            hg_s.append(s_s.astype(state_hgrn.dtype))
        else:
            bb = layer // N_MIXERS
            p = _proj(x, g[2:3], w_attn_in, bb, ATTN_PROJ_TILES, rope=rope, n_rope_cols=QW + KW)
            sinks = attn_sinks[bb].astype(F32)
            o_p = _attn_prompt(p, sinks, B, L, n_q, n_kv)
            pn = p[TP:].reshape(SB, SL, -1)
            ck = cache_k_win.reshape(-1, SB, buf, KW)
            cv = cache_v_win.reshape(-1, SB, buf, KW)
            o_s = _attn_sample(jnp.pad(pn, ((0, 0), (0, 8 - SL), (0, 0))), ck, cv, bb, sinks, n_q, n_kv, SL)
            x = _outproj(o_p, o_s[:, :SL].reshape(TS, QW).astype(BF16), w_attn_out, x, g[3:4], bb)
            tails = [p[(s + 1) * L - buf:(s + 1) * L, QW:] for s in range(B)]
            kp_l.append(jnp.stack([t[:, :KW] for t in tails]).reshape(B, buf, n_kv, HEAD_DIM))
            vp_l.append(jnp.stack([t[:, KW:] for t in tails]).reshape(B, buf, n_kv, HEAD_DIM))
            k_new = pn[:, :, QW:QW + KW].reshape(SB, SL, n_kv, HEAD_DIM)
            v_new = pn[:, :, QW + KW:].reshape(SB, SL, n_kv, HEAD_DIM)
            ks_l.append(jnp.concatenate([cache_k_win[bb], k_new], axis=1)[:, -buf:])
            vs_l.append(jnp.concatenate([cache_v_win[bb], v_new], axis=1)[:, -buf:])
        x = _ffn_half(x, g[4:5], g[5:6], w_ffn_in, w_ffn_out, layer, 1,
                      out_split=TP if layer == depth - 1 else None)

    yp = x[0].reshape(B, L, D)
    ys = x[1].reshape(SB, SL, D)
    return (yp, ys, jnp.stack(hg_p), jnp.stack(hg_s), jnp.stack(kp_l), jnp.stack(vp_l),
            jnp.stack(ks_l), jnp.stack(vs_l))
```

```python
import functools
import math

import numpy as np
import jax
import jax.numpy as jnp
from jax import lax
from jax.experimental import pallas as pl
from jax.experimental.pallas import tpu as pltpu

F32 = jnp.float32
BF16 = jnp.bfloat16

NORM_EPS = 1e-6
HEAD_DIM = 64
HG_DK = 128
HG_DV = 128
HG_CHUNK = 64
WINDOW = 128
PAST_LEN = 16384
ROPE_THETA = 10000.0
N_MIXERS = 2

LANES = 128
SUBLANES = 8
VMEM_BYTES_V7X = 64 * 1024 * 1024
VMEM_HEADROOM = 6 * 1024 * 1024

FFN_TM, FFN_TF = 1040, 256
FFN_RC = 208
HGRN_SAMPLE_PROJ_TN = 1024
FUSED_HB, FUSED_NSEQ, FUSED_RB = 2, 2, 256
ATTN_PROJ_TILES = (1040, 512)
ROPE_ROW_CHUNK = 208
OUT_TM = 512
SAMPLE_PAD = 16


def _vmem_limit(nbytes):
    return int(min(VMEM_BYTES_V7X - VMEM_HEADROOM, max(nbytes * 5 // 4, 16 * 1024 * 1024)))


def _rms(x, g):
    return x * lax.rsqrt(jnp.mean(x * x, axis=-1, keepdims=True) + NORM_EPS) * g


def _dot(a, b):
    return jnp.dot(a, b, preferred_element_type=F32)


def _dot_nt(a, b):
    return lax.dot_general(a, b, (((1,), (1,)), ((), ())), preferred_element_type=F32)


def _dot_tn(a, b):
    return lax.dot_general(a, b, (((0,), (0,)), ((), ())), preferred_element_type=F32)


def _ffn_kernel(*refs, nt, nj, tm, rc, in_split, out_split):
    refs = list(refs)
    x_refs = [refs.pop(0) for _ in range(1 if in_split is None else 2)]
    gpre_ref, gpost_ref, wg_ref, wu_ref, wo_ref = (refs.pop(0) for _ in range(5))
    y_refs = [refs.pop(0) for _ in range(1 if out_split is None else 2)]
    acc_ref, xn_ref, xc_ref, yc_ref, in_sem, out_sem = refs

    i = pl.program_id(0)
    j = pl.program_id(1)
    nch = tm // rc
    slot = i % 2
    pslot = 1 - slot
    k = j - 1
    gpre = gpre_ref[...]
    gpost = gpost_ref[...]

    def in_copy(row0, b):
        return pltpu.make_async_copy(x_refs[0].at[pl.ds(row0, rc)], xc_ref.at[b], in_sem.at[b])

    def out_copy(row0, b):
        return pltpu.make_async_copy(yc_ref.at[b], y_refs[0].at[pl.ds(row0, rc)], out_sem.at[b])

    def crossing_row(split):
        if split is None:
            return None
        assert split % rc != 0 and nt * tm - split < rc, "exactly one chunk must cross the split point"
        return (split // rc) * rc

    in_cross = crossing_row(in_split)
    out_cross = crossing_row(out_split)
    assert out_cross is None or out_cross >= (nt - 1) * tm, "only the last row tile may cross the output split"

    def in_copies_crossing(b):
        na = in_split - in_cross
        return [pltpu.make_async_copy(x_refs[0].at[pl.ds(in_cross, na)], xc_ref.at[b, pl.ds(0, na)], in_sem.at[b]),
                pltpu.make_async_copy(x_refs[1].at[pl.ds(0, rc - na)], xc_ref.at[b, pl.ds(na, rc - na)],
                                      in_sem.at[b])]

    def out_copies_crossing(b):
        na = out_split - out_cross
        return [pltpu.make_async_copy(yc_ref.at[b, pl.ds(0, na)], y_refs[0].at[pl.ds(out_cross, na)], out_sem.at[b]),
                pltpu.make_async_copy(yc_ref.at[b, pl.ds(na, rc - na)], y_refs[1].at[pl.ds(0, rc - na)],
                                      out_sem.at[b])]

    def static_in_copies(row0, b):
        return in_copies_crossing(b) if row0 == in_cross else [in_copy(row0, b)]

    def static_out_copies(row0, b):
        return out_copies_crossing(b) if row0 == out_cross else [out_copy(row0, b)]

    def task_row(t):
        return jnp.where(t < nch, (i - 1) * tm + t * rc, (i + 1) * tm + (t - nch) * rc)

    def task_live(t):
        return (t >= 0) & (t < 2 * nch) & jnp.where(t < nch, i >= 1, i <= nt - 2)

    def task_crosses(t):
        return False if in_cross is None else task_row(t) == in_cross

    @pl.when((i == 0) & (j == 0))
    def _():
        for c in range(nch):
            cps = static_in_copies(c * rc, c % 2)
            for cp in cps:
                cp.start()
            for cp in cps:
                cp.wait()
            xn_ref[0, c * rc:(c + 1) * rc] = _rms(xc_ref[c % 2], gpre).astype(BF16)

    @pl.when(task_live(k) & jnp.logical_not(task_crosses(k)))
    def _():
        in_copy(0, k & 1).wait()

    @pl.when(task_live(k) & (k >= 2) & (k < nch))
    def _():
        out_copy(0, k & 1).wait()

    @pl.when((j == nch + 1) & (i >= 1))
    def _():
        out_copy(0, 0).wait()
        out_copy(0, 1).wait()

    @pl.when(task_live(j) & jnp.logical_not(task_crosses(j)))
    def _():
        in_copy(task_row(j), j & 1).start()

    if in_cross is not None:
        @pl.when(task_live(k) & task_crosses(k))
        def _():
            for cp in in_copies_crossing(k & 1):
                cp.wait()

        @pl.when(task_live(j) & task_crosses(j))
        def _():
            for cp in in_copies_crossing(j & 1):
                cp.start()

    def accumulate(first):
        xn = xn_ref[slot]
        hg = _dot(xn, wg_ref[...].astype(BF16))
        hu = _dot(xn, wu_ref[...].astype(BF16))
        act = (hg * jax.nn.sigmoid(hg) * hu).astype(BF16)
        contrib = _dot(act, wo_ref[...].astype(BF16))
        if first:
            acc_ref[slot] = contrib
        else:
            acc_ref[slot] += contrib

    is_epi = (k >= 0) & (k < nch) & (i >= 1)
    is_pro = (k >= nch) & (k < 2 * nch) & (i <= nt - 2)

    @pl.when(j == 0)
    def _():
        accumulate(True)

    @pl.when(is_epi)
    def _():
        accumulate(False)
        b = k & 1
        rows = pl.ds(pl.multiple_of(k * rc, rc), rc)
        yc_ref[b] = xc_ref[b] + 0.5 * _rms(acc_ref[pslot, rows], gpost)
        out_copy((i - 1) * tm + k * rc, b).start()

    @pl.when(is_pro)
    def _():
        accumulate(False)
        rows = pl.ds(pl.multiple_of((k - nch) * rc, rc), rc)
        xn_ref[pslot, rows] = _rms(xc_ref[k & 1], gpre).astype(BF16)

    @pl.when((j >= 1) & jnp.logical_not(is_epi) & jnp.logical_not(is_pro))
    def _():
        accumulate(False)

    @pl.when((i == nt - 1) & (j == nj - 1))
    def _():
        last = (nt - 1) % 2
        for c in range(nch):
            b = c % 2
            row0 = (nt - 1) * tm + c * rc
            cps = static_in_copies(row0, b)
            for cp in cps:
                cp.start()
            for cp in cps:
                cp.wait()
            yc_ref[b] = xc_ref[b] + 0.5 * _rms(acc_ref[last, c * rc:(c + 1) * rc], gpost)
            ocs = static_out_copies(row0, b)
            for oc in ocs:
                oc.start()
            for oc in ocs:
                oc.wait()


def _ffn_half(x, g_pre, g_post, w_in, w_out, layer, half, out_split=None):
    xs = x if isinstance(x, (tuple, list)) else (x,)
    in_split = xs[0].shape[0] if len(xs) == 2 else None
    T = sum(a.shape[0] for a in xs)
    D = xs[0].shape[1]
    F = w_out.shape[2]
    TM, TF, RC = FFN_TM, FFN_TF, FFN_RC
    nj = F // TF
    nt = T // TM
    assert T % TM == 0 and F % TF == 0 and TM % RC == 0 and TM // RC >= 2 and nj >= 2 * (TM // RC) + 2
    est = (2 * TM * D * (4 + 2) + 4 * RC * D * 4 + 2 * 3 * D * TF * 4
           + 3 * D * TF * 2 + 4 * TM * TF * 4 + TM * D * 4)
    any_spec = pl.BlockSpec(memory_space=pl.ANY)
    if out_split is None:
        out_specs, out_shape = any_spec, jax.ShapeDtypeStruct((T, D), F32)
    else:
        out_specs = [any_spec, any_spec]
        out_shape = [jax.ShapeDtypeStruct((out_split, D), F32), jax.ShapeDtypeStruct((T - out_split, D), F32)]
    return pl.pallas_call(
        functools.partial(_ffn_kernel, nt=nt, nj=nj, tm=TM, rc=RC, in_split=in_split, out_split=out_split),
        grid=(nt, nj),
        in_specs=[any_spec] * len(xs) + [
            pl.BlockSpec((1, D), lambda i, j: (0, 0)),
            pl.BlockSpec((1, D), lambda i, j: (0, 0)),
            pl.BlockSpec((None, None, D, TF), lambda i, j: (layer, half, 0, j)),
            pl.BlockSpec((None, None, D, TF), lambda i, j: (layer, half, 0, nj + j)),
            pl.BlockSpec((None, None, TF, D), lambda i, j: (layer, half, j, 0)),
        ],
        out_specs=out_specs,
        out_shape=out_shape,
        scratch_shapes=[pltpu.VMEM((2, TM, D), F32), pltpu.VMEM((2, TM, D), BF16),
                        pltpu.VMEM((2, RC, D), F32), pltpu.VMEM((2, RC, D), F32),
                        pltpu.SemaphoreType.DMA((2,)), pltpu.SemaphoreType.DMA((2,))],
        compiler_params=pltpu.CompilerParams(
            dimension_semantics=("arbitrary", "arbitrary"),
            vmem_limit_bytes=_vmem_limit(est)),
        name="ffn_half",
    )(*xs, g_pre, g_post, w_in, w_in, w_out)


def _rope_tile(y, cos, sin_signed):
    lane = lax.broadcasted_iota(jnp.int32, y.shape, 1)
    first_half = (lane & (HEAD_DIM - 1)) < (HEAD_DIM // 2)
    rot = jnp.where(first_half,
                    pltpu.roll(y, LANES - HEAD_DIM // 2, 1),
                    pltpu.roll(y, HEAD_DIM // 2, 1))
    return y * cos + rot * sin_signed


def _proj_kernel(x_ref, g_ref, w_ref, o_ref, xn_ref):
    @pl.when(pl.program_id(1) == 0)
    def _():
        xn_ref[...] = _rms(x_ref[...], g_ref[...]).astype(BF16)

    o_ref[...] = _dot(xn_ref[...], w_ref[...].astype(BF16))


def _proj_rope_kernel(x_ref, g_ref, w_ref, cos_ref, sin_ref, o_ref, xn_ref, *, row_chunk):
    @pl.when(pl.program_id(1) == 0)
    def _():
        xn_ref[...] = _rms(x_ref[...], g_ref[...]).astype(BF16)

    wb = w_ref[...].astype(BF16)
    for r0 in range(0, xn_ref.shape[0], row_chunk):
        rows = slice(r0, r0 + row_chunk)
        y = _dot(xn_ref[rows], wb)
        cos = cos_ref[rows]
        sin = sin_ref[rows]
        for c in range(y.shape[1] // LANES):
            cols = slice(c * LANES, (c + 1) * LANES)
            o_ref[rows, cols] = _rope_tile(y[:, cols], cos, sin)


def _proj(x, g, w, layer, tiles, rope=None, n_rope_cols=0):
    T, D = x.shape
    N = w.shape[2]
    TM, TN = tiles
    assert T % TM == 0 and N % TN == 0 and n_rope_cols % TN == 0
    est = 2 * TM * D * 4 + TM * D * 2 + 2 * D * TN * 4 + D * TN * 2 + 4 * TM * TN * 4
    in_specs = [
        pl.BlockSpec((TM, D), lambda i, j: (i, 0)),
        pl.BlockSpec((1, D), lambda i, j: (0, 0)),
        pl.BlockSpec((None, D, TN), lambda i, j: (layer, 0, j)),
    ]
    args = [x, g, w]
    if rope is None:
        body = _proj_kernel
    else:
        assert TM % ROPE_ROW_CHUNK == 0
        body = functools.partial(_proj_rope_kernel, row_chunk=ROPE_ROW_CHUNK)
        n_rope_blocks = n_rope_cols // TN
        table = pl.BlockSpec((None, TM, LANES), lambda i, j: (jnp.where(j < n_rope_blocks, 0, 1), i, 0))
        in_specs += [table, table]
        args += list(rope)
    return pl.pallas_call(
        body,
        grid=(T // TM, N // TN),
        in_specs=in_specs,
        out_specs=pl.BlockSpec((TM, TN), lambda i, j: (i, j)),
        out_shape=jax.ShapeDtypeStruct((T, N), F32),
        scratch_shapes=[pltpu.VMEM((TM, D), BF16)],
        compiler_params=pltpu.CompilerParams(
            dimension_semantics=("parallel", "arbitrary"),
            vmem_limit_bytes=_vmem_limit(est)),
        name="mixer_in_proj",
    )(*args)


def _outproj_kernel(op_ref, os_ref, w_ref, x_ref, g_ref, y_ref, wbf_ref, *, n_prompt_tiles, ts):
    i = pl.program_id(0)

    @pl.when(i == 0)
    def _():
        wbf_ref[...] = w_ref[...].astype(BF16)

    @pl.when(i < n_prompt_tiles)
    def _():
        y_ref[...] = x_ref[...] + _rms(_dot(op_ref[...], wbf_ref[...]), g_ref[...])

    @pl.when(i == n_prompt_tiles)
    def _():
        y_ref[0:ts] = x_ref[0:ts] + _rms(_dot(os_ref[...], wbf_ref[...]), g_ref[...])


def _outproj(o_prompt, o_sample, w, x, g, layer):
    T, D = x.shape
    TP, K = o_prompt.shape
    TS = o_sample.shape[0]
    TM = OUT_TM
    assert TP % TM == 0 and TS <= TM and TS % SUBLANES == 0 and TP + TS == T
    npt = TP // TM
    est = K * D * (4 + 2) + 2 * TM * K * 2 + 2 * TS * K * 2 + 5 * TM * D * 4
    return pl.pallas_call(
        functools.partial(_outproj_kernel, n_prompt_tiles=npt, ts=TS),
        grid=(npt + 1,),
        in_specs=[
            pl.BlockSpec((TM, K), lambda i: (jnp.minimum(i, npt - 1), 0)),
            pl.BlockSpec((TS, K), lambda i: (0, 0)),
            pl.BlockSpec((None, K, D), lambda i: (layer, 0, 0), pipeline_mode=pl.Buffered(1)),
            pl.BlockSpec((TM, D), lambda i: (i, 0)),
            pl.BlockSpec((1, D), lambda i: (0, 0)),
        ],
        out_specs=pl.BlockSpec((TM, D), lambda i: (i, 0)),
        out_shape=jax.ShapeDtypeStruct((T, D), F32),
        scratch_shapes=[pltpu.VMEM((K, D), BF16)],
        compiler_params=pltpu.CompilerParams(
            dimension_semantics=("arbitrary",),
            vmem_limit_bytes=_vmem_limit(est)),
        name="mixer_out_proj",
    )(o_prompt, o_sample, w, x, g)


def _tiles(x):
    return [x[SUBLANES * j:SUBLANES * (j + 1)] for j in range(x.shape[0] // SUBLANES)]


def _bcast_row(t, i):
    return jnp.broadcast_to(t[i:i + 1], t.shape)


def _tile_prefix(t, sub):
    p = t + jnp.where(sub >= 1, pltpu.roll(t, 1, 0), 0.0)
    p = p + jnp.where(sub >= 2, pltpu.roll(p, 2, 0), 0.0)
    return p + jnp.where(sub >= 4, pltpu.roll(p, 4, 0), 0.0)


def _hgrn_gates(fz, loglb, log1m, onem, valid):
    e = jnp.exp(-jnp.abs(fz))
    lsig = jnp.minimum(fz, 0.0) - jnp.log(1.0 + e)
    ct = log1m + lsig
    g = jnp.maximum(loglb, ct) + jnp.log(1.0 + jnp.exp(-jnp.abs(loglb - ct)))
    r = 1.0 / (1.0 + e)
    k = onem * jnp.where(fz > 0.0, e * r, r)
    if valid is not None:
        g = jnp.where(valid, g, 0.0)
        k = jnp.where(valid, k, 0.0)
    return g, k


def _hgrn_level_operands(m, qt, kt, gt, bt, sub):
    nv = len(qt)
    zero = jnp.zeros_like(qt[0])
    qm, km = [], []
    if m >= SUBLANES:
        mv = m // SUBLANES
        for j in range(nv):
            jr = (j // (2 * mv)) * (2 * mv) + mv - 1
            ref = _bcast_row(bt[jr], SUBLANES - 1)
            if j % (2 * mv) >= mv:
                qm.append(qt[j] * jnp.exp(bt[j] - ref))
                km.append(zero)
            else:
                qm.append(zero)
                km.append(kt[j] * jnp.exp(ref - bt[j]))
        return qm, km
    upper = (sub & m) != 0
    for j in range(nv):
        if m == 1:
            e = jnp.exp(gt[j])
            qm.append(jnp.where(upper, qt[j] * e, 0.0))
            km.append(jnp.where(upper, 0.0, kt[j]))
            continue
        if m == 4:
            ref = _bcast_row(bt[j], 3)
        else:
            ref = jnp.where(sub < 4, _bcast_row(bt[j], 1), _bcast_row(bt[j], 5))
        e = jnp.exp(-jnp.abs(bt[j] - ref))
        qm.append(jnp.where(upper, qt[j] * e, 0.0))
        km.append(jnp.where(upper, 0.0, kt[j] * e))
    return qm, km


def _hgrn_chunk_group(loads, params, sts, *, C, valid_len):
    nh = len(loads)
    sub = lax.broadcasted_iota(jnp.int32, (SUBLANES, LANES), 0)
    valid = None
    if valid_len < C:
        valid = lax.broadcasted_iota(jnp.int32, (C, LANES), 0) < valid_len
    ti = lax.broadcasted_iota(jnp.int32, (C, C), 0)
    si = lax.broadcasted_iota(jnp.int32, (C, C), 1)

    gk = [_hgrn_gates(loads[h][1], params[h][0], params[h][1], params[h][2], valid) for h in range(nh)]

    gts, bts = [], []
    for h in range(nh):
        gt = _tiles(gk[h][0])
        bt = []
        for j, t in enumerate(gt):
            p = _tile_prefix(t, sub)
            bt.append(p if j == 0 else p + _bcast_row(bt[j - 1], SUBLANES - 1))
        gts.append(gt)
        bts.append(bt)

    scores = []
    for h in range(nh):
        q, k = loads[h][0], gk[h][1]
        qt, kt = _tiles(q), _tiles(k)
        a = jnp.where(ti == si, _dot_nt(q.astype(BF16), k.astype(BF16)), 0.0)
        m = C // 2
        while m >= 1:
            qm, km = _hgrn_level_operands(m, qt, kt, gts[h], bts[h], sub)
            am = _dot_nt(jnp.concatenate(qm, axis=0).astype(BF16), jnp.concatenate(km, axis=0).astype(BF16))
            if 2 * m < C:
                am = jnp.where((ti & (-2 * m)) == (si & (-2 * m)), am, 0.0)
            a = a + am
            m //= 2
        scores.append(a)

    outs = []
    for h in range(nh):
        q, _, v, gz = loads[h]
        k = gk[h][1]
        ng = params[h][3]
        st = sts[h]
        b = jnp.concatenate(bts[h], axis=0)
        o = _dot(scores[h].astype(BF16), v.astype(BF16)) + _dot_nt((q * jnp.exp(b)).astype(BF16), st.astype(BF16))
        bl = b[C - 1:C]
        kb = (k * jnp.exp(bl - b)).astype(BF16)
        st_new = st * jnp.exp(bl) + _dot_tn(v.astype(BF16), kb)
        on = o * lax.rsqrt(jnp.mean(o * o, axis=-1, keepdims=True) + NORM_EPS) * ng
        outs.append((on * (gz * jax.nn.sigmoid(gz)), st_new))
    return outs


def _hgrn_kernel(*refs, C, NC, HB, valid_len, has_s0, n_prev):
    refs = list(refs)
    q_ref, f_ref, i_ref, g_ref, loglb_ref, log1m_ref, onem_ref, ng_ref = (refs.pop(0) for _ in range(8))
    s0_ref = refs.pop(0) if has_s0 else None
    prev_ref = refs.pop(0) if n_prev else None
    o_ref, s_ref, st_ref = refs

    for h in range(HB):
        if has_s0:
            st_ref[h] = s0_ref[0, h].T
        else:
            st_ref[h] = jnp.zeros((HG_DV, HG_DK), F32)

    def body(c, carry):
        rows = pl.ds(pl.multiple_of(c * C, C), C)
        cols = [slice(h * LANES, (h + 1) * LANES) for h in range(HB)]
        loads = [(q_ref[rows, cs], f_ref[rows, cs], i_ref[rows, cs], g_ref[rows, cs]) for cs in cols]
        params = [(loglb_ref[:, cs], log1m_ref[:, cs], onem_ref[:, cs], ng_ref[:, cs]) for cs in cols]
        outs = _hgrn_chunk_group(loads, params, [st_ref[h] for h in range(HB)], C=C, valid_len=valid_len)
        for h in range(HB):
            st_ref[h] = outs[h][1]
            o_ref[rows, cols[h]] = outs[h][0].astype(o_ref.dtype)
        return carry

    if NC == 1:
        body(0, 0)
    else:
        lax.fori_loop(0, NC, body, 0)

    for layer in range(n_prev):
        s_ref[layer] = prev_ref[layer]
    for h in range(HB):
        s_ref[n_prev, 0, h] = st_ref[h].T


def _hgrn_scan(p, nseq, L, C, HB, valid_len, lbp, ng, s0, out_dtype, prev=None):
    H = ng.shape[1] // LANES
    nhb = H // HB
    W = HB * LANES
    NC = L // C
    loglb, log1m, onem = lbp
    has_s0 = s0 is not None

    def sect(s):
        return pl.BlockSpec((L, W), lambda b, hb, s=s: (b, s * nhb + hb))

    vec = pl.BlockSpec((1, W), lambda b, hb: (0, hb))
    in_specs = [sect(0), sect(1), sect(2), sect(3), vec, vec, vec, vec]
    args = [p, p, p, p, loglb, log1m, onem, ng]
    if has_s0:
        s0_all, s0_layer = s0
        in_specs.append(pl.BlockSpec((None, 1, HB, HG_DK, HG_DV), lambda b, hb: (s0_layer, b, hb, 0, 0)))
        args.append(s0_all)
    n_prev = 0 if prev is None else prev.shape[0]
    if n_prev:
        in_specs.append(pl.BlockSpec((n_prev, 1, HB, HG_DK, HG_DV), lambda b, hb: (0, b, hb, 0, 0)))
        args.append(prev)
    est = (2 * 4 * L * W * 4 + 2 * L * W * 4 + (6 + 4 * (2 * n_prev + 1)) * HB * HG_DK * HG_DV * 4
           + (8 << 20))
    return pl.pallas_call(
        functools.partial(_hgrn_kernel, C=C, NC=NC, HB=HB, valid_len=valid_len, has_s0=has_s0, n_prev=n_prev),
        grid=(nseq, nhb),
        in_specs=in_specs,
        out_specs=[pl.BlockSpec((L, W), lambda b, hb: (b, hb)),
                   pl.BlockSpec((n_prev + 1, 1, HB, HG_DK, HG_DV), lambda b, hb: (0, b, hb, 0, 0))],
        out_shape=[jax.ShapeDtypeStruct((nseq * L, H * LANES), out_dtype),
                   jax.ShapeDtypeStruct((n_prev + 1, nseq, H, HG_DK, HG_DV), F32)],
        scratch_shapes=[pltpu.VMEM((HB, HG_DV, HG_DK), F32)],
        compiler_params=pltpu.CompilerParams(
            dimension_semantics=("parallel", "parallel"),
            vmem_limit_bytes=_vmem_limit(est)),
        name="hgrn_scan",
    )(*args)


def _hgrn_fused_kernel(x_hbm, g_ref, wq_ref, wf_ref, wi_ref, wg_ref, loglb_ref, log1m_ref, onem_ref, ng_ref,
                       o_ref, s_ref, xn_ref, xc_ref, proj_ref, st_ref, in_sem, *, L, C, RB, NSEQ, HB):
    sg = pl.program_id(0)
    hb = pl.program_id(1)
    D = xn_ref.shape[1]
    WC = HB * LANES
    NB = L // RB
    CPB = RB // C
    w_refs = (wq_ref, wf_ref, wi_ref, wg_ref)
    assert CPB == len(w_refs)

    @pl.when(hb == 0)
    def _():
        g = g_ref[...]
        nchunk = NSEQ * NB

        def copy(c):
            return pltpu.make_async_copy(x_hbm.at[pl.ds(sg * (NSEQ * L) + c * RB, RB)],
                                         xc_ref.at[c % 2], in_sem.at[c % 2])
        copy(0).start()
        for c in range(nchunk):
            copy(c).wait()
            if c + 1 < nchunk:
                copy(c + 1).start()
            xn_ref[c * RB:(c + 1) * RB] = _rms(xc_ref[c % 2], g).astype(BF16)

    def project(rb, section, slot):
        rows = [xn_ref[pl.ds(pl.multiple_of(s * L + rb * RB, RB), RB)] for s in range(NSEQ)]
        y = _dot(jnp.concatenate(rows, axis=0), w_refs[section][...].astype(BF16))
        proj_ref[slot, :, section * WC:(section + 1) * WC] = y

    for ch in range(NSEQ * HB):
        st_ref[ch] = jnp.zeros((HG_DV, HG_DK), F32)
    for section in range(CPB):
        project(0, section, 0)

    params = [(loglb_ref[:, h * LANES:(h + 1) * LANES], log1m_ref[:, h * LANES:(h + 1) * LANES],
               onem_ref[:, h * LANES:(h + 1) * LANES], ng_ref[:, h * LANES:(h + 1) * LANES])
              for _ in range(NSEQ) for h in range(HB)]

    def block(rb, project_next):
        slot = rb % 2
        for ci in range(CPB):
            loads = []
            for s in range(NSEQ):
                rows = slice(s * RB + ci * C, s * RB + (ci + 1) * C)
                for h in range(HB):
                    loads.append(tuple(proj_ref[slot, rows, sec * WC + h * LANES:sec * WC + (h + 1) * LANES]
                                       for sec in range(4)))
            if project_next:
                project(rb + 1, ci, 1 - slot)
            outs = _hgrn_chunk_group(loads, params, [st_ref[ch] for ch in range(NSEQ * HB)], C=C, valid_len=C)
            orow = pl.ds(pl.multiple_of(rb * RB + ci * C, C), C)
            for s in range(NSEQ):
                for h in range(HB):
                    ch = s * HB + h
                    st_ref[ch] = outs[ch][1]
                    o_ref[s, orow, h * LANES:(h + 1) * LANES] = outs[ch][0].astype(o_ref.dtype)

    def body(rb, carry):
        block(rb, True)
        return carry

    lax.fori_loop(0, NB - 1, body, 0)
    block(NB - 1, False)

    for s in range(NSEQ):
        for h in range(HB):
            s_ref[s, h] = st_ref[s * HB + h].T


def _hgrn_fused(x, g, w, layer, B, L, lbp, ng):
    D = x.shape[1]
    H = ng.shape[1] // LANES
    HB, NSEQ, RB, C = FUSED_HB, FUSED_NSEQ, FUSED_RB, HG_CHUNK
    WC = HB * LANES
    nhb = H // HB
    assert B % NSEQ == 0 and H % HB == 0 and L % RB == 0 and RB // C == 4
    loglb, log1m, onem = lbp

    def sect(s):
        return pl.BlockSpec((None, D, WC), lambda sg, hb, s=s: (layer, 0, s * nhb + hb))

    vec = pl.BlockSpec((1, WC), lambda sg, hb: (0, hb))
    est = (NSEQ * L * D * 2 + 2 * RB * D * 4 + 2 * NSEQ * RB * 4 * WC * 4 + 2 * 4 * D * WC * 4
           + 2 * NSEQ * L * WC * 2 + (6 << 20))
    return pl.pallas_call(
        functools.partial(_hgrn_fused_kernel, L=L, C=C, RB=RB, NSEQ=NSEQ, HB=HB),
        grid=(B // NSEQ, nhb),
        in_specs=[pl.BlockSpec(memory_space=pl.ANY), pl.BlockSpec((1, D), lambda sg, hb: (0, 0)),
                  sect(0), sect(1), sect(2), sect(3), vec, vec, vec, vec],
        out_specs=[pl.BlockSpec((NSEQ, L, WC), lambda sg, hb: (sg, 0, hb)),
                   pl.BlockSpec((NSEQ, HB, HG_DK, HG_DV), lambda sg, hb: (sg, hb, 0, 0))],
        out_shape=[jax.ShapeDtypeStruct((B, L, H * LANES), BF16),
                   jax.ShapeDtypeStruct((B, H, HG_DK, HG_DV), F32)],
        scratch_shapes=[pltpu.VMEM((NSEQ * L, D), BF16), pltpu.VMEM((2, RB, D), F32),
                        pltpu.VMEM((2, NSEQ * RB, 4 * WC), F32), pltpu.VMEM((NSEQ * HB, HG_DV, HG_DK), F32),
                        pltpu.SemaphoreType.DMA((2,))],
        compiler_params=pltpu.CompilerParams(
            dimension_semantics=("arbitrary", "arbitrary"),
            vmem_limit_bytes=_vmem_limit(est)),
        name="hgrn_fused",
    )(x, g, w, w, w, w, loglb, log1m, onem, ng)


def _sink_softmax(scores, sink):
    m = sink
    for s in scores:
        m = jnp.maximum(m, jnp.max(s, axis=-1, keepdims=True))
    ps = [jnp.exp(s - m) for s in scores]
    denom = jnp.exp(sink - m)
    for p in ps:
        denom = denom + jnp.sum(p, axis=-1, keepdims=True)
    return [p / denom for p in ps]


def _attn_prompt_kernel(sinks_ref, q_ref, kc_ref, kp_ref, vc_ref, vp_ref, o_ref, *, n_kv, group):
    i = pl.program_id(1)
    W = WINDOW
    HD = HEAD_DIM
    j = lax.broadcasted_iota(jnp.int32, (2 * W, W), 0)
    r = lax.broadcasted_iota(jnp.int32, (2 * W, W), 1)
    mask = (j >= jnp.maximum(r, jnp.where(i > 0, 0, W))) & (j <= r + W)
    scale = HD ** -0.5
    zeros = jnp.zeros((HD, W), F32)

    s_all, vt_all = [], []
    for c in range(n_kv // 2):
        cs = slice(c * LANES, (c + 1) * LANES)
        ktile = jnp.concatenate([kp_ref[:, cs], kc_ref[:, cs]], axis=0).astype(BF16)
        vt_all.append(jnp.concatenate([vp_ref[:, cs], vc_ref[:, cs]], axis=0).T.astype(BF16))
        for par in range(2):
            kh = 2 * c + par
            rhs = []
            for pr in range(group // 2):
                c0 = (kh * group + 2 * pr) * HD
                pt = (q_ref[:, c0:c0 + LANES] * scale).T
                for half in range(2):
                    blk = pt[half * HD:(half + 1) * HD]
                    rhs.append(jnp.concatenate([blk, zeros] if par == 0 else [zeros, blk], axis=0))
            s_all.append(_dot(ktile, jnp.concatenate(rhs, axis=1).astype(BF16)))

    p_all, inv_all = [], []
    for kh in range(n_kv):
        pn, inv = [], []
        for hh in range(group):
            sink = sinks_ref[kh * group + hh]
            sm = jnp.where(mask, s_all[kh][:, hh * W:(hh + 1) * W], -jnp.inf)
            mx = jnp.maximum(jnp.max(sm, axis=0, keepdims=True), sink)
            p = jnp.exp(sm - mx)
            inv.append(1.0 / (jnp.sum(p, axis=0, keepdims=True) + jnp.exp(sink - mx)))
            pn.append(p.astype(BF16))
        p_all.append(jnp.concatenate(pn, axis=1))
        inv_all.append(jnp.concatenate(inv, axis=1))

    for kh in range(n_kv):
        par = kh % 2
        o_t = _dot(vt_all[kh // 2], p_all[kh])[par * HD:(par + 1) * HD] * inv_all[kh]
        for pr in range(group // 2):
            pair_t = jnp.concatenate([o_t[:, (2 * pr) * W:(2 * pr + 1) * W],
                                      o_t[:, (2 * pr + 1) * W:(2 * pr + 2) * W]], axis=0)
            c0 = (kh * group + 2 * pr) * HD
            o_ref[:, c0:c0 + LANES] = pair_t.T.astype(o_ref.dtype)


def _attn_prompt(p, sinks, B, L, n_q, n_kv):
    W = WINDOW
    nb = L // W
    QW = n_q * HEAD_DIM
    KW = n_kv * HEAD_DIM
    kblk = QW // KW
    vblk = (QW + KW) // KW
    est = 2 * (W * QW * 4 + 4 * W * KW * 4 + W * QW * 2) + (16 << 20)
    return pl.pallas_call(
        functools.partial(_attn_prompt_kernel, n_kv=n_kv, group=n_q // n_kv),
        grid=(B, nb),
        in_specs=[
            pl.BlockSpec(memory_space=pltpu.SMEM),
            pl.BlockSpec((W, QW), lambda b, i: (b * nb + i, 0)),
            pl.BlockSpec((W, KW), lambda b, i: (b * nb + i, kblk)),
            pl.BlockSpec((W, KW), lambda b, i: (b * nb + jnp.maximum(i - 1, 0), kblk)),
            pl.BlockSpec((W, KW), lambda b, i: (b * nb + i, vblk)),
            pl.BlockSpec((W, KW), lambda b, i: (b * nb + jnp.maximum(i - 1, 0), vblk)),
        ],
        out_specs=pl.BlockSpec((W, QW), lambda b, i: (b * nb + i, 0)),
        out_shape=jax.ShapeDtypeStruct((B * L, QW), BF16),
        compiler_params=pltpu.CompilerParams(
            dimension_semantics=("parallel", "parallel"),
            vmem_limit_bytes=_vmem_limit(est)),
        name="swa_prompt",
    )(sinks, p, p, p, p, p)


def _attn_sample_kernel(sinks_ref, qkv_ref, ck_ref, cv_ref, o_ref, *, n_q, n_kv, n_new):
    R = qkv_ref.shape[1]
    W = WINDOW
    HD = HEAD_DIM
    group = n_q // n_kv
    npair = group // 2
    QW = n_q * HD
    KW = n_kv * HD
    RR = npair * R
    scale = HD ** -0.5
    row = lax.broadcasted_iota(jnp.int32, (RR, 2 * W), 0)
    r = row & (R - 1)
    j = lax.broadcasted_iota(jnp.int32, (RR, 2 * W), 1)
    mask = (j >= r) & (j <= r + W) & (j < W + n_new)
    pair_of_row = lax.broadcasted_iota(jnp.int32, (RR, 1), 0) // R
    first = lax.broadcasted_iota(jnp.int32, (2 * W, LANES), 1) < HD
    zpad = jnp.zeros((W - R, LANES), F32)

    s_all, v_all = [], []
    for c in range(n_kv // 2):
        cs = slice(c * LANES, (c + 1) * LANES)
        kn = qkv_ref[0, :, QW + c * LANES:QW + (c + 1) * LANES]
        vn = qkv_ref[0, :, QW + KW + c * LANES:QW + KW + (c + 1) * LANES]
        ktile = jnp.concatenate([ck_ref[0, :, cs], kn, zpad], axis=0)
        vtile = jnp.concatenate([cv_ref[0, :, cs], vn, zpad], axis=0)
        kswap = pltpu.roll(ktile, HD, 1)
        vswap = pltpu.roll(vtile, HD, 1)
        for par in range(2):
            kh = 2 * c + par
            k_lo, k_hi = (ktile, kswap) if par == 0 else (kswap, ktile)
            v_lo, v_hi = (vtile, vswap) if par == 0 else (vswap, vtile)
            kcat = jnp.concatenate([jnp.where(first, k_lo, 0.0), jnp.where(first, 0.0, k_hi)], axis=0)
            vcat = jnp.concatenate([jnp.where(first, v_lo, 0.0), jnp.where(first, 0.0, v_hi)], axis=0)
            qs = jnp.concatenate(
                [qkv_ref[0, :, (kh * group + 2 * pr) * HD:(kh * group + 2 * pr) * HD + LANES]
                 for pr in range(npair)], axis=0) * scale
            s_all.append(_dot_nt(qs.astype(BF16), kcat.astype(BF16)))
            v_all.append(vcat.astype(BF16))

    p_all = []
    for kh in range(n_kv):
        halves = []
        for hf in range(2):
            sink = jnp.full((RR, 1), sinks_ref[kh * group + hf], F32)
            for pr in range(1, npair):
                sink = jnp.where(pair_of_row == pr, sinks_ref[kh * group + 2 * pr + hf], sink)
            sm = jnp.where(mask, s_all[kh][:, hf * 2 * W:(hf + 1) * 2 * W], -jnp.inf)
            (p,) = _sink_softmax([sm], sink)
            halves.append(p.astype(BF16))
        p_all.append(jnp.concatenate(halves, axis=1))

    for kh in range(n_kv):
        o = _dot(p_all[kh], v_all[kh])
        for pr in range(npair):
            c0 = (kh * group + 2 * pr) * HD
            o_ref[0, :, c0:c0 + LANES] = o[pr * R:(pr + 1) * R]


def _attn_sample(qkv, ck, cv, layer, sinks, n_q, n_kv, n_new):
    B, R, NW = qkv.shape
    QW = n_q * HEAD_DIM
    KW = n_kv * HEAD_DIM
    return pl.pallas_call(
        functools.partial(_attn_sample_kernel, n_q=n_q, n_kv=n_kv, n_new=n_new),
        grid=(B,),
        in_specs=[
            pl.BlockSpec(memory_space=pltpu.SMEM),
            pl.BlockSpec((1, R, NW), lambda b: (b, 0, 0)),
            pl.BlockSpec((None, 1, WINDOW, KW), lambda b: (layer, b, 0, 0)),
            pl.BlockSpec((None, 1, WINDOW, KW), lambda b: (layer, b, 0, 0)),
        ],
        out_specs=pl.BlockSpec((1, R, QW), lambda b: (b, 0, 0)),
        out_shape=jax.ShapeDtypeStruct((B, R, QW), F32),
        compiler_params=pltpu.CompilerParams(dimension_semantics=("parallel",)),
        name="swa_sample",
    )(sinks, qkv, ck, cv)


def _rope_tables(pos):
    half = HEAD_DIM // 2
    inv = jnp.power(ROPE_THETA, -jnp.arange(half, dtype=F32) * (2.0 / HEAD_DIM))
    ang = pos.astype(F32)[:, None] * inv[None, :]
    cos = jnp.cos(ang)
    sin = jnp.sin(ang)
    reps = LANES // HEAD_DIM
    cos_t = jnp.tile(jnp.concatenate([cos, cos], axis=1), (1, reps))
    sin_t = jnp.tile(jnp.concatenate([-sin, sin], axis=1), (1, reps))
    return (jnp.stack([cos_t, jnp.ones_like(cos_t)]), jnp.stack([sin_t, jnp.zeros_like(sin_t)]))


def kernel(x_prompt, x_sample, state_hgrn, cache_k_win, cache_v_win, norm_gains, w_ffn_in, w_ffn_out,
           w_hgrn_in, hgrn_lb_logits, hgrn_norm_gain, w_hgrn_out, w_attn_in, attn_sinks, w_attn_out):
    B, L, D = x_prompt.shape
    SB, SL, _ = x_sample.shape
    depth = norm_gains.shape[0]
    TP, TS = B * L, SB * SL
    H = hgrn_norm_gain.shape[1] // HG_DV
    n_kv = cache_k_win.shape[3]
    n_q = w_attn_out.shape[1] // HEAD_DIM
    QW, KW = n_q * HEAD_DIM, n_kv * HEAD_DIM
    buf = cache_k_win.shape[2]

    x = (x_prompt.reshape(TP, D), x_sample.reshape(TS, D))

    lb_all = jnp.cumsum(jax.nn.softmax(hgrn_lb_logits.astype(F32), axis=0), axis=0)
    lb_all = lb_all - lb_all[0:1]

    pos = jnp.concatenate([jnp.tile(jnp.arange(L, dtype=jnp.int32), B),
                           jnp.tile(PAST_LEN + jnp.arange(SL, dtype=jnp.int32), SB)])
    rope = _rope_tables(pos)

    hg_p, kp_l, vp_l, ks_l, vs_l = [], [], [], [], []
    hg_s = None
    for layer in range(depth):
        g = norm_gains[layer]
        x = _ffn_half(x, g[0:1], g[1:2], w_ffn_in, w_ffn_out, layer, 0)
        if layer % N_MIXERS == 0:
            a = layer // N_MIXERS
            lb = lb_all[a][None, :]
            lbp = (jnp.log(lb), jnp.log1p(-lb), 1.0 - lb)
            ng = hgrn_norm_gain[a][None, :].astype(F32)
            o_p, s_p = _hgrn_fused(x, g[2:3], w_hgrn_in, a, B, L, lbp, ng)
            o_p = o_p.reshape(TP, -1)
            p_s = _proj(x[TP:], g[2:3], w_hgrn_in, a, (TS, HGRN_SAMPLE_PROJ_TN))
            ps = jnp.pad(p_s.reshape(SB, SL, -1), ((0, 0), (0, SAMPLE_PAD - SL), (0, 0)))
            o_s, hg_s = _hgrn_scan(ps.reshape(SB * SAMPLE_PAD, -1), SB, SAMPLE_PAD, SAMPLE_PAD, H, SL,
                                   lbp, ng, (state_hgrn, a), F32, prev=hg_s)
            o_s = o_s.reshape(SB, SAMPLE_PAD, -1)[:, :SL].reshape(TS, -1).astype(BF16)
            x = _outproj(o_p, o_s, w_hgrn_out, x, g[3:4], a)
            hg_p.append(s_p.astype(state_hgrn.dtype))
        else:
            bb = layer // N_MIXERS
            p = _proj(x, g[2:3], w_attn_in, bb, ATTN_PROJ_TILES, rope=rope, n_rope_cols=QW + KW)
            sinks = attn_sinks[bb].astype(F32)
            o_p = _attn_prompt(p, sinks, B, L, n_q, n_kv)
            pn = p[TP:].reshape(SB, SL, -1)
            ck = cache_k_win.reshape(-1, SB, buf, KW)
            cv = cache_v_win.reshape(-1, SB, buf, KW)
            o_s = _attn_sample(jnp.pad(pn, ((0, 0), (0, 8 - SL), (0, 0))), ck, cv, bb, sinks, n_q, n_kv, SL)
            x = _outproj(o_p, o_s[:, :SL].reshape(TS, QW).astype(BF16), w_attn_out, x, g[3:4], bb)
            tails = [p[(s + 1) * L - buf:(s + 1) * L, QW:] for s in range(B)]
            kp_l.append(jnp.stack([t[:, :KW] for t in tails]).reshape(B, buf, n_kv, HEAD_DIM))
            vp_l.append(jnp.stack([t[:, KW:] for t in tails]).reshape(B, buf, n_kv, HEAD_DIM))
            k_new = pn[:, :, QW:QW + KW].reshape(SB, SL, n_kv, HEAD_DIM)
            v_new = pn[:, :, QW + KW:].reshape(SB, SL, n_kv, HEAD_DIM)
            ks_l.append(k_new)
            vs_l.append(v_new)
        x = _ffn_half(x, g[4:5], g[5:6], w_ffn_in, w_ffn_out, layer, 1,
                      out_split=TP if layer == depth - 1 else None)

    yp = x[0].reshape(B, L, D)
    ys = x[1].reshape(SB, SL, D)
    k_win_s = jnp.concatenate([cache_k_win, jnp.stack(ks_l).astype(cache_k_win.dtype)], axis=2)[:, :, -buf:]
    v_win_s = jnp.concatenate([cache_v_win, jnp.stack(vs_l).astype(cache_v_win.dtype)], axis=2)[:, :, -buf:]
    return (yp, ys, jnp.stack(hg_p), hg_s.astype(state_hgrn.dtype), jnp.stack(kp_l), jnp.stack(vp_l),
            k_win_s, v_win_s)
```

```python
import functools
import math

import numpy as np
import jax
import jax.numpy as jnp
from jax import lax
from jax.experimental import pallas as pl
from jax.experimental.pallas import tpu as pltpu

F32 = jnp.float32
BF16 = jnp.bfloat16

NORM_EPS = 1e-6
HEAD_DIM = 64
HG_DK = 128
HG_DV = 128
HG_CHUNK = 64
WINDOW = 128
PAST_LEN = 16384
ROPE_THETA = 10000.0
N_MIXERS = 2

LANES = 128
SUBLANES = 8
VMEM_BYTES_V7X = 64 * 1024 * 1024
VMEM_HEADROOM = 6 * 1024 * 1024

FFN_TM, FFN_TF = 1040, 256
FFN_W_SPLIT = 2
FFN_RC = 208
HGRN_SAMPLE_PROJ_TN = 1024
FUSED_HB, FUSED_NSEQ, FUSED_RB = 2, 2, 256
ATTN_PROJ_TILES = (2080, 512)
PROJ_X_DOUBLE_BUFFER_MAX_BYTES = 12 * 1024 * 1024
ROPE_ROW_CHUNK = 208
OUT_TM = 512
SAMPLE_PAD = 16


def _vmem_limit(nbytes):
    return int(min(VMEM_BYTES_V7X - VMEM_HEADROOM, max(nbytes * 5 // 4, 16 * 1024 * 1024)))


def _rms(x, g):
    return x * lax.rsqrt(jnp.mean(x * x, axis=-1, keepdims=True) + NORM_EPS) * g


def _dot(a, b):
    return jnp.dot(a, b, preferred_element_type=F32)


def _dot_nt(a, b):
    return lax.dot_general(a, b, (((1,), (1,)), ((), ())), preferred_element_type=F32)


def _dot_tn(a, b):
    return lax.dot_general(a, b, (((0,), (0,)), ((), ())), preferred_element_type=F32)


def _ffn_kernel(*refs, nt, nj, tm, rc, in_split, out_split):
    refs = list(refs)
    x_refs = [refs.pop(0) for _ in range(1 if in_split is None else 2)]
    gpre_ref, gpost_ref = refs.pop(0), refs.pop(0)
    wg_refs = [refs.pop(0) for _ in range(FFN_W_SPLIT)]
    wu_refs = [refs.pop(0) for _ in range(FFN_W_SPLIT)]
    wo_ref = refs.pop(0)
    y_refs = [refs.pop(0) for _ in range(1 if out_split is None else 2)]
    acc_ref, xn_ref, xc_ref, yc_ref, in_sem, out_sem = refs

    i = pl.program_id(0)
    j = pl.program_id(1)
    nch = tm // rc
    slot = i % 2
    pslot = 1 - slot
    k = j - 1
    gpre = gpre_ref[...]
    gpost = gpost_ref[...]

    def in_copy(row0, b):
        return pltpu.make_async_copy(x_refs[0].at[pl.ds(row0, rc)], xc_ref.at[b], in_sem.at[b])

    def out_copy(row0, b):
        return pltpu.make_async_copy(yc_ref.at[b], y_refs[0].at[pl.ds(row0, rc)], out_sem.at[b])

    def crossing_row(split):
        if split is None:
            return None
        assert split % rc != 0 and nt * tm - split < rc, "exactly one chunk must cross the split point"
        return (split // rc) * rc

    in_cross = crossing_row(in_split)
    out_cross = crossing_row(out_split)
    assert out_cross is None or out_cross >= (nt - 1) * tm, "only the last row tile may cross the output split"

    def in_copies_crossing(b):
        na = in_split - in_cross
        return [pltpu.make_async_copy(x_refs[0].at[pl.ds(in_cross, na)], xc_ref.at[b, pl.ds(0, na)], in_sem.at[b]),
                pltpu.make_async_copy(x_refs[1].at[pl.ds(0, rc - na)], xc_ref.at[b, pl.ds(na, rc - na)],
                                      in_sem.at[b])]

    def out_copies_crossing(b):
        na = out_split - out_cross
        return [pltpu.make_async_copy(yc_ref.at[b, pl.ds(0, na)], y_refs[0].at[pl.ds(out_cross, na)], out_sem.at[b]),
                pltpu.make_async_copy(yc_ref.at[b, pl.ds(na, rc - na)], y_refs[1].at[pl.ds(0, rc - na)],
                                      out_sem.at[b])]

    def static_in_copies(row0, b):
        return in_copies_crossing(b) if row0 == in_cross else [in_copy(row0, b)]

    def static_out_copies(row0, b):
        return out_copies_crossing(b) if row0 == out_cross else [out_copy(row0, b)]

    def task_row(t):
        return jnp.where(t < nch, (i - 1) * tm + t * rc, (i + 1) * tm + (t - nch) * rc)

    def task_live(t):
        return (t >= 0) & (t < 2 * nch) & jnp.where(t < nch, i >= 1, i <= nt - 2)

    def task_crosses(t):
        return False if in_cross is None else task_row(t) == in_cross

    @pl.when((i == 0) & (j == 0))
    def _():
        for c in range(nch):
            cps = static_in_copies(c * rc, c % 2)
            for cp in cps:
                cp.start()
            for cp in cps:
                cp.wait()
            xn_ref[0, c * rc:(c + 1) * rc] = _rms(xc_ref[c % 2], gpre).astype(BF16)

    @pl.when(task_live(k) & jnp.logical_not(task_crosses(k)))
    def _():
        in_copy(0, k & 1).wait()

    @pl.when(task_live(k) & (k >= 2) & (k < nch))
    def _():
        out_copy(0, k & 1).wait()

    @pl.when((j == nch + 1) & (i >= 1))
    def _():
        out_copy(0, 0).wait()
        out_copy(0, 1).wait()

    @pl.when(task_live(j) & jnp.logical_not(task_crosses(j)))
    def _():
        in_copy(task_row(j), j & 1).start()

    if in_cross is not None:
        @pl.when(task_live(k) & task_crosses(k))
        def _():
            for cp in in_copies_crossing(k & 1):
                cp.wait()

        @pl.when(task_live(j) & task_crosses(j))
        def _():
            for cp in in_copies_crossing(j & 1):
                cp.start()

    def accumulate(first):
        xn = xn_ref[slot]
        hg = _dot(xn, jnp.concatenate([r[...].astype(BF16) for r in wg_refs], axis=0))
        hu = _dot(xn, jnp.concatenate([r[...].astype(BF16) for r in wu_refs], axis=0))
        act = (hg * jax.nn.sigmoid(hg) * hu).astype(BF16)
        contrib = _dot(act, wo_ref[...].astype(BF16))
        if first:
            acc_ref[slot] = contrib
        else:
            acc_ref[slot] += contrib

    is_epi = (k >= 0) & (k < nch) & (i >= 1)
    is_pro = (k >= nch) & (k < 2 * nch) & (i <= nt - 2)

    @pl.when(j == 0)
    def _():
        accumulate(True)

    @pl.when(is_epi)
    def _():
        accumulate(False)
        b = k & 1
        rows = pl.ds(pl.multiple_of(k * rc, rc), rc)
        yc_ref[b] = xc_ref[b] + 0.5 * _rms(acc_ref[pslot, rows], gpost)
        out_copy((i - 1) * tm + k * rc, b).start()

    @pl.when(is_pro)
    def _():
        accumulate(False)
        rows = pl.ds(pl.multiple_of((k - nch) * rc, rc), rc)
        xn_ref[pslot, rows] = _rms(xc_ref[k & 1], gpre).astype(BF16)

    @pl.when((j >= 1) & jnp.logical_not(is_epi) & jnp.logical_not(is_pro))
    def _():
        accumulate(False)

    @pl.when((i == nt - 1) & (j == nj - 1))
    def _():
        last = (nt - 1) % 2
        for c in range(nch):
            b = c % 2
            row0 = (nt - 1) * tm + c * rc
            cps = static_in_copies(row0, b)
            for cp in cps:
                cp.start()
            for cp in cps:
                cp.wait()
            yc_ref[b] = xc_ref[b] + 0.5 * _rms(acc_ref[last, c * rc:(c + 1) * rc], gpost)
            ocs = static_out_copies(row0, b)
            for oc in ocs:
                oc.start()
            for oc in ocs:
                oc.wait()


def _ffn_half(x, g_pre, g_post, w_in, w_out, layer, half, out_split=None):
    xs = x if isinstance(x, (tuple, list)) else (x,)
    in_split = xs[0].shape[0] if len(xs) == 2 else None
    T = sum(a.shape[0] for a in xs)
    D = xs[0].shape[1]
    F = w_out.shape[2]
    TM, TF, RC = FFN_TM, FFN_TF, FFN_RC
    nj = F // TF
    nt = T // TM
    assert T % TM == 0 and F % TF == 0 and TM % RC == 0 and TM // RC >= 2 and nj >= 2 * (TM // RC) + 2
    est = (2 * TM * D * (4 + 2) + 4 * RC * D * 4 + 2 * 3 * D * TF * 4
           + 3 * D * TF * 2 + 4 * TM * TF * 4 + TM * D * 4)
    any_spec = pl.BlockSpec(memory_space=pl.ANY)
    if out_split is None:
        out_specs, out_shape = any_spec, jax.ShapeDtypeStruct((T, D), F32)
    else:
        out_specs = [any_spec, any_spec]
        out_shape = [jax.ShapeDtypeStruct((out_split, D), F32), jax.ShapeDtypeStruct((T - out_split, D), F32)]
    return pl.pallas_call(
        functools.partial(_ffn_kernel, nt=nt, nj=nj, tm=TM, rc=RC, in_split=in_split, out_split=out_split),
        grid=(nt, nj),
        in_specs=[any_spec] * len(xs) + [
            pl.BlockSpec((1, D), lambda i, j: (0, 0)),
            pl.BlockSpec((1, D), lambda i, j: (0, 0)),
        ] + [
            pl.BlockSpec((None, None, D // FFN_W_SPLIT, TF), lambda i, j, s=s, c=c: (layer, half, s, c * nj + j))
            for c in range(2) for s in range(FFN_W_SPLIT)
        ] + [
            pl.BlockSpec((None, None, TF, D), lambda i, j: (layer, half, j, 0)),
        ],
        out_specs=out_specs,
        out_shape=out_shape,
        scratch_shapes=[pltpu.VMEM((2, TM, D), F32), pltpu.VMEM((2, TM, D), BF16),
                        pltpu.VMEM((2, RC, D), F32), pltpu.VMEM((2, RC, D), F32),
                        pltpu.SemaphoreType.DMA((2,)), pltpu.SemaphoreType.DMA((2,))],
        compiler_params=pltpu.CompilerParams(
            dimension_semantics=("arbitrary", "arbitrary"),
            vmem_limit_bytes=_vmem_limit(est)),
        name="ffn_half",
    )(*xs, g_pre, g_post, *([w_in] * (2 * FFN_W_SPLIT)), w_out)


def _rope_tile(y, cos, sin_signed):
    lane = lax.broadcasted_iota(jnp.int32, y.shape, 1)
    first_half = (lane & (HEAD_DIM - 1)) < (HEAD_DIM // 2)
    rot = jnp.where(first_half,
                    pltpu.roll(y, LANES - HEAD_DIM // 2, 1),
                    pltpu.roll(y, HEAD_DIM // 2, 1))
    return y * cos + rot * sin_signed


def _proj_kernel(x_ref, g_ref, w_ref, o_ref, xn_ref):
    @pl.when(pl.program_id(1) == 0)
    def _():
        xn_ref[...] = _rms(x_ref[...], g_ref[...]).astype(BF16)

    o_ref[...] = _dot(xn_ref[...], w_ref[...].astype(BF16))


def _proj_rope_kernel(x_ref, g_ref, w_ref, cos_ref, sin_ref, o_ref, xn_ref, *, row_chunk):
    @pl.when(pl.program_id(1) == 0)
    def _():
        xn_ref[...] = _rms(x_ref[...], g_ref[...]).astype(BF16)

    wb = w_ref[...].astype(BF16)
    for r0 in range(0, xn_ref.shape[0], row_chunk):
        rows = slice(r0, r0 + row_chunk)
        y = _dot(xn_ref[rows], wb)
        cos = cos_ref[rows]
        sin = sin_ref[rows]
        for c in range(y.shape[1] // LANES):
            cols = slice(c * LANES, (c + 1) * LANES)
            o_ref[rows, cols] = _rope_tile(y[:, cols], cos, sin)


def _proj(x, g, w, layer, tiles, rope=None, n_rope_cols=0):
    T, D = x.shape
    N = w.shape[2]
    TM, TN = tiles
    assert T % TM == 0 and N % TN == 0 and n_rope_cols % TN == 0
    x_buffers = 1 if TM * D * 4 > PROJ_X_DOUBLE_BUFFER_MAX_BYTES else 2
    est = x_buffers * TM * D * 4 + TM * D * 2 + 2 * D * TN * 4 + D * TN * 2 + 3 * TM * TN * 4
    in_specs = [
        pl.BlockSpec((TM, D), lambda i, j: (i, 0), pipeline_mode=pl.Buffered(x_buffers)),
        pl.BlockSpec((1, D), lambda i, j: (0, 0)),
        pl.BlockSpec((None, D, TN), lambda i, j: (layer, 0, j)),
    ]
    args = [x, g, w]
    if rope is None:
        body = _proj_kernel
    else:
        assert TM % ROPE_ROW_CHUNK == 0
        body = functools.partial(_proj_rope_kernel, row_chunk=ROPE_ROW_CHUNK)
        n_rope_blocks = n_rope_cols // TN
        table = pl.BlockSpec((None, TM, LANES), lambda i, j: (jnp.where(j < n_rope_blocks, 0, 1), i, 0))
        in_specs += [table, table]
        args += list(rope)
    return pl.pallas_call(
        body,
        grid=(T // TM, N // TN),
        in_specs=in_specs,
        out_specs=pl.BlockSpec((TM, TN), lambda i, j: (i, j)),
        out_shape=jax.ShapeDtypeStruct((T, N), F32),
        scratch_shapes=[pltpu.VMEM((TM, D), BF16)],
        compiler_params=pltpu.CompilerParams(
            dimension_semantics=("parallel", "arbitrary"),
            vmem_limit_bytes=_vmem_limit(est)),
        name="mixer_in_proj",
    )(*args)


def _outproj_kernel(op_ref, os_ref, w_ref, x_ref, g_ref, y_ref, wbf_ref, *, n_prompt_tiles, ts):
    i = pl.program_id(0)

    @pl.when(i == 0)
    def _():
        wbf_ref[...] = w_ref[...].astype(BF16)

    @pl.when(i < n_prompt_tiles)
    def _():
        y_ref[...] = x_ref[...] + _rms(_dot(op_ref[...], wbf_ref[...]), g_ref[...])

    @pl.when(i == n_prompt_tiles)
    def _():
        y_ref[0:ts] = x_ref[0:ts] + _rms(_dot(os_ref[...], wbf_ref[...]), g_ref[...])


def _outproj(o_prompt, o_sample, w, x, g, layer):
    T, D = x.shape
    TP, K = o_prompt.shape
    TS = o_sample.shape[0]
    TM = OUT_TM
    assert TP % TM == 0 and TS <= TM and TS % SUBLANES == 0 and TP + TS == T
    npt = TP // TM
    est = K * D * (4 + 2) + 2 * TM * K * 2 + 2 * TS * K * 2 + 5 * TM * D * 4
    return pl.pallas_call(
        functools.partial(_outproj_kernel, n_prompt_tiles=npt, ts=TS),
        grid=(npt + 1,),
        in_specs=[
            pl.BlockSpec((TM, K), lambda i: (jnp.minimum(i, npt - 1), 0)),
            pl.BlockSpec((TS, K), lambda i: (0, 0)),
            pl.BlockSpec((None, K, D), lambda i: (layer, 0, 0), pipeline_mode=pl.Buffered(1)),
            pl.BlockSpec((TM, D), lambda i: (i, 0)),
            pl.BlockSpec((1, D), lambda i: (0, 0)),
        ],
        out_specs=pl.BlockSpec((TM, D), lambda i: (i, 0)),
        out_shape=jax.ShapeDtypeStruct((T, D), F32),
        scratch_shapes=[pltpu.VMEM((K, D), BF16)],
        compiler_params=pltpu.CompilerParams(
            dimension_semantics=("arbitrary",),
            vmem_limit_bytes=_vmem_limit(est)),
        name="mixer_out_proj",
    )(o_prompt, o_sample, w, x, g)


def _tiles(x):
    return [x[SUBLANES * j:SUBLANES * (j + 1)] for j in range(x.shape[0] // SUBLANES)]


def _bcast_row(t, i):
    return jnp.broadcast_to(t[i:i + 1], t.shape)


def _tile_prefix(t, sub):
    p = t + jnp.where(sub >= 1, pltpu.roll(t, 1, 0), 0.0)
    p = p + jnp.where(sub >= 2, pltpu.roll(p, 2, 0), 0.0)
    return p + jnp.where(sub >= 4, pltpu.roll(p, 4, 0), 0.0)


def _hgrn_gates(fz, loglb, log1m, onem, valid):
    e = jnp.exp(-jnp.abs(fz))
    lsig = jnp.minimum(fz, 0.0) - jnp.log(1.0 + e)
    ct = log1m + lsig
    g = jnp.maximum(loglb, ct) + jnp.log(1.0 + jnp.exp(-jnp.abs(loglb - ct)))
    r = 1.0 / (1.0 + e)
    k = onem * jnp.where(fz > 0.0, e * r, r)
    if valid is not None:
        g = jnp.where(valid, g, 0.0)
        k = jnp.where(valid, k, 0.0)
    return g, k


def _hgrn_level_operands(m, qt, kt, gt, bt, sub):
    nv = len(qt)
    zero = jnp.zeros_like(qt[0])
    qm, km = [], []
    if m >= SUBLANES:
        mv = m // SUBLANES
        for j in range(nv):
            jr = (j // (2 * mv)) * (2 * mv) + mv - 1
            ref = _bcast_row(bt[jr], SUBLANES - 1)
            if j % (2 * mv) >= mv:
                qm.append(qt[j] * jnp.exp(bt[j] - ref))
                km.append(zero)
            else:
                qm.append(zero)
                km.append(kt[j] * jnp.exp(ref - bt[j]))
        return qm, km
    upper = (sub & m) != 0
    for j in range(nv):
        if m == 1:
            e = jnp.exp(gt[j])
            qm.append(jnp.where(upper, qt[j] * e, 0.0))
            km.append(jnp.where(upper, 0.0, kt[j]))
            continue
        if m == 4:
            ref = _bcast_row(bt[j], 3)
        else:
            ref = jnp.where(sub < 4, _bcast_row(bt[j], 1), _bcast_row(bt[j], 5))
        e = jnp.exp(-jnp.abs(bt[j] - ref))
        qm.append(jnp.where(upper, qt[j] * e, 0.0))
        km.append(jnp.where(upper, 0.0, kt[j] * e))
    return qm, km


def _hgrn_chunk_group(loads, params, sts, *, C, valid_len):
    nh = len(loads)
    sub = lax.broadcasted_iota(jnp.int32, (SUBLANES, LANES), 0)
    valid = None
    if valid_len < C:
        valid = lax.broadcasted_iota(jnp.int32, (C, LANES), 0) < valid_len
    ti = lax.broadcasted_iota(jnp.int32, (C, C), 0)
    si = lax.broadcasted_iota(jnp.int32, (C, C), 1)

    gk = [_hgrn_gates(loads[h][1], params[h][0], params[h][1], params[h][2], valid) for h in range(nh)]

    gts, bts = [], []
    for h in range(nh):
        gt = _tiles(gk[h][0])
        bt = []
        for j, t in enumerate(gt):
            p = _tile_prefix(t, sub)
            bt.append(p if j == 0 else p + _bcast_row(bt[j - 1], SUBLANES - 1))
        gts.append(gt)
        bts.append(bt)

    scores = []
    for h in range(nh):
        q, k = loads[h][0], gk[h][1]
        qt, kt = _tiles(q), _tiles(k)
        a = jnp.where(ti == si, _dot_nt(q.astype(BF16), k.astype(BF16)), 0.0)
        m = C // 2
        while m >= 1:
            qm, km = _hgrn_level_operands(m, qt, kt, gts[h], bts[h], sub)
            am = _dot_nt(jnp.concatenate(qm, axis=0).astype(BF16), jnp.concatenate(km, axis=0).astype(BF16))
            if 2 * m < C:
                am = jnp.where((ti & (-2 * m)) == (si & (-2 * m)), am, 0.0)
            a = a + am
            m //= 2
        scores.append(a)

    outs = []
    for h in range(nh):
        q, _, v, gz = loads[h]
        k = gk[h][1]
        ng = params[h][3]
        st = sts[h]
        b = jnp.concatenate(bts[h], axis=0)
        o = _dot(scores[h].astype(BF16), v.astype(BF16)) + _dot_nt((q * jnp.exp(b)).astype(BF16), st.astype(BF16))
        bl = b[C - 1:C]
        kb = (k * jnp.exp(bl - b)).astype(BF16)
        st_new = st * jnp.exp(bl) + _dot_tn(v.astype(BF16), kb)
        on = o * lax.rsqrt(jnp.mean(o * o, axis=-1, keepdims=True) + NORM_EPS) * ng
        outs.append((on * (gz * jax.nn.sigmoid(gz)), st_new))
    return outs


def _hgrn_kernel(*refs, C, NC, HB, valid_len, has_s0, n_prev):
    refs = list(refs)
    q_ref, f_ref, i_ref, g_ref, loglb_ref, log1m_ref, onem_ref, ng_ref = (refs.pop(0) for _ in range(8))
    s0_ref = refs.pop(0) if has_s0 else None
    prev_ref = refs.pop(0) if n_prev else None
    o_ref, s_ref, st_ref = refs

    for h in range(HB):
        if has_s0:
            st_ref[h] = s0_ref[0, h].T
        else:
            st_ref[h] = jnp.zeros((HG_DV, HG_DK), F32)

    def body(c, carry):
        rows = pl.ds(pl.multiple_of(c * C, C), C)
        cols = [slice(h * LANES, (h + 1) * LANES) for h in range(HB)]
        loads = [(q_ref[rows, cs], f_ref[rows, cs], i_ref[rows, cs], g_ref[rows, cs]) for cs in cols]
        params = [(loglb_ref[:, cs], log1m_ref[:, cs], onem_ref[:, cs], ng_ref[:, cs]) for cs in cols]
        outs = _hgrn_chunk_group(loads, params, [st_ref[h] for h in range(HB)], C=C, valid_len=valid_len)
        for h in range(HB):
            st_ref[h] = outs[h][1]
            o_ref[rows, cols[h]] = outs[h][0].astype(o_ref.dtype)
        return carry

    if NC == 1:
        body(0, 0)
    else:
        lax.fori_loop(0, NC, body, 0)

    for layer in range(n_prev):
        s_ref[layer] = prev_ref[layer]
    for h in range(HB):
        s_ref[n_prev, 0, h] = st_ref[h].T


def _hgrn_scan(p, nseq, L, C, HB, valid_len, lbp, ng, s0, out_dtype, prev=None):
    H = ng.shape[1] // LANES
    nhb = H // HB
    W = HB * LANES
    NC = L // C
    loglb, log1m, onem = lbp
    has_s0 = s0 is not None

    def sect(s):
        return pl.BlockSpec((L, W), lambda b, hb, s=s: (b, s * nhb + hb))

    vec = pl.BlockSpec((1, W), lambda b, hb: (0, hb))
    in_specs = [sect(0), sect(1), sect(2), sect(3), vec, vec, vec, vec]
    args = [p, p, p, p, loglb, log1m, onem, ng]
    if has_s0:
        s0_all, s0_layer = s0
        in_specs.append(pl.BlockSpec((None, 1, HB, HG_DK, HG_DV), lambda b, hb: (s0_layer, b, hb, 0, 0)))
        args.append(s0_all)
    n_prev = 0 if prev is None else prev.shape[0]
    if n_prev:
        in_specs.append(pl.BlockSpec((n_prev, 1, HB, HG_DK, HG_DV), lambda b, hb: (0, b, hb, 0, 0)))
        args.append(prev)
    est = (2 * 4 * L * W * 4 + 2 * L * W * 4 + (6 + 4 * (2 * n_prev + 1)) * HB * HG_DK * HG_DV * 4
           + (8 << 20))
    return pl.pallas_call(
        functools.partial(_hgrn_kernel, C=C, NC=NC, HB=HB, valid_len=valid_len, has_s0=has_s0, n_prev=n_prev),
        grid=(nseq, nhb),
        in_specs=in_specs,
        out_specs=[pl.BlockSpec((L, W), lambda b, hb: (b, hb)),
                   pl.BlockSpec((n_prev + 1, 1, HB, HG_DK, HG_DV), lambda b, hb: (0, b, hb, 0, 0))],
        out_shape=[jax.ShapeDtypeStruct((nseq * L, H * LANES), out_dtype),
                   jax.ShapeDtypeStruct((n_prev + 1, nseq, H, HG_DK, HG_DV), F32)],
        scratch_shapes=[pltpu.VMEM((HB, HG_DV, HG_DK), F32)],
        compiler_params=pltpu.CompilerParams(
            dimension_semantics=("parallel", "parallel"),
            vmem_limit_bytes=_vmem_limit(est)),
        name="hgrn_scan",
    )(*args)


def _hgrn_fused_kernel(x_hbm, g_ref, wq_ref, wf_ref, wi_ref, wg_ref, loglb_ref, log1m_ref, onem_ref, ng_ref,
                       o_ref, s_ref, xn_ref, xc_ref, proj_ref, st_ref, in_sem, *, L, C, RB, NSEQ, HB):
    sg = pl.program_id(0)
    hb = pl.program_id(1)
    D = xn_ref.shape[1]
    WC = HB * LANES
    NB = L // RB
    CPB = RB // C
    w_refs = (wq_ref, wf_ref, wi_ref, wg_ref)
    assert CPB == len(w_refs)

    @pl.when(hb == 0)
    def _():
        g = g_ref[...]
        nchunk = NSEQ * NB

        def copy(c):
            return pltpu.make_async_copy(x_hbm.at[pl.ds(sg * (NSEQ * L) + c * RB, RB)],
                                         xc_ref.at[c % 2], in_sem.at[c % 2])
        copy(0).start()
        for c in range(nchunk):
            copy(c).wait()
            if c + 1 < nchunk:
                copy(c + 1).start()
            xn_ref[c * RB:(c + 1) * RB] = _rms(xc_ref[c % 2], g).astype(BF16)

    def project(rb, section, slot):
        rows = [xn_ref[pl.ds(pl.multiple_of(s * L + rb * RB, RB), RB)] for s in range(NSEQ)]
        y = _dot(jnp.concatenate(rows, axis=0), w_refs[section][...].astype(BF16))
        proj_ref[slot, :, section * WC:(section + 1) * WC] = y

    for ch in range(NSEQ * HB):
        st_ref[ch] = jnp.zeros((HG_DV, HG_DK), F32)
    for section in range(CPB):
        project(0, section, 0)

    params = [(loglb_ref[:, h * LANES:(h + 1) * LANES], log1m_ref[:, h * LANES:(h + 1) * LANES],
               onem_ref[:, h * LANES:(h + 1) * LANES], ng_ref[:, h * LANES:(h + 1) * LANES])
              for _ in range(NSEQ) for h in range(HB)]

    def block(rb, project_next):
        slot = rb % 2
        for ci in range(CPB):
            loads = []
            for s in range(NSEQ):
                rows = slice(s * RB + ci * C, s * RB + (ci + 1) * C)
                for h in range(HB):
                    loads.append(tuple(proj_ref[slot, rows, sec * WC + h * LANES:sec * WC + (h + 1) * LANES]
                                       for sec in range(4)))
            if project_next:
                project(rb + 1, ci, 1 - slot)
            outs = _hgrn_chunk_group(loads, params, [st_ref[ch] for ch in range(NSEQ * HB)], C=C, valid_len=C)
            orow = pl.ds(pl.multiple_of(rb * RB + ci * C, C), C)
            for s in range(NSEQ):
                for h in range(HB):
                    ch = s * HB + h
                    st_ref[ch] = outs[ch][1]
                    o_ref[s, orow, h * LANES:(h + 1) * LANES] = outs[ch][0].astype(o_ref.dtype)

    def body(rb, carry):
        block(rb, True)
        return carry

    lax.fori_loop(0, NB - 1, body, 0)
    block(NB - 1, False)

    for s in range(NSEQ):
        for h in range(HB):
            s_ref[s, h] = st_ref[s * HB + h].T


def _hgrn_fused(x, g, w, layer, B, L, lbp, ng):
    D = x.shape[1]
    H = ng.shape[1] // LANES
    HB, NSEQ, RB, C = FUSED_HB, FUSED_NSEQ, FUSED_RB, HG_CHUNK
    WC = HB * LANES
    nhb = H // HB
    assert B % NSEQ == 0 and H % HB == 0 and L % RB == 0 and RB // C == 4
    loglb, log1m, onem = lbp

    def sect(s):
        return pl.BlockSpec((None, D, WC), lambda sg, hb, s=s: (layer, 0, s * nhb + hb))

    vec = pl.BlockSpec((1, WC), lambda sg, hb: (0, hb))
    est = (NSEQ * L * D * 2 + 2 * RB * D * 4 + 2 * NSEQ * RB * 4 * WC * 4 + 2 * 4 * D * WC * 4
           + 2 * NSEQ * L * WC * 2 + (6 << 20))
    return pl.pallas_call(
        functools.partial(_hgrn_fused_kernel, L=L, C=C, RB=RB, NSEQ=NSEQ, HB=HB),
        grid=(B // NSEQ, nhb),
        in_specs=[pl.BlockSpec(memory_space=pl.ANY), pl.BlockSpec((1, D), lambda sg, hb: (0, 0)),
                  sect(0), sect(1), sect(2), sect(3), vec, vec, vec, vec],
        out_specs=[pl.BlockSpec((NSEQ, L, WC), lambda sg, hb: (sg, 0, hb)),
                   pl.BlockSpec((NSEQ, HB, HG_DK, HG_DV), lambda sg, hb: (sg, hb, 0, 0))],
        out_shape=[jax.ShapeDtypeStruct((B, L, H * LANES), BF16),
                   jax.ShapeDtypeStruct((B, H, HG_DK, HG_DV), F32)],
        scratch_shapes=[pltpu.VMEM((NSEQ * L, D), BF16), pltpu.VMEM((2, RB, D), F32),
                        pltpu.VMEM((2, NSEQ * RB, 4 * WC), F32), pltpu.VMEM((NSEQ * HB, HG_DV, HG_DK), F32),
                        pltpu.SemaphoreType.DMA((2,))],
        compiler_params=pltpu.CompilerParams(
            dimension_semantics=("arbitrary", "arbitrary"),
            vmem_limit_bytes=_vmem_limit(est)),
        name="hgrn_fused",
    )(x, g, w, w, w, w, loglb, log1m, onem, ng)


def _sink_softmax(scores, sink):
    m = sink
    for s in scores:
        m = jnp.maximum(m, jnp.max(s, axis=-1, keepdims=True))
    ps = [jnp.exp(s - m) for s in scores]
    denom = jnp.exp(sink - m)
    for p in ps:
        denom = denom + jnp.sum(p, axis=-1, keepdims=True)
    return [p / denom for p in ps]


def _attn_prompt_kernel(sinks_ref, q_ref, kc_ref, kp_ref, vc_ref, vp_ref, o_ref, *, n_kv, group):
    i = pl.program_id(1)
    W = WINDOW
    HD = HEAD_DIM
    j = lax.broadcasted_iota(jnp.int32, (2 * W, W), 0)
    r = lax.broadcasted_iota(jnp.int32, (2 * W, W), 1)
    mask = (j >= jnp.maximum(r, jnp.where(i > 0, 0, W))) & (j <= r + W)
    scale = HD ** -0.5
    zeros = jnp.zeros((HD, W), F32)

    s_all, vt_all = [], []
    for c in range(n_kv // 2):
        cs = slice(c * LANES, (c + 1) * LANES)
        ktile = jnp.concatenate([kp_ref[:, cs], kc_ref[:, cs]], axis=0).astype(BF16)
        vt_all.append(jnp.concatenate([vp_ref[:, cs], vc_ref[:, cs]], axis=0).T.astype(BF16))
        for par in range(2):
            kh = 2 * c + par
            rhs = []
            for pr in range(group // 2):
                c0 = (kh * group + 2 * pr) * HD
                pt = (q_ref[:, c0:c0 + LANES] * scale).T
                for half in range(2):
                    blk = pt[half * HD:(half + 1) * HD]
                    rhs.append(jnp.concatenate([blk, zeros] if par == 0 else [zeros, blk], axis=0))
            s_all.append(_dot(ktile, jnp.concatenate(rhs, axis=1).astype(BF16)))

    p_all, inv_all = [], []
    for kh in range(n_kv):
        pn, inv = [], []
        for hh in range(group):
            sink = sinks_ref[kh * group + hh]
            sm = jnp.where(mask, s_all[kh][:, hh * W:(hh + 1) * W], -jnp.inf)
            mx = jnp.maximum(jnp.max(sm, axis=0, keepdims=True), sink)
            p = jnp.exp(sm - mx)
            inv.append(1.0 / (jnp.sum(p, axis=0, keepdims=True) + jnp.exp(sink - mx)))
            pn.append(p.astype(BF16))
        p_all.append(jnp.concatenate(pn, axis=1))
        inv_all.append(jnp.concatenate(inv, axis=1))

    for kh in range(n_kv):
        par = kh % 2
        o_t = _dot(vt_all[kh // 2], p_all[kh])[par * HD:(par + 1) * HD] * inv_all[kh]
        for pr in range(group // 2):
            pair_t = jnp.concatenate([o_t[:, (2 * pr) * W:(2 * pr + 1) * W],
                                      o_t[:, (2 * pr + 1) * W:(2 * pr + 2) * W]], axis=0)
            c0 = (kh * group + 2 * pr) * HD
            o_ref[:, c0:c0 + LANES] = pair_t.T.astype(o_ref.dtype)


def _attn_prompt(p, sinks, B, L, n_q, n_kv):
    W = WINDOW
    nb = L // W
    QW = n_q * HEAD_DIM
    KW = n_kv * HEAD_DIM
    kblk = QW // KW
    vblk = (QW + KW) // KW
    est = 2 * (W * QW * 4 + 4 * W * KW * 4 + W * QW * 2) + (16 << 20)
    return pl.pallas_call(
        functools.partial(_attn_prompt_kernel, n_kv=n_kv, group=n_q // n_kv),
        grid=(B, nb),
        in_specs=[
            pl.BlockSpec(memory_space=pltpu.SMEM),
            pl.BlockSpec((W, QW), lambda b, i: (b * nb + i, 0)),
            pl.BlockSpec((W, KW), lambda b, i: (b * nb + i, kblk)),
            pl.BlockSpec((W, KW), lambda b, i: (b * nb + jnp.maximum(i - 1, 0), kblk)),
            pl.BlockSpec((W, KW), lambda b, i: (b * nb + i, vblk)),
            pl.BlockSpec((W, KW), lambda b, i: (b * nb + jnp.maximum(i - 1, 0), vblk)),
        ],
        out_specs=pl.BlockSpec((W, QW), lambda b, i: (b * nb + i, 0)),
        out_shape=jax.ShapeDtypeStruct((B * L, QW), BF16),
        compiler_params=pltpu.CompilerParams(
            dimension_semantics=("parallel", "parallel"),
            vmem_limit_bytes=_vmem_limit(est)),
        name="swa_prompt",
    )(sinks, p, p, p, p, p)


def _attn_sample_kernel(sinks_ref, qkv_ref, ck_ref, cv_ref, o_ref, *, n_q, n_kv, n_new):
    R = qkv_ref.shape[1]
    W = WINDOW
    HD = HEAD_DIM
    group = n_q // n_kv
    npair = group // 2
    QW = n_q * HD
    KW = n_kv * HD
    RR = npair * R
    scale = HD ** -0.5
    row = lax.broadcasted_iota(jnp.int32, (RR, 2 * W), 0)
    r = row & (R - 1)
    j = lax.broadcasted_iota(jnp.int32, (RR, 2 * W), 1)
    mask = (j >= r) & (j <= r + W) & (j < W + n_new)
    pair_of_row = lax.broadcasted_iota(jnp.int32, (RR, 1), 0) // R
    first = lax.broadcasted_iota(jnp.int32, (2 * W, LANES), 1) < HD
    zpad = jnp.zeros((W - R, LANES), F32)

    s_all, v_all = [], []
    for c in range(n_kv // 2):
        cs = slice(c * LANES, (c + 1) * LANES)
        kn = qkv_ref[0, :, QW + c * LANES:QW + (c + 1) * LANES]
        vn = qkv_ref[0, :, QW + KW + c * LANES:QW + KW + (c + 1) * LANES]
        ktile = jnp.concatenate([ck_ref[0, :, cs], kn, zpad], axis=0)
        vtile = jnp.concatenate([cv_ref[0, :, cs], vn, zpad], axis=0)
        kswap = pltpu.roll(ktile, HD, 1)
        vswap = pltpu.roll(vtile, HD, 1)
        for par in range(2):
            kh = 2 * c + par
            k_lo, k_hi = (ktile, kswap) if par == 0 else (kswap, ktile)
            v_lo, v_hi = (vtile, vswap) if par == 0 else (vswap, vtile)
            kcat = jnp.concatenate([jnp.where(first, k_lo, 0.0), jnp.where(first, 0.0, k_hi)], axis=0)
            vcat = jnp.concatenate([jnp.where(first, v_lo, 0.0), jnp.where(first, 0.0, v_hi)], axis=0)
            qs = jnp.concatenate(
                [qkv_ref[0, :, (kh * group + 2 * pr) * HD:(kh * group + 2 * pr) * HD + LANES]
                 for pr in range(npair)], axis=0) * scale
            s_all.append(_dot_nt(qs.astype(BF16), kcat.astype(BF16)))
            v_all.append(vcat.astype(BF16))

    p_all = []
    for kh in range(n_kv):
        halves = []
        for hf in range(2):
            sink = jnp.full((RR, 1), sinks_ref[kh * group + hf], F32)
            for pr in range(1, npair):
                sink = jnp.where(pair_of_row == pr, sinks_ref[kh * group + 2 * pr + hf], sink)
            sm = jnp.where(mask, s_all[kh][:, hf * 2 * W:(hf + 1) * 2 * W], -jnp.inf)
            (p,) = _sink_softmax([sm], sink)
            halves.append(p.astype(BF16))
        p_all.append(jnp.concatenate(halves, axis=1))

    for kh in range(n_kv):
        o = _dot(p_all[kh], v_all[kh])
        for pr in range(npair):
            c0 = (kh * group + 2 * pr) * HD
            o_ref[0, :, c0:c0 + LANES] = o[pr * R:(pr + 1) * R]


def _attn_sample(qkv, ck, cv, layer, sinks, n_q, n_kv, n_new):
    B, R, NW = qkv.shape
    QW = n_q * HEAD_DIM
    KW = n_kv * HEAD_DIM
    return pl.pallas_call(
        functools.partial(_attn_sample_kernel, n_q=n_q, n_kv=n_kv, n_new=n_new),
        grid=(B,),
        in_specs=[
            pl.BlockSpec(memory_space=pltpu.SMEM),
            pl.BlockSpec((1, R, NW), lambda b: (b, 0, 0)),
            pl.BlockSpec((None, 1, WINDOW, KW), lambda b: (layer, b, 0, 0)),
            pl.BlockSpec((None, 1, WINDOW, KW), lambda b: (layer, b, 0, 0)),
        ],
        out_specs=pl.BlockSpec((1, R, QW), lambda b: (b, 0, 0)),
        out_shape=jax.ShapeDtypeStruct((B, R, QW), F32),
        compiler_params=pltpu.CompilerParams(dimension_semantics=("parallel",)),
        name="swa_sample",
    )(sinks, qkv, ck, cv)


def _rope_tables(pos):
    half = HEAD_DIM // 2
    inv = jnp.power(ROPE_THETA, -jnp.arange(half, dtype=F32) * (2.0 / HEAD_DIM))
    ang = pos.astype(F32)[:, None] * inv[None, :]
    cos = jnp.cos(ang)
    sin = jnp.sin(ang)
    reps = LANES // HEAD_DIM
    cos_t = jnp.tile(jnp.concatenate([cos, cos], axis=1), (1, reps))
    sin_t = jnp.tile(jnp.concatenate([-sin, sin], axis=1), (1, reps))
    return (jnp.stack([cos_t, jnp.ones_like(cos_t)]), jnp.stack([sin_t, jnp.zeros_like(sin_t)]))


def kernel(x_prompt, x_sample, state_hgrn, cache_k_win, cache_v_win, norm_gains, w_ffn_in, w_ffn_out,
           w_hgrn_in, hgrn_lb_logits, hgrn_norm_gain, w_hgrn_out, w_attn_in, attn_sinks, w_attn_out):
    B, L, D = x_prompt.shape
    SB, SL, _ = x_sample.shape
    depth = norm_gains.shape[0]
    TP, TS = B * L, SB * SL
    H = hgrn_norm_gain.shape[1] // HG_DV
    n_kv = cache_k_win.shape[3]
    n_q = w_attn_out.shape[1] // HEAD_DIM
    QW, KW = n_q * HEAD_DIM, n_kv * HEAD_DIM
    buf = cache_k_win.shape[2]

    x = (x_prompt.reshape(TP, D), x_sample.reshape(TS, D))

    lb_all = jnp.cumsum(jax.nn.softmax(hgrn_lb_logits.astype(F32), axis=0), axis=0)
    lb_all = lb_all - lb_all[0:1]

    pos = jnp.concatenate([jnp.tile(jnp.arange(L, dtype=jnp.int32), B),
                           jnp.tile(PAST_LEN + jnp.arange(SL, dtype=jnp.int32), SB)])
    rope = _rope_tables(pos)

    hg_p, kp_l, vp_l, ks_l, vs_l = [], [], [], [], []
    hg_s = None
    for layer in range(depth):
        g = norm_gains[layer]
        x = _ffn_half(x, g[0:1], g[1:2], w_ffn_in, w_ffn_out, layer, 0)
        if layer % N_MIXERS == 0:
            a = layer // N_MIXERS
            lb = lb_all[a][None, :]
            lbp = (jnp.log(lb), jnp.log1p(-lb), 1.0 - lb)
            ng = hgrn_norm_gain[a][None, :].astype(F32)
            o_p, s_p = _hgrn_fused(x, g[2:3], w_hgrn_in, a, B, L, lbp, ng)
            o_p = o_p.reshape(TP, -1)
            p_s = _proj(x[TP:], g[2:3], w_hgrn_in, a, (TS, HGRN_SAMPLE_PROJ_TN))
            ps = jnp.pad(p_s.reshape(SB, SL, -1), ((0, 0), (0, SAMPLE_PAD - SL), (0, 0)))
            o_s, hg_s = _hgrn_scan(ps.reshape(SB * SAMPLE_PAD, -1), SB, SAMPLE_PAD, SAMPLE_PAD, H, SL,
                                   lbp, ng, (state_hgrn, a), F32, prev=hg_s)
            o_s = o_s.reshape(SB, SAMPLE_PAD, -1)[:, :SL].reshape(TS, -1).astype(BF16)
            x = _outproj(o_p, o_s, w_hgrn_out, x, g[3:4], a)
            hg_p.append(s_p.astype(state_hgrn.dtype))
        else:
            bb = layer // N_MIXERS
            p = _proj(x, g[2:3], w_attn_in, bb, ATTN_PROJ_TILES, rope=rope, n_rope_cols=QW + KW)
            sinks = attn_sinks[bb].astype(F32)
            o_p = _attn_prompt(p, sinks, B, L, n_q, n_kv)
            pn = p[TP:].reshape(SB, SL, -1)
            ck = cache_k_win.reshape(-1, SB, buf, KW)
            cv = cache_v_win.reshape(-1, SB, buf, KW)
            o_s = _attn_sample(jnp.pad(pn, ((0, 0), (0, 8 - SL), (0, 0))), ck, cv, bb, sinks, n_q, n_kv, SL)
            x = _outproj(o_p, o_s[:, :SL].reshape(TS, QW).astype(BF16), w_attn_out, x, g[3:4], bb)
            tails = [p[(s + 1) * L - buf:(s + 1) * L, QW:] for s in range(B)]
            kp_l.append(jnp.stack([t[:, :KW] for t in tails]).reshape(B, buf, n_kv, HEAD_DIM))
            vp_l.append(jnp.stack([t[:, KW:] for t in tails]).reshape(B, buf, n_kv, HEAD_DIM))
            k_new = pn[:, :, QW:QW + KW].reshape(SB, SL, n_kv, HEAD_DIM)
            v_new = pn[:, :, QW + KW:].reshape(SB, SL, n_kv, HEAD_DIM)
            ks_l.append(k_new)
            vs_l.append(v_new)
        x = _ffn_half(x, g[4:5], g[5:6], w_ffn_in, w_ffn_out, layer, 1,
                      out_split=TP if layer == depth - 1 else None)

    yp = x[0].reshape(B, L, D)
    ys = x[1].reshape(SB, SL, D)
    k_win_s = jnp.concatenate([cache_k_win, jnp.stack(ks_l).astype(cache_k_win.dtype)], axis=2)[:, :, -buf:]
    v_win_s = jnp.concatenate([cache_v_win, jnp.stack(vs_l).astype(cache_v_win.dtype)], axis=2)[:, :, -buf:]
    return (yp, ys, jnp.stack(hg_p), hg_s.astype(state_hgrn.dtype), jnp.stack(kp_l), jnp.stack(vp_l),
            k_win_s, v_win_s)
```

```python
import functools

import jax
import jax.numpy as jnp
from jax import lax
from jax.experimental import pallas as pl
from jax.experimental.pallas import tpu as pltpu

F32 = jnp.float32
BF16 = jnp.bfloat16

NORM_EPS = 1e-6
HEAD_DIM = 64
HG_DK = 128
HG_DV = 128
HG_CHUNK = 64
WINDOW = 128
PAST_LEN = 16384
ROPE_THETA = 10000.0
N_MIXERS = 2

LANES = 128
SUBLANES = 8
MIB = 1024 * 1024
VMEM_BYTES_V7X = 64 * MIB
VMEM_HEADROOM = 6 * MIB
VMEM_MIN_REQUEST = 16 * MIB
LOOP_TEMPS_BYTES = 8 * MIB

FFN_TM, FFN_TF = 1040, 256
FFN_RC = 208
HGRN_SAMPLE_PROJ_TN = 1024
FUSED_HB, FUSED_NSEQ, FUSED_RB = 2, 2, 256
ATTN_PROJ_TILES = (2080, 512)
PROJ_X_DOUBLE_BUFFER_MAX_BYTES = 12 * MIB
ROPE_ROW_CHUNK = 208
OUT_TM = 512
SAMPLE_PAD = 16


def _vmem_limit(nbytes):
    return int(min(VMEM_BYTES_V7X - VMEM_HEADROOM, max(nbytes * 5 // 4, VMEM_MIN_REQUEST)))


def _rms(x, g):
    return x * lax.rsqrt(jnp.mean(x * x, axis=-1, keepdims=True) + NORM_EPS) * g


def _dot(a, b):
    return jnp.dot(a, b, preferred_element_type=F32)


def _dot_nt(a, b):
    return lax.dot_general(a, b, (((1,), (1,)), ((), ())), preferred_element_type=F32)


def _dot_tn(a, b):
    return lax.dot_general(a, b, (((0,), (0,)), ((), ())), preferred_element_type=F32)


def _ffn_kernel(*refs, nt, nj, tm, rc, in_split, out_split):
    refs = list(refs)
    x_refs = [refs.pop(0) for _ in range(1 if in_split is None else 2)]
    gpre_ref, gpost_ref, wg_ref, wu_ref, wo_ref = (refs.pop(0) for _ in range(5))
    y_refs = [refs.pop(0) for _ in range(1 if out_split is None else 2)]
    acc_ref, xn_ref, xc_ref, yc_ref, in_sem, out_sem = refs

    i = pl.program_id(0)
    j = pl.program_id(1)
    nch = tm // rc
    slot = i % 2
    pslot = 1 - slot
    k = j - 1
    gpre = gpre_ref[...]
    gpost = gpost_ref[...]

    def in_copy(row0, b):
        return pltpu.make_async_copy(x_refs[0].at[pl.ds(row0, rc)], xc_ref.at[b], in_sem.at[b])

    def out_copy(row0, b):
        return pltpu.make_async_copy(yc_ref.at[b], y_refs[0].at[pl.ds(row0, rc)], out_sem.at[b])

    def crossing_row(split):
        if split is None:
            return None
        assert split % rc != 0 and nt * tm - split < rc, "exactly one chunk must cross the split point"
        return (split // rc) * rc

    in_cross = crossing_row(in_split)
    out_cross = crossing_row(out_split)
    assert out_cross is None or out_cross >= (nt - 1) * tm, "only the last row tile may cross the output split"

    def in_copies_crossing(b):
        na = in_split - in_cross
        return [pltpu.make_async_copy(x_refs[0].at[pl.ds(in_cross, na)], xc_ref.at[b, pl.ds(0, na)], in_sem.at[b]),
                pltpu.make_async_copy(x_refs[1].at[pl.ds(0, rc - na)], xc_ref.at[b, pl.ds(na, rc - na)],
                                      in_sem.at[b])]

    def out_copies_crossing(b):
        na = out_split - out_cross
        return [pltpu.make_async_copy(yc_ref.at[b, pl.ds(0, na)], y_refs[0].at[pl.ds(out_cross, na)], out_sem.at[b]),
                pltpu.make_async_copy(yc_ref.at[b, pl.ds(na, rc - na)], y_refs[1].at[pl.ds(0, rc - na)],
                                      out_sem.at[b])]

    def static_in_copies(row0, b):
        return in_copies_crossing(b) if row0 == in_cross else [in_copy(row0, b)]

    def static_out_copies(row0, b):
        return out_copies_crossing(b) if row0 == out_cross else [out_copy(row0, b)]

    def task_row(t):
        return jnp.where(t < nch, (i - 1) * tm + t * rc, (i + 1) * tm + (t - nch) * rc)

    def task_live(t):
        return (t >= 0) & (t < 2 * nch) & jnp.where(t < nch, i >= 1, i <= nt - 2)

    def task_crosses(t):
        return False if in_cross is None else task_row(t) == in_cross

    @pl.when((i == 0) & (j == 0))
    def _():
        for c in range(nch):
            cps = static_in_copies(c * rc, c % 2)
            for cp in cps:
                cp.start()
            for cp in cps:
                cp.wait()
            xn_ref[0, c * rc:(c + 1) * rc] = _rms(xc_ref[c % 2], gpre).astype(BF16)

    @pl.when(task_live(k) & jnp.logical_not(task_crosses(k)))
    def _():
        in_copy(0, k & 1).wait()

    @pl.when(task_live(k) & (k >= 2) & (k < nch))
    def _():
        out_copy(0, k & 1).wait()

    @pl.when((j == nch + 1) & (i >= 1))
    def _():
        out_copy(0, 0).wait()
        out_copy(0, 1).wait()

    @pl.when(task_live(j) & jnp.logical_not(task_crosses(j)))
    def _():
        in_copy(task_row(j), j & 1).start()

    if in_cross is not None:
        @pl.when(task_live(k) & task_crosses(k))
        def _():
            for cp in in_copies_crossing(k & 1):
                cp.wait()

        @pl.when(task_live(j) & task_crosses(j))
        def _():
            for cp in in_copies_crossing(j & 1):
                cp.start()

    def accumulate(first):
        xn = xn_ref[slot]
        hg = _dot(xn, wg_ref[...].astype(BF16))
        hu = _dot(xn, wu_ref[...].astype(BF16))
        act = (hg * jax.nn.sigmoid(hg) * hu).astype(BF16)
        contrib = _dot(act, wo_ref[...].astype(BF16))
        if first:
            acc_ref[slot] = contrib
        else:
            acc_ref[slot] += contrib

    is_epi = (k >= 0) & (k < nch) & (i >= 1)
    is_pro = (k >= nch) & (k < 2 * nch) & (i <= nt - 2)

    @pl.when(j == 0)
    def _():
        accumulate(True)

    @pl.when(is_epi)
    def _():
        accumulate(False)
        b = k & 1
        rows = pl.ds(pl.multiple_of(k * rc, rc), rc)
        yc_ref[b] = xc_ref[b] + 0.5 * _rms(acc_ref[pslot, rows], gpost)
        out_copy((i - 1) * tm + k * rc, b).start()

    @pl.when(is_pro)
    def _():
        accumulate(False)
        rows = pl.ds(pl.multiple_of((k - nch) * rc, rc), rc)
        xn_ref[pslot, rows] = _rms(xc_ref[k & 1], gpre).astype(BF16)

    @pl.when((j >= 1) & jnp.logical_not(is_epi) & jnp.logical_not(is_pro))
    def _():
        accumulate(False)

    @pl.when((i == nt - 1) & (j == nj - 1))
    def _():
        last = (nt - 1) % 2
        for c in range(nch):
            b = c % 2
            row0 = (nt - 1) * tm + c * rc
            cps = static_in_copies(row0, b)
            for cp in cps:
                cp.start()
            for cp in cps:
                cp.wait()
            yc_ref[b] = xc_ref[b] + 0.5 * _rms(acc_ref[last, c * rc:(c + 1) * rc], gpost)
            ocs = static_out_copies(row0, b)
            for oc in ocs:
                oc.start()
            for oc in ocs:
                oc.wait()


def _ffn_half(x, g_pre, g_post, w_in, w_out, layer, half, out_split=None):
    xs = x if isinstance(x, (tuple, list)) else (x,)
    in_split = xs[0].shape[0] if len(xs) == 2 else None
    T = sum(a.shape[0] for a in xs)
    D = xs[0].shape[1]
    F = w_out.shape[2]
    TM, TF, RC = FFN_TM, FFN_TF, FFN_RC
    nj = F // TF
    nt = T // TM
    assert T % TM == 0 and F % TF == 0 and TM % RC == 0 and TM // RC >= 2 and nj >= 2 * (TM // RC) + 2
    est = (2 * TM * D * (4 + 2) + 4 * RC * D * 4 + 2 * 3 * D * TF * 4
           + 3 * D * TF * 2 + 4 * TM * TF * 4 + TM * D * 4)
    any_spec = pl.BlockSpec(memory_space=pl.ANY)
    if out_split is None:
        out_specs, out_shape = any_spec, jax.ShapeDtypeStruct((T, D), F32)
    else:
        out_specs = [any_spec, any_spec]
        out_shape = [jax.ShapeDtypeStruct((out_split, D), F32), jax.ShapeDtypeStruct((T - out_split, D), F32)]
    return pl.pallas_call(
        functools.partial(_ffn_kernel, nt=nt, nj=nj, tm=TM, rc=RC, in_split=in_split, out_split=out_split),
        grid=(nt, nj),
        in_specs=[any_spec] * len(xs) + [
            pl.BlockSpec((1, D), lambda i, j: (0, 0)),
            pl.BlockSpec((1, D), lambda i, j: (0, 0)),
            pl.BlockSpec((None, None, D, TF), lambda i, j: (layer, half, 0, j)),
            pl.BlockSpec((None, None, D, TF), lambda i, j: (layer, half, 0, nj + j)),
            pl.BlockSpec((None, None, TF, D), lambda i, j: (layer, half, j, 0)),
        ],
        out_specs=out_specs,
        out_shape=out_shape,
        scratch_shapes=[pltpu.VMEM((2, TM, D), F32), pltpu.VMEM((2, TM, D), BF16),
                        pltpu.VMEM((2, RC, D), F32), pltpu.VMEM((2, RC, D), F32),
                        pltpu.SemaphoreType.DMA((2,)), pltpu.SemaphoreType.DMA((2,))],
        compiler_params=pltpu.CompilerParams(
            dimension_semantics=("arbitrary", "arbitrary"),
            vmem_limit_bytes=_vmem_limit(est)),
        name="ffn_half",
    )(*xs, g_pre, g_post, w_in, w_in, w_out)


def _rope_tile(y, cos, sin_signed):
    lane = lax.broadcasted_iota(jnp.int32, y.shape, 1)
    first_half = (lane & (HEAD_DIM - 1)) < (HEAD_DIM // 2)
    rot = jnp.where(first_half,
                    pltpu.roll(y, LANES - HEAD_DIM // 2, 1),
                    pltpu.roll(y, HEAD_DIM // 2, 1))
    return y * cos + rot * sin_signed


def _proj_kernel(x_ref, g_ref, w_ref, o_ref, xn_ref):
    @pl.when(pl.program_id(1) == 0)
    def _():
        xn_ref[...] = _rms(x_ref[...], g_ref[...]).astype(BF16)

    o_ref[...] = _dot(xn_ref[...], w_ref[...].astype(BF16))


def _proj_rope_kernel(x_ref, g_ref, w_ref, cos_ref, sin_ref, o_ref, xn_ref, *, row_chunk):
    @pl.when(pl.program_id(1) == 0)
    def _():
        xn_ref[...] = _rms(x_ref[...], g_ref[...]).astype(BF16)

    wb = w_ref[...].astype(BF16)
    for r0 in range(0, xn_ref.shape[0], row_chunk):
        rows = slice(r0, r0 + row_chunk)
        y = _dot(xn_ref[rows], wb)
        cos = cos_ref[rows]
        sin = sin_ref[rows]
        for c in range(y.shape[1] // LANES):
            cols = slice(c * LANES, (c + 1) * LANES)
            o_ref[rows, cols] = _rope_tile(y[:, cols], cos, sin)


def _proj(x, g, w, layer, tiles, rope=None, n_rope_cols=0):
    T, D = x.shape
    N = w.shape[2]
    TM, TN = tiles
    assert T % TM == 0 and N % TN == 0 and n_rope_cols % TN == 0
    x_buffers = 1 if TM * D * 4 > PROJ_X_DOUBLE_BUFFER_MAX_BYTES else 2
    est = x_buffers * TM * D * 4 + TM * D * 2 + 2 * D * TN * 4 + D * TN * 2 + 3 * TM * TN * 4
    in_specs = [
        pl.BlockSpec((TM, D), lambda i, j: (i, 0), pipeline_mode=pl.Buffered(x_buffers)),
        pl.BlockSpec((1, D), lambda i, j: (0, 0)),
        pl.BlockSpec((None, D, TN), lambda i, j: (layer, 0, j)),
    ]
    args = [x, g, w]
    if rope is None:
        body = _proj_kernel
    else:
        assert TM % ROPE_ROW_CHUNK == 0
        body = functools.partial(_proj_rope_kernel, row_chunk=ROPE_ROW_CHUNK)
        n_rope_blocks = n_rope_cols // TN
        table = pl.BlockSpec((None, TM, LANES), lambda i, j: (jnp.where(j < n_rope_blocks, 0, 1), i, 0))
        in_specs += [table, table]
        args += list(rope)
    return pl.pallas_call(
        body,
        grid=(T // TM, N // TN),
        in_specs=in_specs,
        out_specs=pl.BlockSpec((TM, TN), lambda i, j: (i, j)),
        out_shape=jax.ShapeDtypeStruct((T, N), F32),
        scratch_shapes=[pltpu.VMEM((TM, D), BF16)],
        compiler_params=pltpu.CompilerParams(
            dimension_semantics=("parallel", "arbitrary"),
            vmem_limit_bytes=_vmem_limit(est)),
        name="mixer_in_proj",
    )(*args)


def _outproj_kernel(op_ref, os_ref, w_ref, x_ref, g_ref, y_ref, wbf_ref, *, n_prompt_tiles, ts):
    i = pl.program_id(0)

    @pl.when(i == 0)
    def _():
        wbf_ref[...] = w_ref[...].astype(BF16)

    @pl.when(i < n_prompt_tiles)
    def _():
        y_ref[...] = x_ref[...] + _rms(_dot(op_ref[...], wbf_ref[...]), g_ref[...])

    @pl.when(i == n_prompt_tiles)
    def _():
        y_ref[0:ts] = x_ref[0:ts] + _rms(_dot(os_ref[...], wbf_ref[...]), g_ref[...])


def _outproj(o_prompt, o_sample, w, x, g, layer):
    T, D = x.shape
    TP, K = o_prompt.shape
    TS = o_sample.shape[0]
    TM = OUT_TM
    assert TP % TM == 0 and TS <= TM and TS % SUBLANES == 0 and TP + TS == T
    npt = TP // TM
    est = K * D * (4 + 2) + 2 * TM * K * 2 + 2 * TS * K * 2 + 5 * TM * D * 4
    return pl.pallas_call(
        functools.partial(_outproj_kernel, n_prompt_tiles=npt, ts=TS),
        grid=(npt + 1,),
        in_specs=[
            pl.BlockSpec((TM, K), lambda i: (jnp.minimum(i, npt - 1), 0)),
            pl.BlockSpec((TS, K), lambda i: (0, 0)),
            pl.BlockSpec((None, K, D), lambda i: (layer, 0, 0), pipeline_mode=pl.Buffered(1)),
            pl.BlockSpec((TM, D), lambda i: (i, 0)),
            pl.BlockSpec((1, D), lambda i: (0, 0)),
        ],
        out_specs=pl.BlockSpec((TM, D), lambda i: (i, 0)),
        out_shape=jax.ShapeDtypeStruct((T, D), F32),
        scratch_shapes=[pltpu.VMEM((K, D), BF16)],
        compiler_params=pltpu.CompilerParams(
            dimension_semantics=("arbitrary",),
            vmem_limit_bytes=_vmem_limit(est)),
        name="mixer_out_proj",
    )(o_prompt, o_sample, w, x, g)


def _tiles(x):
    return [x[SUBLANES * j:SUBLANES * (j + 1)] for j in range(x.shape[0] // SUBLANES)]


def _bcast_row(t, i):
    return jnp.broadcast_to(t[i:i + 1], t.shape)


def _tile_prefix(t, sub):
    p = t + jnp.where(sub >= 1, pltpu.roll(t, 1, 0), 0.0)
    p = p + jnp.where(sub >= 2, pltpu.roll(p, 2, 0), 0.0)
    return p + jnp.where(sub >= 4, pltpu.roll(p, 4, 0), 0.0)


def _hgrn_gates(fz, loglb, log1m, onem, valid):
    e = jnp.exp(-jnp.abs(fz))
    lsig = jnp.minimum(fz, 0.0) - jnp.log(1.0 + e)
    ct = log1m + lsig
    g = jnp.maximum(loglb, ct) + jnp.log(1.0 + jnp.exp(-jnp.abs(loglb - ct)))
    r = 1.0 / (1.0 + e)
    k = onem * jnp.where(fz > 0.0, e * r, r)
    if valid is not None:
        g = jnp.where(valid, g, 0.0)
        k = jnp.where(valid, k, 0.0)
    return g, k


def _hgrn_level_operands(m, qt, kt, gt, bt, sub):
    nv = len(qt)
    zero = jnp.zeros_like(qt[0])
    qm, km = [], []
    if m >= SUBLANES:
        mv = m // SUBLANES
        for j in range(nv):
            jr = (j // (2 * mv)) * (2 * mv) + mv - 1
            ref = _bcast_row(bt[jr], SUBLANES - 1)
            if j % (2 * mv) >= mv:
                qm.append(qt[j] * jnp.exp(bt[j] - ref))
                km.append(zero)
            else:
                qm.append(zero)
                km.append(kt[j] * jnp.exp(ref - bt[j]))
        return qm, km
    upper = (sub & m) != 0
    for j in range(nv):
        if m == 1:
            e = jnp.exp(gt[j])
            qm.append(jnp.where(upper, qt[j] * e, 0.0))
            km.append(jnp.where(upper, 0.0, kt[j]))
            continue
        if m == 4:
            ref = _bcast_row(bt[j], 3)
        else:
            ref = jnp.where(sub < 4, _bcast_row(bt[j], 1), _bcast_row(bt[j], 5))
        e = jnp.exp(-jnp.abs(bt[j] - ref))
        qm.append(jnp.where(upper, qt[j] * e, 0.0))
        km.append(jnp.where(upper, 0.0, kt[j] * e))
    return qm, km


def _hgrn_chunk_group(loads, params, sts, *, C, valid_len):
    nh = len(loads)
    sub = lax.broadcasted_iota(jnp.int32, (SUBLANES, LANES), 0)
    valid = None
    if valid_len < C:
        valid = lax.broadcasted_iota(jnp.int32, (C, LANES), 0) < valid_len
    ti = lax.broadcasted_iota(jnp.int32, (C, C), 0)
    si = lax.broadcasted_iota(jnp.int32, (C, C), 1)

    gk = [_hgrn_gates(loads[h][1], params[h][0], params[h][1], params[h][2], valid) for h in range(nh)]

    gts, bts = [], []
    for h in range(nh):
        gt = _tiles(gk[h][0])
        bt = []
        for j, t in enumerate(gt):
            p = _tile_prefix(t, sub)
            bt.append(p if j == 0 else p + _bcast_row(bt[j - 1], SUBLANES - 1))
        gts.append(gt)
        bts.append(bt)

    scores = []
    for h in range(nh):
        q, k = loads[h][0], gk[h][1]
        qt, kt = _tiles(q), _tiles(k)
        a = jnp.where(ti == si, _dot_nt(q.astype(BF16), k.astype(BF16)), 0.0)
        m = C // 2
        while m >= 1:
            qm, km = _hgrn_level_operands(m, qt, kt, gts[h], bts[h], sub)
            am = _dot_nt(jnp.concatenate(qm, axis=0).astype(BF16), jnp.concatenate(km, axis=0).astype(BF16))
            if 2 * m < C:
                am = jnp.where((ti & (-2 * m)) == (si & (-2 * m)), am, 0.0)
            a = a + am
            m //= 2
        scores.append(a)

    outs = []
    for h in range(nh):
        q, _, v, gz = loads[h]
        k = gk[h][1]
        ng = params[h][3]
        st = sts[h]
        b = jnp.concatenate(bts[h], axis=0)
        o = _dot(scores[h].astype(BF16), v.astype(BF16)) + _dot_nt((q * jnp.exp(b)).astype(BF16), st.astype(BF16))
        bl = b[C - 1:C]
        kb = (k * jnp.exp(bl - b)).astype(BF16)
        st_new = st * jnp.exp(bl) + _dot_tn(v.astype(BF16), kb)
        on = o * lax.rsqrt(jnp.mean(o * o, axis=-1, keepdims=True) + NORM_EPS) * ng
        outs.append((on * (gz * jax.nn.sigmoid(gz)), st_new))
    return outs


def _hgrn_kernel(*refs, C, NC, HB, valid_len, has_s0, n_prev):
    refs = list(refs)
    q_ref, f_ref, i_ref, g_ref, loglb_ref, log1m_ref, onem_ref, ng_ref = (refs.pop(0) for _ in range(8))
    s0_ref = refs.pop(0) if has_s0 else None
    prev_ref = refs.pop(0) if n_prev else None
    o_ref, s_ref, st_ref = refs

    for h in range(HB):
        if has_s0:
            st_ref[h] = s0_ref[0, h].T
        else:
            st_ref[h] = jnp.zeros((HG_DV, HG_DK), F32)

    def body(c, carry):
        rows = pl.ds(pl.multiple_of(c * C, C), C)
        cols = [slice(h * LANES, (h + 1) * LANES) for h in range(HB)]
        loads = [(q_ref[rows, cs], f_ref[rows, cs], i_ref[rows, cs], g_ref[rows, cs]) for cs in cols]
        params = [(loglb_ref[:, cs], log1m_ref[:, cs], onem_ref[:, cs], ng_ref[:, cs]) for cs in cols]
        outs = _hgrn_chunk_group(loads, params, [st_ref[h] for h in range(HB)], C=C, valid_len=valid_len)
        for h in range(HB):
            st_ref[h] = outs[h][1]
            o_ref[rows, cols[h]] = outs[h][0].astype(o_ref.dtype)
        return carry

    if NC == 1:
        body(0, 0)
    else:
        lax.fori_loop(0, NC, body, 0)

    for layer in range(n_prev):
        s_ref[layer] = prev_ref[layer]
    for h in range(HB):
        s_ref[n_prev, 0, h] = st_ref[h].T


def _hgrn_scan(p, nseq, L, C, HB, valid_len, lbp, ng, s0, out_dtype, prev=None):
    H = ng.shape[1] // LANES
    nhb = H // HB
    W = HB * LANES
    NC = L // C
    loglb, log1m, onem = lbp
    has_s0 = s0 is not None

    def sect(s):
        return pl.BlockSpec((L, W), lambda b, hb, s=s: (b, s * nhb + hb))

    vec = pl.BlockSpec((1, W), lambda b, hb: (0, hb))
    in_specs = [sect(0), sect(1), sect(2), sect(3), vec, vec, vec, vec]
    args = [p, p, p, p, loglb, log1m, onem, ng]
    if has_s0:
        s0_all, s0_layer = s0
        in_specs.append(pl.BlockSpec((None, 1, HB, HG_DK, HG_DV), lambda b, hb: (s0_layer, b, hb, 0, 0)))
        args.append(s0_all)
    n_prev = 0 if prev is None else prev.shape[0]
    if n_prev:
        in_specs.append(pl.BlockSpec((n_prev, 1, HB, HG_DK, HG_DV), lambda b, hb: (0, b, hb, 0, 0)))
        args.append(prev)
    est = (2 * 4 * L * W * 4 + 2 * L * W * 4 + (6 + 4 * (2 * n_prev + 1)) * HB * HG_DK * HG_DV * 4
           + LOOP_TEMPS_BYTES)
    return pl.pallas_call(
        functools.partial(_hgrn_kernel, C=C, NC=NC, HB=HB, valid_len=valid_len, has_s0=has_s0, n_prev=n_prev),
        grid=(nseq, nhb),
        in_specs=in_specs,
        out_specs=[pl.BlockSpec((L, W), lambda b, hb: (b, hb)),
                   pl.BlockSpec((n_prev + 1, 1, HB, HG_DK, HG_DV), lambda b, hb: (0, b, hb, 0, 0))],
        out_shape=[jax.ShapeDtypeStruct((nseq * L, H * LANES), out_dtype),
                   jax.ShapeDtypeStruct((n_prev + 1, nseq, H, HG_DK, HG_DV), F32)],
        scratch_shapes=[pltpu.VMEM((HB, HG_DV, HG_DK), F32)],
        compiler_params=pltpu.CompilerParams(
            dimension_semantics=("parallel", "parallel"),
            vmem_limit_bytes=_vmem_limit(est)),
        name="hgrn_scan",
    )(*args)


def _hgrn_fused_kernel(x_hbm, g_ref, wq_ref, wf_ref, wi_ref, wg_ref, loglb_ref, log1m_ref, onem_ref, ng_ref,
                       o_ref, s_ref, xn_ref, xc_ref, proj_ref, st_ref, in_sem, *, L, C, RB, NSEQ, HB):
    sg = pl.program_id(0)
    hb = pl.program_id(1)
    D = xn_ref.shape[1]
    WC = HB * LANES
    NB = L // RB
    CPB = RB // C
    w_refs = (wq_ref, wf_ref, wi_ref, wg_ref)
    assert CPB == len(w_refs)

    @pl.when(hb == 0)
    def _():
        g = g_ref[...]
        nchunk = NSEQ * NB

        def copy(c):
            return pltpu.make_async_copy(x_hbm.at[pl.ds(sg * (NSEQ * L) + c * RB, RB)],
                                         xc_ref.at[c % 2], in_sem.at[c % 2])
        copy(0).start()
        for c in range(nchunk):
            copy(c).wait()
            if c + 1 < nchunk:
                copy(c + 1).start()
            xn_ref[c * RB:(c + 1) * RB] = _rms(xc_ref[c % 2], g).astype(BF16)

    def project(rb, section, slot):
        rows = [xn_ref[pl.ds(pl.multiple_of(s * L + rb * RB, RB), RB)] for s in range(NSEQ)]
        y = _dot(jnp.concatenate(rows, axis=0), w_refs[section][...].astype(BF16))
        proj_ref[slot, :, section * WC:(section + 1) * WC] = y

    for ch in range(NSEQ * HB):
        st_ref[ch] = jnp.zeros((HG_DV, HG_DK), F32)
    for section in range(CPB):
        project(0, section, 0)

    params = [(loglb_ref[:, h * LANES:(h + 1) * LANES], log1m_ref[:, h * LANES:(h + 1) * LANES],
               onem_ref[:, h * LANES:(h + 1) * LANES], ng_ref[:, h * LANES:(h + 1) * LANES])
              for _ in range(NSEQ) for h in range(HB)]

    def block(rb, project_next):
        slot = rb % 2
        for ci in range(CPB):
            loads = []
            for s in range(NSEQ):
                rows = slice(s * RB + ci * C, s * RB + (ci + 1) * C)
                for h in range(HB):
                    loads.append(tuple(proj_ref[slot, rows, sec * WC + h * LANES:sec * WC + (h + 1) * LANES]
                                       for sec in range(4)))
            if project_next:
                project(rb + 1, ci, 1 - slot)
            outs = _hgrn_chunk_group(loads, params, [st_ref[ch] for ch in range(NSEQ * HB)], C=C, valid_len=C)
            orow = pl.ds(pl.multiple_of(rb * RB + ci * C, C), C)
            for s in range(NSEQ):
                for h in range(HB):
                    ch = s * HB + h
                    st_ref[ch] = outs[ch][1]
                    o_ref[s, orow, h * LANES:(h + 1) * LANES] = outs[ch][0].astype(o_ref.dtype)

    def body(rb, carry):
        block(rb, True)
        return carry

    lax.fori_loop(0, NB - 1, body, 0)
    block(NB - 1, False)

    for s in range(NSEQ):
        for h in range(HB):
            s_ref[s, h] = st_ref[s * HB + h].T


def _hgrn_fused(x, g, w, layer, B, L, lbp, ng):
    D = x.shape[1]
    H = ng.shape[1] // LANES
    HB, NSEQ, RB, C = FUSED_HB, FUSED_NSEQ, FUSED_RB, HG_CHUNK
    WC = HB * LANES
    nhb = H // HB
    assert B % NSEQ == 0 and H % HB == 0 and L % RB == 0 and RB // C == 4
    loglb, log1m, onem = lbp

    def sect(s):
        return pl.BlockSpec((None, D, WC), lambda sg, hb, s=s: (layer, 0, s * nhb + hb))

    vec = pl.BlockSpec((1, WC), lambda sg, hb: (0, hb))
    est = (NSEQ * L * D * 2 + 2 * RB * D * 4 + 2 * NSEQ * RB * 4 * WC * 4 + 2 * 4 * D * WC * 4
           + 2 * NSEQ * L * WC * 2 + LOOP_TEMPS_BYTES)
    return pl.pallas_call(
        functools.partial(_hgrn_fused_kernel, L=L, C=C, RB=RB, NSEQ=NSEQ, HB=HB),
        grid=(B // NSEQ, nhb),
        in_specs=[pl.BlockSpec(memory_space=pl.ANY), pl.BlockSpec((1, D), lambda sg, hb: (0, 0)),
                  sect(0), sect(1), sect(2), sect(3), vec, vec, vec, vec],
        out_specs=[pl.BlockSpec((NSEQ, L, WC), lambda sg, hb: (sg, 0, hb)),
                   pl.BlockSpec((NSEQ, HB, HG_DK, HG_DV), lambda sg, hb: (sg, hb, 0, 0))],
        out_shape=[jax.ShapeDtypeStruct((B, L, H * LANES), BF16),
                   jax.ShapeDtypeStruct((B, H, HG_DK, HG_DV), F32)],
        scratch_shapes=[pltpu.VMEM((NSEQ * L, D), BF16), pltpu.VMEM((2, RB, D), F32),
                        pltpu.VMEM((2, NSEQ * RB, 4 * WC), F32), pltpu.VMEM((NSEQ * HB, HG_DV, HG_DK), F32),
                        pltpu.SemaphoreType.DMA((2,))],
        compiler_params=pltpu.CompilerParams(
            dimension_semantics=("arbitrary", "arbitrary"),
            vmem_limit_bytes=_vmem_limit(est)),
        name="hgrn_fused",
    )(x, g, w, w, w, w, loglb, log1m, onem, ng)


def _sink_softmax(scores, sink):
    m = sink
    for s in scores:
        m = jnp.maximum(m, jnp.max(s, axis=-1, keepdims=True))
    ps = [jnp.exp(s - m) for s in scores]
    denom = jnp.exp(sink - m)
    for p in ps:
        denom = denom + jnp.sum(p, axis=-1, keepdims=True)
    return [p / denom for p in ps]


def _attn_prompt_kernel(sinks_ref, q_ref, kc_ref, kp_ref, vc_ref, vp_ref, o_ref, *, n_kv, group):
    i = pl.program_id(1)
    W = WINDOW
    HD = HEAD_DIM
    j = lax.broadcasted_iota(jnp.int32, (2 * W, W), 0)
    r = lax.broadcasted_iota(jnp.int32, (2 * W, W), 1)
    mask = (j >= jnp.maximum(r, jnp.where(i > 0, 0, W))) & (j <= r + W)
    scale = HD ** -0.5
    zeros = jnp.zeros((HD, W), F32)

    s_all, vt_all = [], []
    for c in range(n_kv // 2):
        cs = slice(c * LANES, (c + 1) * LANES)
        ktile = jnp.concatenate([kp_ref[:, cs], kc_ref[:, cs]], axis=0).astype(BF16)
        vt_all.append(jnp.concatenate([vp_ref[:, cs], vc_ref[:, cs]], axis=0).T.astype(BF16))
        for par in range(2):
            kh = 2 * c + par
            rhs = []
            for pr in range(group // 2):
                c0 = (kh * group + 2 * pr) * HD
                pt = (q_ref[:, c0:c0 + LANES] * scale).T
                for half in range(2):
                    blk = pt[half * HD:(half + 1) * HD]
                    rhs.append(jnp.concatenate([blk, zeros] if par == 0 else [zeros, blk], axis=0))
            s_all.append(_dot(ktile, jnp.concatenate(rhs, axis=1).astype(BF16)))

    p_all, inv_all = [], []
    for kh in range(n_kv):
        pn, inv = [], []
        for hh in range(group):
            sink = sinks_ref[kh * group + hh]
            sm = jnp.where(mask, s_all[kh][:, hh * W:(hh + 1) * W], -jnp.inf)
            mx = jnp.maximum(jnp.max(sm, axis=0, keepdims=True), sink)
            p = jnp.exp(sm - mx)
            inv.append(1.0 / (jnp.sum(p, axis=0, keepdims=True) + jnp.exp(sink - mx)))
            pn.append(p.astype(BF16))
        p_all.append(jnp.concatenate(pn, axis=1))
        inv_all.append(jnp.concatenate(inv, axis=1))

    for kh in range(n_kv):
        par = kh % 2
        o_t = _dot(vt_all[kh // 2], p_all[kh])[par * HD:(par + 1) * HD] * inv_all[kh]
        for pr in range(group // 2):
            pair_t = jnp.concatenate([o_t[:, (2 * pr) * W:(2 * pr + 1) * W],
                                      o_t[:, (2 * pr + 1) * W:(2 * pr + 2) * W]], axis=0)
            c0 = (kh * group + 2 * pr) * HD
            o_ref[:, c0:c0 + LANES] = pair_t.T.astype(o_ref.dtype)


def _attn_prompt(p, sinks, B, L, n_q, n_kv):
    W = WINDOW
    nb = L // W
    QW = n_q * HEAD_DIM
    KW = n_kv * HEAD_DIM
    assert 2 * HEAD_DIM == LANES and W == LANES and L % W == 0
    assert n_kv % 2 == 0 and (n_q // n_kv) % 2 == 0 and QW % KW == 0
    kblk = QW // KW
    vblk = (QW + KW) // KW
    n_heads_scores = n_q * 2 * W * W * 4
    est = 2 * (W * QW * 4 + 4 * W * KW * 4 + W * QW * 2) + 2 * n_heads_scores + LOOP_TEMPS_BYTES
    return pl.pallas_call(
        functools.partial(_attn_prompt_kernel, n_kv=n_kv, group=n_q // n_kv),
        grid=(B, nb),
        in_specs=[
            pl.BlockSpec(memory_space=pltpu.SMEM),
            pl.BlockSpec((W, QW), lambda b, i: (b * nb + i, 0)),
            pl.BlockSpec((W, KW), lambda b, i: (b * nb + i, kblk)),
            pl.BlockSpec((W, KW), lambda b, i: (b * nb + jnp.maximum(i - 1, 0), kblk)),
            pl.BlockSpec((W, KW), lambda b, i: (b * nb + i, vblk)),
            pl.BlockSpec((W, KW), lambda b, i: (b * nb + jnp.maximum(i - 1, 0), vblk)),
        ],
        out_specs=pl.BlockSpec((W, QW), lambda b, i: (b * nb + i, 0)),
        out_shape=jax.ShapeDtypeStruct((B * L, QW), BF16),
        compiler_params=pltpu.CompilerParams(
            dimension_semantics=("parallel", "parallel"),
            vmem_limit_bytes=_vmem_limit(est)),
        name="swa_prompt",
    )(sinks, p, p, p, p, p)


def _attn_sample_kernel(sinks_ref, qkv_ref, ck_ref, cv_ref, o_ref, *, n_q, n_kv, n_new):
    R = qkv_ref.shape[1]
    W = WINDOW
    HD = HEAD_DIM
    group = n_q // n_kv
    npair = group // 2
    QW = n_q * HD
    KW = n_kv * HD
    RR = npair * R
    scale = HD ** -0.5
    row = lax.broadcasted_iota(jnp.int32, (RR, 2 * W), 0)
    r = row & (R - 1)
    j = lax.broadcasted_iota(jnp.int32, (RR, 2 * W), 1)
    mask = (j >= r) & (j <= r + W) & (j < W + n_new)
    pair_of_row = lax.broadcasted_iota(jnp.int32, (RR, 1), 0) // R
    first = lax.broadcasted_iota(jnp.int32, (2 * W, LANES), 1) < HD
    zpad = jnp.zeros((W - R, LANES), F32)

    s_all, v_all = [], []
    for c in range(n_kv // 2):
        cs = slice(c * LANES, (c + 1) * LANES)
        kn = qkv_ref[0, :, QW + c * LANES:QW + (c + 1) * LANES]
        vn = qkv_ref[0, :, QW + KW + c * LANES:QW + KW + (c + 1) * LANES]
        ktile = jnp.concatenate([ck_ref[0, :, cs], kn, zpad], axis=0)
        vtile = jnp.concatenate([cv_ref[0, :, cs], vn, zpad], axis=0)
        kswap = pltpu.roll(ktile, HD, 1)
        vswap = pltpu.roll(vtile, HD, 1)
        for par in range(2):
            kh = 2 * c + par
            k_lo, k_hi = (ktile, kswap) if par == 0 else (kswap, ktile)
            v_lo, v_hi = (vtile, vswap) if par == 0 else (vswap, vtile)
            kcat = jnp.concatenate([jnp.where(first, k_lo, 0.0), jnp.where(first, 0.0, k_hi)], axis=0)
            vcat = jnp.concatenate([jnp.where(first, v_lo, 0.0), jnp.where(first, 0.0, v_hi)], axis=0)
            qs = jnp.concatenate(
                [qkv_ref[0, :, (kh * group + 2 * pr) * HD:(kh * group + 2 * pr) * HD + LANES]
                 for pr in range(npair)], axis=0) * scale
            s_all.append(_dot_nt(qs.astype(BF16), kcat.astype(BF16)))
            v_all.append(vcat.astype(BF16))

    p_all = []
    for kh in range(n_kv):
        halves = []
        for hf in range(2):
            sink = jnp.full((RR, 1), sinks_ref[kh * group + hf], F32)
            for pr in range(1, npair):
                sink = jnp.where(pair_of_row == pr, sinks_ref[kh * group + 2 * pr + hf], sink)
            sm = jnp.where(mask, s_all[kh][:, hf * 2 * W:(hf + 1) * 2 * W], -jnp.inf)
            (p,) = _sink_softmax([sm], sink)
            halves.append(p.astype(BF16))
        p_all.append(jnp.concatenate(halves, axis=1))

    for kh in range(n_kv):
        o = _dot(p_all[kh], v_all[kh])
        for pr in range(npair):
            c0 = (kh * group + 2 * pr) * HD
            o_ref[0, :, c0:c0 + LANES] = o[pr * R:(pr + 1) * R]


def _attn_sample(qkv, ck, cv, layer, sinks, n_q, n_kv, n_new):
    B, R, NW = qkv.shape
    assert R & (R - 1) == 0 and n_new <= R and 2 * HEAD_DIM == LANES and ck.shape[2] == WINDOW
    QW = n_q * HEAD_DIM
    KW = n_kv * HEAD_DIM
    return pl.pallas_call(
        functools.partial(_attn_sample_kernel, n_q=n_q, n_kv=n_kv, n_new=n_new),
        grid=(B,),
        in_specs=[
            pl.BlockSpec(memory_space=pltpu.SMEM),
            pl.BlockSpec((1, R, NW), lambda b: (b, 0, 0)),
            pl.BlockSpec((None, 1, WINDOW, KW), lambda b: (layer, b, 0, 0)),
            pl.BlockSpec((None, 1, WINDOW, KW), lambda b: (layer, b, 0, 0)),
        ],
        out_specs=pl.BlockSpec((1, R, QW), lambda b: (b, 0, 0)),
        out_shape=jax.ShapeDtypeStruct((B, R, QW), F32),
        compiler_params=pltpu.CompilerParams(dimension_semantics=("parallel",)),
        name="swa_sample",
    )(sinks, qkv, ck, cv)


def _rope_tables(pos):
    half = HEAD_DIM // 2
    inv = jnp.power(ROPE_THETA, -jnp.arange(half, dtype=F32) * (2.0 / HEAD_DIM))
    ang = pos.astype(F32)[:, None] * inv[None, :]
    cos = jnp.cos(ang)
    sin = jnp.sin(ang)
    reps = LANES // HEAD_DIM
    cos_t = jnp.tile(jnp.concatenate([cos, cos], axis=1), (1, reps))
    sin_t = jnp.tile(jnp.concatenate([-sin, sin], axis=1), (1, reps))
    return (jnp.stack([cos_t, jnp.ones_like(cos_t)]), jnp.stack([sin_t, jnp.zeros_like(sin_t)]))


def kernel(x_prompt, x_sample, state_hgrn, cache_k_win, cache_v_win, norm_gains, w_ffn_in, w_ffn_out,
           w_hgrn_in, hgrn_lb_logits, hgrn_norm_gain, w_hgrn_out, w_attn_in, attn_sinks, w_attn_out):
    B, L, D = x_prompt.shape
    SB, SL, _ = x_sample.shape
    depth = norm_gains.shape[0]
    TP, TS = B * L, SB * SL
    H = hgrn_norm_gain.shape[1] // HG_DV
    n_kv = cache_k_win.shape[3]
    n_q = w_attn_out.shape[1] // HEAD_DIM
    QW, KW = n_q * HEAD_DIM, n_kv * HEAD_DIM
    buf = cache_k_win.shape[2]

    x = (x_prompt.reshape(TP, D), x_sample.reshape(TS, D))

    lb_all = jnp.cumsum(jax.nn.softmax(hgrn_lb_logits.astype(F32), axis=0), axis=0)
    lb_all = lb_all - lb_all[0:1]

    pos = jnp.concatenate([jnp.tile(jnp.arange(L, dtype=jnp.int32), B),
                           jnp.tile(PAST_LEN + jnp.arange(SL, dtype=jnp.int32), SB)])
    rope = _rope_tables(pos)

    hg_p, kp_l, vp_l, ks_l, vs_l = [], [], [], [], []
    hg_s = None
    for layer in range(depth):
        g = norm_gains[layer]
        x = _ffn_half(x, g[0:1], g[1:2], w_ffn_in, w_ffn_out, layer, 0)
        if layer % N_MIXERS == 0:
            a = layer // N_MIXERS
            lb = lb_all[a][None, :]
            lbp = (jnp.log(lb), jnp.log1p(-lb), 1.0 - lb)
            ng = hgrn_norm_gain[a][None, :].astype(F32)
            o_p, s_p = _hgrn_fused(x, g[2:3], w_hgrn_in, a, B, L, lbp, ng)
            o_p = o_p.reshape(TP, -1)
            p_s = _proj(x[TP:], g[2:3], w_hgrn_in, a, (TS, HGRN_SAMPLE_PROJ_TN))
            ps = jnp.pad(p_s.reshape(SB, SL, -1), ((0, 0), (0, SAMPLE_PAD - SL), (0, 0)))
            o_s, hg_s = _hgrn_scan(ps.reshape(SB * SAMPLE_PAD, -1), SB, SAMPLE_PAD, SAMPLE_PAD, H, SL,
                                   lbp, ng, (state_hgrn, a), F32, prev=hg_s)
            o_s = o_s.reshape(SB, SAMPLE_PAD, -1)[:, :SL].reshape(TS, -1).astype(BF16)
            x = _outproj(o_p, o_s, w_hgrn_out, x, g[3:4], a)
            hg_p.append(s_p.astype(state_hgrn.dtype))
        else:
            bb = layer // N_MIXERS
            p = _proj(x, g[2:3], w_attn_in, bb, ATTN_PROJ_TILES, rope=rope, n_rope_cols=QW + KW)
            sinks = attn_sinks[bb].astype(F32)
            o_p = _attn_prompt(p, sinks, B, L, n_q, n_kv)
            pn = p[TP:].reshape(SB, SL, -1)
            ck = cache_k_win.reshape(-1, SB, buf, KW)
            cv = cache_v_win.reshape(-1, SB, buf, KW)
            pn_rows = jnp.pad(pn, ((0, 0), (0, -SL % SUBLANES), (0, 0)))
            o_s = _attn_sample(pn_rows, ck, cv, bb, sinks, n_q, n_kv, SL)
            x = _outproj(o_p, o_s[:, :SL].reshape(TS, QW).astype(BF16), w_attn_out, x, g[3:4], bb)
            tails = [p[(s + 1) * L - buf:(s + 1) * L, QW:] for s in range(B)]
            kp_l.append(jnp.stack([t[:, :KW] for t in tails]).reshape(B, buf, n_kv, HEAD_DIM))
            vp_l.append(jnp.stack([t[:, KW:] for t in tails]).reshape(B, buf, n_kv, HEAD_DIM))
            k_new = pn[:, :, QW:QW + KW].reshape(SB, SL, n_kv, HEAD_DIM)
            v_new = pn[:, :, QW + KW:].reshape(SB, SL, n_kv, HEAD_DIM)
            ks_l.append(k_new)
            vs_l.append(v_new)
        x = _ffn_half(x, g[4:5], g[5:6], w_ffn_in, w_ffn_out, layer, 1,
                      out_split=TP if layer == depth - 1 else None)

    yp = x[0].reshape(B, L, D)
    ys = x[1].reshape(SB, SL, D)
    k_win_s = jnp.concatenate([cache_k_win, jnp.stack(ks_l).astype(cache_k_win.dtype)], axis=2)[:, :, -buf:]
    v_win_s = jnp.concatenate([cache_v_win, jnp.stack(vs_l).astype(cache_v_win.dtype)], axis=2)[:, :, -buf:]
    return (yp, ys, jnp.stack(hg_p), hg_s.astype(state_hgrn.dtype), jnp.stack(kp_l), jnp.stack(vp_l),
            k_win_s, v_win_s)
```

```python
import functools

import jax
import jax.numpy as jnp
from jax import lax
from jax.experimental import pallas as pl
from jax.experimental.pallas import tpu as pltpu

F32 = jnp.float32
BF16 = jnp.bfloat16

NORM_EPS = 1e-6
HEAD_DIM = 64
HG_DK = 128
HG_DV = 128
HG_CHUNK = 64
WINDOW = 128
PAST_LEN = 16384
ROPE_THETA = 10000.0
N_MIXERS = 2

LANES = 128
SUBLANES = 8
MIB = 1024 * 1024
VMEM_BYTES_V7X = 64 * MIB
VMEM_HEADROOM = 6 * MIB
VMEM_MIN_REQUEST = 16 * MIB
LOOP_TEMPS_BYTES = 8 * MIB

FFN_TM, FFN_TF = 1040, 256
FFN_RC = 208
HGRN_SAMPLE_PROJ_TN = 1024
FUSED_HB, FUSED_NSEQ, FUSED_RB = 2, 2, 256
ATTN_PROJ_TILES = (2080, 512)
PROJ_X_DOUBLE_BUFFER_MAX_BYTES = 12 * MIB
ROPE_ROW_CHUNK = 208
OUT_TM = 512
ATTN_Q_BLOCKS = 2
SAMPLE_PAD = 16


def _vmem_limit(nbytes):
    return int(min(VMEM_BYTES_V7X - VMEM_HEADROOM, max(nbytes * 5 // 4, VMEM_MIN_REQUEST)))


def _rms(x, g):
    return x * lax.rsqrt(jnp.mean(x * x, axis=-1, keepdims=True) + NORM_EPS) * g


def _dot(a, b):
    return jnp.dot(a, b, preferred_element_type=F32)


def _dot_nt(a, b):
    return lax.dot_general(a, b, (((1,), (1,)), ((), ())), preferred_element_type=F32)


def _dot_tn(a, b):
    return lax.dot_general(a, b, (((0,), (0,)), ((), ())), preferred_element_type=F32)


def _ffn_kernel(*refs, nt, nj, tm, rc, in_split, out_split):
    refs = list(refs)
    x_refs = [refs.pop(0) for _ in range(1 if in_split is None else 2)]
    gpre_ref, gpost_ref, wg_ref, wu_ref, wo_ref = (refs.pop(0) for _ in range(5))
    y_refs = [refs.pop(0) for _ in range(1 if out_split is None else 2)]
    acc_ref, xn_ref, xc_ref, yc_ref, in_sem, out_sem = refs

    i = pl.program_id(0)
    j = pl.program_id(1)
    nch = tm // rc
    slot = i % 2
    pslot = 1 - slot
    k = j - 1
    gpre = gpre_ref[...]
    gpost = gpost_ref[...]

    def in_copy(row0, b):
        return pltpu.make_async_copy(x_refs[0].at[pl.ds(row0, rc)], xc_ref.at[b], in_sem.at[b])

    def out_copy(row0, b):
        return pltpu.make_async_copy(yc_ref.at[b], y_refs[0].at[pl.ds(row0, rc)], out_sem.at[b])

    def crossing_row(split):
        if split is None:
            return None
        assert split % rc != 0 and nt * tm - split < rc, "exactly one chunk must cross the split point"
        return (split // rc) * rc

    in_cross = crossing_row(in_split)
    out_cross = crossing_row(out_split)
    assert out_cross is None or out_cross >= (nt - 1) * tm, "only the last row tile may cross the output split"

    def in_copies_crossing(b):
        na = in_split - in_cross
        return [pltpu.make_async_copy(x_refs[0].at[pl.ds(in_cross, na)], xc_ref.at[b, pl.ds(0, na)], in_sem.at[b]),
                pltpu.make_async_copy(x_refs[1].at[pl.ds(0, rc - na)], xc_ref.at[b, pl.ds(na, rc - na)],
                                      in_sem.at[b])]

    def out_copies_crossing(b):
        na = out_split - out_cross
        return [pltpu.make_async_copy(yc_ref.at[b, pl.ds(0, na)], y_refs[0].at[pl.ds(out_cross, na)], out_sem.at[b]),
                pltpu.make_async_copy(yc_ref.at[b, pl.ds(na, rc - na)], y_refs[1].at[pl.ds(0, rc - na)],
                                      out_sem.at[b])]

    def static_in_copies(row0, b):
        return in_copies_crossing(b) if row0 == in_cross else [in_copy(row0, b)]

    def static_out_copies(row0, b):
        return out_copies_crossing(b) if row0 == out_cross else [out_copy(row0, b)]

    def task_row(t):
        return jnp.where(t < nch, (i - 1) * tm + t * rc, (i + 1) * tm + (t - nch) * rc)

    def task_live(t):
        return (t >= 0) & (t < 2 * nch) & jnp.where(t < nch, i >= 1, i <= nt - 2)

    def task_crosses(t):
        return False if in_cross is None else task_row(t) == in_cross

    @pl.when((i == 0) & (j == 0))
    def _():
        for c in range(nch):
            cps = static_in_copies(c * rc, c % 2)
            for cp in cps:
                cp.start()
            for cp in cps:
                cp.wait()
            xn_ref[0, c * rc:(c + 1) * rc] = _rms(xc_ref[c % 2], gpre).astype(BF16)

    @pl.when(task_live(k) & jnp.logical_not(task_crosses(k)))
    def _():
        in_copy(0, k & 1).wait()

    @pl.when(task_live(k) & (k >= 2) & (k < nch))
    def _():
        out_copy(0, k & 1).wait()

    @pl.when((j == nch + 1) & (i >= 1))
    def _():
        out_copy(0, 0).wait()
        out_copy(0, 1).wait()

    @pl.when(task_live(j) & jnp.logical_not(task_crosses(j)))
    def _():
        in_copy(task_row(j), j & 1).start()

    if in_cross is not None:
        @pl.when(task_live(k) & task_crosses(k))
        def _():
            for cp in in_copies_crossing(k & 1):
                cp.wait()

        @pl.when(task_live(j) & task_crosses(j))
        def _():
            for cp in in_copies_crossing(j & 1):
                cp.start()

    def accumulate(first):
        xn = xn_ref[slot]
        hg = _dot(xn, wg_ref[...].astype(BF16))
        hu = _dot(xn, wu_ref[...].astype(BF16))
        act = (hg * jax.nn.sigmoid(hg) * hu).astype(BF16)
        contrib = _dot(act, wo_ref[...].astype(BF16))
        if first:
            acc_ref[slot] = contrib
        else:
            acc_ref[slot] += contrib

    is_epi = (k >= 0) & (k < nch) & (i >= 1)
    is_pro = (k >= nch) & (k < 2 * nch) & (i <= nt - 2)

    @pl.when(j == 0)
    def _():
        accumulate(True)

    @pl.when(is_epi)
    def _():
        accumulate(False)
        b = k & 1
        rows = pl.ds(pl.multiple_of(k * rc, rc), rc)
        yc_ref[b] = xc_ref[b] + 0.5 * _rms(acc_ref[pslot, rows], gpost)
        out_copy((i - 1) * tm + k * rc, b).start()

    @pl.when(is_pro)
    def _():
        accumulate(False)
        rows = pl.ds(pl.multiple_of((k - nch) * rc, rc), rc)
        xn_ref[pslot, rows] = _rms(xc_ref[k & 1], gpre).astype(BF16)

    @pl.when((j >= 1) & jnp.logical_not(is_epi) & jnp.logical_not(is_pro))
    def _():
        accumulate(False)

    @pl.when((i == nt - 1) & (j == nj - 1))
    def _():
        last = (nt - 1) % 2
        for c in range(nch):
            b = c % 2
            row0 = (nt - 1) * tm + c * rc
            cps = static_in_copies(row0, b)
            for cp in cps:
                cp.start()
            for cp in cps:
                cp.wait()
            yc_ref[b] = xc_ref[b] + 0.5 * _rms(acc_ref[last, c * rc:(c + 1) * rc], gpost)
            ocs = static_out_copies(row0, b)
            for oc in ocs:
                oc.start()
            for oc in ocs:
                oc.wait()


def _ffn_half(x, g_pre, g_post, w_in, w_out, layer, half, out_split=None):
    xs = x if isinstance(x, (tuple, list)) else (x,)
    in_split = xs[0].shape[0] if len(xs) == 2 else None
    T = sum(a.shape[0] for a in xs)
    D = xs[0].shape[1]
    F = w_out.shape[2]
    TM, TF, RC = FFN_TM, FFN_TF, FFN_RC
    nj = F // TF
    nt = T // TM
    assert T % TM == 0 and F % TF == 0 and TM % RC == 0 and TM // RC >= 2 and nj >= 2 * (TM // RC) + 2
    est = (2 * TM * D * (4 + 2) + 4 * RC * D * 4 + 2 * 3 * D * TF * 4
           + 3 * D * TF * 2 + 4 * TM * TF * 4 + TM * D * 4)
    any_spec = pl.BlockSpec(memory_space=pl.ANY)
    if out_split is None:
        out_specs, out_shape = any_spec, jax.ShapeDtypeStruct((T, D), F32)
    else:
        out_specs = [any_spec, any_spec]
        out_shape = [jax.ShapeDtypeStruct((out_split, D), F32), jax.ShapeDtypeStruct((T - out_split, D), F32)]
    return pl.pallas_call(
        functools.partial(_ffn_kernel, nt=nt, nj=nj, tm=TM, rc=RC, in_split=in_split, out_split=out_split),
        grid=(nt, nj),
        in_specs=[any_spec] * len(xs) + [
            pl.BlockSpec((1, D), lambda i, j: (0, 0)),
            pl.BlockSpec((1, D), lambda i, j: (0, 0)),
            pl.BlockSpec((None, None, D, TF), lambda i, j: (layer, half, 0, j)),
            pl.BlockSpec((None, None, D, TF), lambda i, j: (layer, half, 0, nj + j)),
            pl.BlockSpec((None, None, TF, D), lambda i, j: (layer, half, j, 0)),
        ],
        out_specs=out_specs,
        out_shape=out_shape,
        scratch_shapes=[pltpu.VMEM((2, TM, D), F32), pltpu.VMEM((2, TM, D), BF16),
                        pltpu.VMEM((2, RC, D), F32), pltpu.VMEM((2, RC, D), F32),
                        pltpu.SemaphoreType.DMA((2,)), pltpu.SemaphoreType.DMA((2,))],
        compiler_params=pltpu.CompilerParams(
            dimension_semantics=("arbitrary", "arbitrary"),
            vmem_limit_bytes=_vmem_limit(est)),
        name="ffn_half",
    )(*xs, g_pre, g_post, w_in, w_in, w_out)


def _rope_tile(y, cos, sin_signed):
    lane = lax.broadcasted_iota(jnp.int32, y.shape, 1)
    first_half = (lane & (HEAD_DIM - 1)) < (HEAD_DIM // 2)
    rot = jnp.where(first_half,
                    pltpu.roll(y, LANES - HEAD_DIM // 2, 1),
                    pltpu.roll(y, HEAD_DIM // 2, 1))
    return y * cos + rot * sin_signed


def _proj_kernel(x_ref, g_ref, w_ref, o_ref, xn_ref):
    @pl.when(pl.program_id(1) == 0)
    def _():
        xn_ref[...] = _rms(x_ref[...], g_ref[...]).astype(BF16)

    o_ref[...] = _dot(xn_ref[...], w_ref[...].astype(BF16))


def _proj_rope_kernel(x_ref, g_ref, w_ref, cos_ref, sin_ref, o_ref, xn_ref, *, row_chunk):
    @pl.when(pl.program_id(1) == 0)
    def _():
        xn_ref[...] = _rms(x_ref[...], g_ref[...]).astype(BF16)

    wb = w_ref[...].astype(BF16)
    for r0 in range(0, xn_ref.shape[0], row_chunk):
        rows = slice(r0, r0 + row_chunk)
        y = _dot(xn_ref[rows], wb)
        cos = cos_ref[rows]
        sin = sin_ref[rows]
        for c in range(y.shape[1] // LANES):
            cols = slice(c * LANES, (c + 1) * LANES)
            o_ref[rows, cols] = _rope_tile(y[:, cols], cos, sin)


def _proj(x, g, w, layer, tiles, rope=None, n_rope_cols=0):
    T, D = x.shape
    N = w.shape[2]
    TM, TN = tiles
    assert T % TM == 0 and N % TN == 0 and n_rope_cols % TN == 0
    x_buffers = 1 if TM * D * 4 > PROJ_X_DOUBLE_BUFFER_MAX_BYTES else 2
    est = x_buffers * TM * D * 4 + TM * D * 2 + 2 * D * TN * 4 + D * TN * 2 + 3 * TM * TN * 4
    in_specs = [
        pl.BlockSpec((TM, D), lambda i, j: (i, 0), pipeline_mode=pl.Buffered(x_buffers)),
        pl.BlockSpec((1, D), lambda i, j: (0, 0)),
        pl.BlockSpec((None, D, TN), lambda i, j: (layer, 0, j)),
    ]
    args = [x, g, w]
    if rope is None:
        body = _proj_kernel
    else:
        assert TM % ROPE_ROW_CHUNK == 0
        body = functools.partial(_proj_rope_kernel, row_chunk=ROPE_ROW_CHUNK)
        n_rope_blocks = n_rope_cols // TN
        table = pl.BlockSpec((None, TM, LANES), lambda i, j: (jnp.where(j < n_rope_blocks, 0, 1), i, 0))
        in_specs += [table, table]
        args += list(rope)
    return pl.pallas_call(
        body,
        grid=(T // TM, N // TN),
        in_specs=in_specs,
        out_specs=pl.BlockSpec((TM, TN), lambda i, j: (i, j)),
        out_shape=jax.ShapeDtypeStruct((T, N), F32),
        scratch_shapes=[pltpu.VMEM((TM, D), BF16)],
        compiler_params=pltpu.CompilerParams(
            dimension_semantics=("parallel", "arbitrary"),
            vmem_limit_bytes=_vmem_limit(est)),
        name="mixer_in_proj",
    )(*args)


def _outproj_kernel(op_ref, os_ref, w_ref, x_ref, g_ref, y_ref, wbf_ref, *, n_prompt_tiles, ts):
    i = pl.program_id(0)

    @pl.when(i == 0)
    def _():
        wbf_ref[...] = w_ref[...].astype(BF16)

    @pl.when(i < n_prompt_tiles)
    def _():
        y_ref[...] = x_ref[...] + _rms(_dot(op_ref[...], wbf_ref[...]), g_ref[...])

    @pl.when(i == n_prompt_tiles)
    def _():
        y_ref[0:ts] = x_ref[0:ts] + _rms(_dot(os_ref[...], wbf_ref[...]), g_ref[...])


def _outproj(o_prompt, o_sample, w, x, g, layer):
    T, D = x.shape
    TP, K = o_prompt.shape
    TS = o_sample.shape[0]
    TM = OUT_TM
    assert TP % TM == 0 and TS <= TM and TS % SUBLANES == 0 and TP + TS == T
    npt = TP // TM
    est = K * D * (4 + 2) + 2 * TM * K * 2 + 2 * TS * K * 2 + 5 * TM * D * 4
    return pl.pallas_call(
        functools.partial(_outproj_kernel, n_prompt_tiles=npt, ts=TS),
        grid=(npt + 1,),
        in_specs=[
            pl.BlockSpec((TM, K), lambda i: (jnp.minimum(i, npt - 1), 0)),
            pl.BlockSpec((TS, K), lambda i: (0, 0)),
            pl.BlockSpec((None, K, D), lambda i: (layer, 0, 0), pipeline_mode=pl.Buffered(1)),
            pl.BlockSpec((TM, D), lambda i: (i, 0)),
            pl.BlockSpec((1, D), lambda i: (0, 0)),
        ],
        out_specs=pl.BlockSpec((TM, D), lambda i: (i, 0)),
        out_shape=jax.ShapeDtypeStruct((T, D), F32),
        scratch_shapes=[pltpu.VMEM((K, D), BF16)],
        compiler_params=pltpu.CompilerParams(
            dimension_semantics=("arbitrary",),
            vmem_limit_bytes=_vmem_limit(est)),
        name="mixer_out_proj",
    )(o_prompt, o_sample, w, x, g)


def _tiles(x):
    return [x[SUBLANES * j:SUBLANES * (j + 1)] for j in range(x.shape[0] // SUBLANES)]


def _bcast_row(t, i):
    return jnp.broadcast_to(t[i:i + 1], t.shape)


def _tile_prefix(t, sub):
    p = t + jnp.where(sub >= 1, pltpu.roll(t, 1, 0), 0.0)
    p = p + jnp.where(sub >= 2, pltpu.roll(p, 2, 0), 0.0)
    return p + jnp.where(sub >= 4, pltpu.roll(p, 4, 0), 0.0)


def _hgrn_gates(fz, loglb, log1m, onem, valid):
    e = jnp.exp(-jnp.abs(fz))
    lsig = jnp.minimum(fz, 0.0) - jnp.log(1.0 + e)
    ct = log1m + lsig
    g = jnp.maximum(loglb, ct) + jnp.log(1.0 + jnp.exp(-jnp.abs(loglb - ct)))
    r = 1.0 / (1.0 + e)
    k = onem * jnp.where(fz > 0.0, e * r, r)
    if valid is not None:
        g = jnp.where(valid, g, 0.0)
        k = jnp.where(valid, k, 0.0)
    return g, k


def _hgrn_level_operands(m, qt, kt, gt, bt, sub):
    nv = len(qt)
    zero = jnp.zeros_like(qt[0])
    qm, km = [], []
    if m >= SUBLANES:
        mv = m // SUBLANES
        for j in range(nv):
            jr = (j // (2 * mv)) * (2 * mv) + mv - 1
            ref = _bcast_row(bt[jr], SUBLANES - 1)
            if j % (2 * mv) >= mv:
                qm.append(qt[j] * jnp.exp(bt[j] - ref))
                km.append(zero)
            else:
                qm.append(zero)
                km.append(kt[j] * jnp.exp(ref - bt[j]))
        return qm, km
    upper = (sub & m) != 0
    for j in range(nv):
        if m == 1:
            e = jnp.exp(gt[j])
            qm.append(jnp.where(upper, qt[j] * e, 0.0))
            km.append(jnp.where(upper, 0.0, kt[j]))
            continue
        if m == 4:
            ref = _bcast_row(bt[j], 3)
        else:
            ref = jnp.where(sub < 4, _bcast_row(bt[j], 1), _bcast_row(bt[j], 5))
        e = jnp.exp(-jnp.abs(bt[j] - ref))
        qm.append(jnp.where(upper, qt[j] * e, 0.0))
        km.append(jnp.where(upper, 0.0, kt[j] * e))
    return qm, km


def _hgrn_chunk_group(loads, params, sts, *, C, valid_len):
    nh = len(loads)
    sub = lax.broadcasted_iota(jnp.int32, (SUBLANES, LANES), 0)
    valid = None
    if valid_len < C:
        valid = lax.broadcasted_iota(jnp.int32, (C, LANES), 0) < valid_len
    ti = lax.broadcasted_iota(jnp.int32, (C, C), 0)
    si = lax.broadcasted_iota(jnp.int32, (C, C), 1)

    gk = [_hgrn_gates(loads[h][1], params[h][0], params[h][1], params[h][2], valid) for h in range(nh)]

    gts, bts = [], []
    for h in range(nh):
        gt = _tiles(gk[h][0])
        bt = []
        for j, t in enumerate(gt):
            p = _tile_prefix(t, sub)
            bt.append(p if j == 0 else p + _bcast_row(bt[j - 1], SUBLANES - 1))
        gts.append(gt)
        bts.append(bt)

    scores = []
    for h in range(nh):
        q, k = loads[h][0], gk[h][1]
        qt, kt = _tiles(q), _tiles(k)
        a = jnp.where(ti == si, _dot_nt(q.astype(BF16), k.astype(BF16)), 0.0)
        m = C // 2
        while m >= 1:
            qm, km = _hgrn_level_operands(m, qt, kt, gts[h], bts[h], sub)
            am = _dot_nt(jnp.concatenate(qm, axis=0).astype(BF16), jnp.concatenate(km, axis=0).astype(BF16))
            if 2 * m < C:
                am = jnp.where((ti & (-2 * m)) == (si & (-2 * m)), am, 0.0)
            a = a + am
            m //= 2
        scores.append(a)

    outs = []
    for h in range(nh):
        q, _, v, gz = loads[h]
        k = gk[h][1]
        ng = params[h][3]
        st = sts[h]
        b = jnp.concatenate(bts[h], axis=0)
        o = _dot(scores[h].astype(BF16), v.astype(BF16)) + _dot_nt((q * jnp.exp(b)).astype(BF16), st.astype(BF16))
        bl = b[C - 1:C]
        kb = (k * jnp.exp(bl - b)).astype(BF16)
        st_new = st * jnp.exp(bl) + _dot_tn(v.astype(BF16), kb)
        on = o * lax.rsqrt(jnp.mean(o * o, axis=-1, keepdims=True) + NORM_EPS) * ng
        outs.append((on * (gz * jax.nn.sigmoid(gz)), st_new))
    return outs


def _hgrn_kernel(*refs, C, NC, HB, valid_len, has_s0, n_prev):
    refs = list(refs)
    q_ref, f_ref, i_ref, g_ref, loglb_ref, log1m_ref, onem_ref, ng_ref = (refs.pop(0) for _ in range(8))
    s0_ref = refs.pop(0) if has_s0 else None
    prev_ref = refs.pop(0) if n_prev else None
    o_ref, s_ref, st_ref = refs

    for h in range(HB):
        if has_s0:
            st_ref[h] = s0_ref[0, h].T
        else:
            st_ref[h] = jnp.zeros((HG_DV, HG_DK), F32)

    def body(c, carry):
        rows = pl.ds(pl.multiple_of(c * C, C), C)
        cols = [slice(h * LANES, (h + 1) * LANES) for h in range(HB)]
        loads = [(q_ref[rows, cs], f_ref[rows, cs], i_ref[rows, cs], g_ref[rows, cs]) for cs in cols]
        params = [(loglb_ref[:, cs], log1m_ref[:, cs], onem_ref[:, cs], ng_ref[:, cs]) for cs in cols]
        outs = _hgrn_chunk_group(loads, params, [st_ref[h] for h in range(HB)], C=C, valid_len=valid_len)
        for h in range(HB):
            st_ref[h] = outs[h][1]
            o_ref[rows, cols[h]] = outs[h][0].astype(o_ref.dtype)
        return carry

    if NC == 1:
        body(0, 0)
    else:
        lax.fori_loop(0, NC, body, 0)

    for layer in range(n_prev):
        s_ref[layer] = prev_ref[layer]
    for h in range(HB):
        s_ref[n_prev, 0, h] = st_ref[h].T


def _hgrn_scan(p, nseq, L, C, HB, valid_len, lbp, ng, s0, out_dtype, prev=None):
    H = ng.shape[1] // LANES
    nhb = H // HB
    W = HB * LANES
    NC = L // C
    loglb, log1m, onem = lbp
    has_s0 = s0 is not None

    def sect(s):
        return pl.BlockSpec((L, W), lambda b, hb, s=s: (b, s * nhb + hb))

    vec = pl.BlockSpec((1, W), lambda b, hb: (0, hb))
    in_specs = [sect(0), sect(1), sect(2), sect(3), vec, vec, vec, vec]
    args = [p, p, p, p, loglb, log1m, onem, ng]
    if has_s0:
        s0_all, s0_layer = s0
        in_specs.append(pl.BlockSpec((None, 1, HB, HG_DK, HG_DV), lambda b, hb: (s0_layer, b, hb, 0, 0)))
        args.append(s0_all)
    n_prev = 0 if prev is None else prev.shape[0]
    if n_prev:
        in_specs.append(pl.BlockSpec((n_prev, 1, HB, HG_DK, HG_DV), lambda b, hb: (0, b, hb, 0, 0)))
        args.append(prev)
    est = (2 * 4 * L * W * 4 + 2 * L * W * 4 + (6 + 4 * (2 * n_prev + 1)) * HB * HG_DK * HG_DV * 4
           + LOOP_TEMPS_BYTES)
    return pl.pallas_call(
        functools.partial(_hgrn_kernel, C=C, NC=NC, HB=HB, valid_len=valid_len, has_s0=has_s0, n_prev=n_prev),
        grid=(nseq, nhb),
        in_specs=in_specs,
        out_specs=[pl.BlockSpec((L, W), lambda b, hb: (b, hb)),
                   pl.BlockSpec((n_prev + 1, 1, HB, HG_DK, HG_DV), lambda b, hb: (0, b, hb, 0, 0))],
        out_shape=[jax.ShapeDtypeStruct((nseq * L, H * LANES), out_dtype),
                   jax.ShapeDtypeStruct((n_prev + 1, nseq, H, HG_DK, HG_DV), F32)],
        scratch_shapes=[pltpu.VMEM((HB, HG_DV, HG_DK), F32)],
        compiler_params=pltpu.CompilerParams(
            dimension_semantics=("parallel", "parallel"),
            vmem_limit_bytes=_vmem_limit(est)),
        name="hgrn_scan",
    )(*args)


def _hgrn_fused_kernel(x_hbm, g_ref, wq_ref, wf_ref, wi_ref, wg_ref, loglb_ref, log1m_ref, onem_ref, ng_ref,
                       o_ref, s_ref, xn_ref, xc_ref, proj_ref, st_ref, in_sem, *, L, C, RB, NSEQ, HB):
    sg = pl.program_id(0)
    hb = pl.program_id(1)
    D = xn_ref.shape[1]
    WC = HB * LANES
    NB = L // RB
    CPB = RB // C
    w_refs = (wq_ref, wf_ref, wi_ref, wg_ref)
    assert CPB == len(w_refs)

    @pl.when(hb == 0)
    def _():
        g = g_ref[...]
        nchunk = NSEQ * NB

        def copy(c):
            return pltpu.make_async_copy(x_hbm.at[pl.ds(sg * (NSEQ * L) + c * RB, RB)],
                                         xc_ref.at[c % 2], in_sem.at[c % 2])
        copy(0).start()
        for c in range(nchunk):
            copy(c).wait()
            if c + 1 < nchunk:
                copy(c + 1).start()
            xn_ref[c * RB:(c + 1) * RB] = _rms(xc_ref[c % 2], g).astype(BF16)

    def project(rb, section, slot):
        rows = [xn_ref[pl.ds(pl.multiple_of(s * L + rb * RB, RB), RB)] for s in range(NSEQ)]
        y = _dot(jnp.concatenate(rows, axis=0), w_refs[section][...].astype(BF16))
        proj_ref[slot, :, section * WC:(section + 1) * WC] = y

    for ch in range(NSEQ * HB):
        st_ref[ch] = jnp.zeros((HG_DV, HG_DK), F32)
    for section in range(CPB):
        project(0, section, 0)

    params = [(loglb_ref[:, h * LANES:(h + 1) * LANES], log1m_ref[:, h * LANES:(h + 1) * LANES],
               onem_ref[:, h * LANES:(h + 1) * LANES], ng_ref[:, h * LANES:(h + 1) * LANES])
              for _ in range(NSEQ) for h in range(HB)]

    def block(rb, project_next):
        slot = rb % 2
        for ci in range(CPB):
            loads = []
            for s in range(NSEQ):
                rows = slice(s * RB + ci * C, s * RB + (ci + 1) * C)
                for h in range(HB):
                    loads.append(tuple(proj_ref[slot, rows, sec * WC + h * LANES:sec * WC + (h + 1) * LANES]
                                       for sec in range(4)))
            if project_next:
                project(rb + 1, ci, 1 - slot)
            outs = _hgrn_chunk_group(loads, params, [st_ref[ch] for ch in range(NSEQ * HB)], C=C, valid_len=C)
            orow = pl.ds(pl.multiple_of(rb * RB + ci * C, C), C)
            for s in range(NSEQ):
                for h in range(HB):
                    ch = s * HB + h
                    st_ref[ch] = outs[ch][1]
                    o_ref[s, orow, h * LANES:(h + 1) * LANES] = outs[ch][0].astype(o_ref.dtype)

    def body(rb, carry):
        block(rb, True)
        return carry

    lax.fori_loop(0, NB - 1, body, 0)
    block(NB - 1, False)

    for s in range(NSEQ):
        for h in range(HB):
            s_ref[s, h] = st_ref[s * HB + h].T


def _hgrn_fused(x, g, w, layer, B, L, lbp, ng):
    D = x.shape[1]
    H = ng.shape[1] // LANES
    HB, NSEQ, RB, C = FUSED_HB, FUSED_NSEQ, FUSED_RB, HG_CHUNK
    WC = HB * LANES
    nhb = H // HB
    assert B % NSEQ == 0 and H % HB == 0 and L % RB == 0 and RB // C == 4
    loglb, log1m, onem = lbp

    def sect(s):
        return pl.BlockSpec((None, D, WC), lambda sg, hb, s=s: (layer, 0, s * nhb + hb))

    vec = pl.BlockSpec((1, WC), lambda sg, hb: (0, hb))
    est = (NSEQ * L * D * 2 + 2 * RB * D * 4 + 2 * NSEQ * RB * 4 * WC * 4 + 2 * 4 * D * WC * 4
           + 2 * NSEQ * L * WC * 2 + LOOP_TEMPS_BYTES)
    return pl.pallas_call(
        functools.partial(_hgrn_fused_kernel, L=L, C=C, RB=RB, NSEQ=NSEQ, HB=HB),
        grid=(B // NSEQ, nhb),
        in_specs=[pl.BlockSpec(memory_space=pl.ANY), pl.BlockSpec((1, D), lambda sg, hb: (0, 0)),
                  sect(0), sect(1), sect(2), sect(3), vec, vec, vec, vec],
        out_specs=[pl.BlockSpec((NSEQ, L, WC), lambda sg, hb: (sg, 0, hb)),
                   pl.BlockSpec((NSEQ, HB, HG_DK, HG_DV), lambda sg, hb: (sg, hb, 0, 0))],
        out_shape=[jax.ShapeDtypeStruct((B, L, H * LANES), BF16),
                   jax.ShapeDtypeStruct((B, H, HG_DK, HG_DV), F32)],
        scratch_shapes=[pltpu.VMEM((NSEQ * L, D), BF16), pltpu.VMEM((2, RB, D), F32),
                        pltpu.VMEM((2, NSEQ * RB, 4 * WC), F32), pltpu.VMEM((NSEQ * HB, HG_DV, HG_DK), F32),
                        pltpu.SemaphoreType.DMA((2,))],
        compiler_params=pltpu.CompilerParams(
            dimension_semantics=("arbitrary", "arbitrary"),
            vmem_limit_bytes=_vmem_limit(est)),
        name="hgrn_fused",
    )(x, g, w, w, w, w, loglb, log1m, onem, ng)


def _sink_softmax(scores, sink):
    m = sink
    for s in scores:
        m = jnp.maximum(m, jnp.max(s, axis=-1, keepdims=True))
    ps = [jnp.exp(s - m) for s in scores]
    denom = jnp.exp(sink - m)
    for p in ps:
        denom = denom + jnp.sum(p, axis=-1, keepdims=True)
    return [p / denom for p in ps]


def _attn_prompt_kernel(sinks_ref, q_ref, kc_ref, kp_ref, vc_ref, vp_ref, o_ref, *, n_kv, group, qb):
    for u in range(qb):
        _attn_prompt_block(sinks_ref, q_ref, kc_ref, kp_ref, vc_ref, vp_ref, o_ref, u, n_kv=n_kv, group=group)


def _attn_prompt_block(sinks_ref, q_ref, kc_ref, kp_ref, vc_ref, vp_ref, o_ref, u, *, n_kv, group):
    i = pl.program_id(1)
    W = WINDOW
    HD = HEAD_DIM
    rows = slice(u * W, (u + 1) * W)
    prev = slice((u - 1) * W, u * W)
    j = lax.broadcasted_iota(jnp.int32, (2 * W, W), 0)
    r = lax.broadcasted_iota(jnp.int32, (2 * W, W), 1)
    first_key = jnp.where(i > 0, 0, W) if u == 0 else 0
    mask = (j >= jnp.maximum(r, first_key)) & (j <= r + W)
    scale = HD ** -0.5
    zeros = jnp.zeros((HD, W), F32)

    s_all, vt_all = [], []
    for c in range(n_kv // 2):
        cs = slice(c * LANES, (c + 1) * LANES)
        k_prev = kp_ref[:, cs] if u == 0 else kc_ref[prev, cs]
        v_prev = vp_ref[:, cs] if u == 0 else vc_ref[prev, cs]
        ktile = jnp.concatenate([k_prev, kc_ref[rows, cs]], axis=0).astype(BF16)
        vt_all.append(jnp.concatenate([v_prev, vc_ref[rows, cs]], axis=0).T.astype(BF16))
        for par in range(2):
            kh = 2 * c + par
            rhs = []
            for pr in range(group // 2):
                c0 = (kh * group + 2 * pr) * HD
                pt = (q_ref[rows, c0:c0 + LANES] * scale).T
                for half in range(2):
                    blk = pt[half * HD:(half + 1) * HD]
                    rhs.append(jnp.concatenate([blk, zeros] if par == 0 else [zeros, blk], axis=0))
            s_all.append(_dot(ktile, jnp.concatenate(rhs, axis=1).astype(BF16)))

    p_all, inv_all = [], []
    for kh in range(n_kv):
        pn, inv = [], []
        for hh in range(group):
            sink = sinks_ref[kh * group + hh]
            sm = jnp.where(mask, s_all[kh][:, hh * W:(hh + 1) * W], -jnp.inf)
            mx = jnp.maximum(jnp.max(sm, axis=0, keepdims=True), sink)
            p = jnp.exp(sm - mx)
            inv.append(1.0 / (jnp.sum(p, axis=0, keepdims=True) + jnp.exp(sink - mx)))
            pn.append(p.astype(BF16))
        p_all.append(jnp.concatenate(pn, axis=1))
        inv_all.append(jnp.concatenate(inv, axis=1))

    for kh in range(n_kv):
        par = kh % 2
        o_t = _dot(vt_all[kh // 2], p_all[kh])[par * HD:(par + 1) * HD] * inv_all[kh]
        for pr in range(group // 2):
            pair_t = jnp.concatenate([o_t[:, (2 * pr) * W:(2 * pr + 1) * W],
                                      o_t[:, (2 * pr + 1) * W:(2 * pr + 2) * W]], axis=0)
            c0 = (kh * group + 2 * pr) * HD
            o_ref[rows, c0:c0 + LANES] = pair_t.T.astype(o_ref.dtype)


def _attn_prompt(p, sinks, B, L, n_q, n_kv):
    W = WINDOW
    nb = L // W
    QW = n_q * HEAD_DIM
    KW = n_kv * HEAD_DIM
    assert 2 * HEAD_DIM == LANES and W == LANES and L % W == 0
    assert n_kv % 2 == 0 and (n_q // n_kv) % 2 == 0 and QW % KW == 0
    kblk = QW // KW
    vblk = (QW + KW) // KW
    QB = ATTN_Q_BLOCKS
    assert nb % QB == 0
    ns = nb // QB
    n_heads_scores = n_q * 2 * W * W * 4
    est = 2 * QB * (W * QW * 4 + 2 * W * KW * 4 + W * QW * 2) + 4 * W * KW * 4 + 2 * n_heads_scores + LOOP_TEMPS_BYTES
    return pl.pallas_call(
        functools.partial(_attn_prompt_kernel, n_kv=n_kv, group=n_q // n_kv, qb=QB),
        grid=(B, ns),
        in_specs=[
            pl.BlockSpec(memory_space=pltpu.SMEM),
            pl.BlockSpec((QB * W, QW), lambda b, i: (b * ns + i, 0)),
            pl.BlockSpec((QB * W, KW), lambda b, i: (b * ns + i, kblk)),
            pl.BlockSpec((W, KW), lambda b, i: (b * nb + jnp.maximum(QB * i - 1, 0), kblk)),
            pl.BlockSpec((QB * W, KW), lambda b, i: (b * ns + i, vblk)),
            pl.BlockSpec((W, KW), lambda b, i: (b * nb + jnp.maximum(QB * i - 1, 0), vblk)),
        ],
        out_specs=pl.BlockSpec((QB * W, QW), lambda b, i: (b * ns + i, 0)),
        out_shape=jax.ShapeDtypeStruct((B * L, QW), BF16),
        compiler_params=pltpu.CompilerParams(
            dimension_semantics=("parallel", "parallel"),
            vmem_limit_bytes=_vmem_limit(est)),
        name="swa_prompt",
    )(sinks, p, p, p, p, p)


def _attn_sample_kernel(sinks_ref, qkv_ref, ck_ref, cv_ref, o_ref, *, n_q, n_kv, n_new):
    R = qkv_ref.shape[1]
    W = WINDOW
    HD = HEAD_DIM
    group = n_q // n_kv
    npair = group // 2
    QW = n_q * HD
    KW = n_kv * HD
    RR = npair * R
    scale = HD ** -0.5
    row = lax.broadcasted_iota(jnp.int32, (RR, 2 * W), 0)
    r = row & (R - 1)
    j = lax.broadcasted_iota(jnp.int32, (RR, 2 * W), 1)
    mask = (j >= r) & (j <= r + W) & (j < W + n_new)
    pair_of_row = lax.broadcasted_iota(jnp.int32, (RR, 1), 0) // R
    first = lax.broadcasted_iota(jnp.int32, (2 * W, LANES), 1) < HD
    zpad = jnp.zeros((W - R, LANES), F32)

    s_all, v_all = [], []
    for c in range(n_kv // 2):
        cs = slice(c * LANES, (c + 1) * LANES)
        kn = qkv_ref[0, :, QW + c * LANES:QW + (c + 1) * LANES]
        vn = qkv_ref[0, :, QW + KW + c * LANES:QW + KW + (c + 1) * LANES]
        ktile = jnp.concatenate([ck_ref[0, :, cs], kn, zpad], axis=0)
        vtile = jnp.concatenate([cv_ref[0, :, cs], vn, zpad], axis=0)
        kswap = pltpu.roll(ktile, HD, 1)
        vswap = pltpu.roll(vtile, HD, 1)
        for par in range(2):
            kh = 2 * c + par
            k_lo, k_hi = (ktile, kswap) if par == 0 else (kswap, ktile)
            v_lo, v_hi = (vtile, vswap) if par == 0 else (vswap, vtile)
            kcat = jnp.concatenate([jnp.where(first, k_lo, 0.0), jnp.where(first, 0.0, k_hi)], axis=0)
            vcat = jnp.concatenate([jnp.where(first, v_lo, 0.0), jnp.where(first, 0.0, v_hi)], axis=0)
            qs = jnp.concatenate(
                [qkv_ref[0, :, (kh * group + 2 * pr) * HD:(kh * group + 2 * pr) * HD + LANES]
                 for pr in range(npair)], axis=0) * scale
            s_all.append(_dot_nt(qs.astype(BF16), kcat.astype(BF16)))
            v_all.append(vcat.astype(BF16))

    p_all = []
    for kh in range(n_kv):
        halves = []
        for hf in range(2):
            sink = jnp.full((RR, 1), sinks_ref[kh * group + hf], F32)
            for pr in range(1, npair):
                sink = jnp.where(pair_of_row == pr, sinks_ref[kh * group + 2 * pr + hf], sink)
            sm = jnp.where(mask, s_all[kh][:, hf * 2 * W:(hf + 1) * 2 * W], -jnp.inf)
            (p,) = _sink_softmax([sm], sink)
            halves.append(p.astype(BF16))
        p_all.append(jnp.concatenate(halves, axis=1))

    for kh in range(n_kv):
        o = _dot(p_all[kh], v_all[kh])
        for pr in range(npair):
            c0 = (kh * group + 2 * pr) * HD
            o_ref[0, :, c0:c0 + LANES] = o[pr * R:(pr + 1) * R]


def _attn_sample(qkv, ck, cv, layer, sinks, n_q, n_kv, n_new):
    B, R, NW = qkv.shape
    assert R & (R - 1) == 0 and n_new <= R and 2 * HEAD_DIM == LANES and ck.shape[2] == WINDOW
    QW = n_q * HEAD_DIM
    KW = n_kv * HEAD_DIM
    return pl.pallas_call(
        functools.partial(_attn_sample_kernel, n_q=n_q, n_kv=n_kv, n_new=n_new),
        grid=(B,),
        in_specs=[
            pl.BlockSpec(memory_space=pltpu.SMEM),
            pl.BlockSpec((1, R, NW), lambda b: (b, 0, 0)),
            pl.BlockSpec((None, 1, WINDOW, KW), lambda b: (layer, b, 0, 0)),
            pl.BlockSpec((None, 1, WINDOW, KW), lambda b: (layer, b, 0, 0)),
        ],
        out_specs=pl.BlockSpec((1, R, QW), lambda b: (b, 0, 0)),
        out_shape=jax.ShapeDtypeStruct((B, R, QW), F32),
        compiler_params=pltpu.CompilerParams(dimension_semantics=("parallel",)),
        name="swa_sample",
    )(sinks, qkv, ck, cv)


def _rope_tables(pos):
    half = HEAD_DIM // 2
    inv = jnp.power(ROPE_THETA, -jnp.arange(half, dtype=F32) * (2.0 / HEAD_DIM))
    ang = pos.astype(F32)[:, None] * inv[None, :]
    cos = jnp.cos(ang)
    sin = jnp.sin(ang)
    reps = LANES // HEAD_DIM
    cos_t = jnp.tile(jnp.concatenate([cos, cos], axis=1), (1, reps))
    sin_t = jnp.tile(jnp.concatenate([-sin, sin], axis=1), (1, reps))
    return (jnp.stack([cos_t, jnp.ones_like(cos_t)]), jnp.stack([sin_t, jnp.zeros_like(sin_t)]))


def kernel(x_prompt, x_sample, state_hgrn, cache_k_win, cache_v_win, norm_gains, w_ffn_in, w_ffn_out,
           w_hgrn_in, hgrn_lb_logits, hgrn_norm_gain, w_hgrn_out, w_attn_in, attn_sinks, w_attn_out):
    B, L, D = x_prompt.shape
    SB, SL, _ = x_sample.shape
    depth = norm_gains.shape[0]
    TP, TS = B * L, SB * SL
    H = hgrn_norm_gain.shape[1] // HG_DV
    n_kv = cache_k_win.shape[3]
    n_q = w_attn_out.shape[1] // HEAD_DIM
    QW, KW = n_q * HEAD_DIM, n_kv * HEAD_DIM
    buf = cache_k_win.shape[2]

    x = (x_prompt.reshape(TP, D), x_sample.reshape(TS, D))

    lb_all = jnp.cumsum(jax.nn.softmax(hgrn_lb_logits.astype(F32), axis=0), axis=0)
    lb_all = lb_all - lb_all[0:1]

    pos = jnp.concatenate([jnp.tile(jnp.arange(L, dtype=jnp.int32), B),
                           jnp.tile(PAST_LEN + jnp.arange(SL, dtype=jnp.int32), SB)])
    rope = _rope_tables(pos)

    hg_p, kp_l, vp_l, ks_l, vs_l = [], [], [], [], []
    hg_s = None
    for layer in range(depth):
        g = norm_gains[layer]
        x = _ffn_half(x, g[0:1], g[1:2], w_ffn_in, w_ffn_out, layer, 0)
        if layer % N_MIXERS == 0:
            a = layer // N_MIXERS
            lb = lb_all[a][None, :]
            lbp = (jnp.log(lb), jnp.log1p(-lb), 1.0 - lb)
            ng = hgrn_norm_gain[a][None, :].astype(F32)
            o_p, s_p = _hgrn_fused(x, g[2:3], w_hgrn_in, a, B, L, lbp, ng)
            o_p = o_p.reshape(TP, -1)
            p_s = _proj(x[TP:], g[2:3], w_hgrn_in, a, (TS, HGRN_SAMPLE_PROJ_TN))
            ps = jnp.pad(p_s.reshape(SB, SL, -1), ((0, 0), (0, SAMPLE_PAD - SL), (0, 0)))
            o_s, hg_s = _hgrn_scan(ps.reshape(SB * SAMPLE_PAD, -1), SB, SAMPLE_PAD, SAMPLE_PAD, H, SL,
                                   lbp, ng, (state_hgrn, a), F32, prev=hg_s)
            o_s = o_s.reshape(SB, SAMPLE_PAD, -1)[:, :SL].reshape(TS, -1).astype(BF16)
            x = _outproj(o_p, o_s, w_hgrn_out, x, g[3:4], a)
            hg_p.append(s_p.astype(state_hgrn.dtype))
        else:
            bb = layer // N_MIXERS
            p = _proj(x, g[2:3], w_attn_in, bb, ATTN_PROJ_TILES, rope=rope, n_rope_cols=QW + KW)
            sinks = attn_sinks[bb].astype(F32)
            o_p = _attn_prompt(p, sinks, B, L, n_q, n_kv)
            pn = p[TP:].reshape(SB, SL, -1)
            ck = cache_k_win.reshape(-1, SB, buf, KW)
            cv = cache_v_win.reshape(-1, SB, buf, KW)
            pn_rows = jnp.pad(pn, ((0, 0), (0, -SL % SUBLANES), (0, 0)))
            o_s = _attn_sample(pn_rows, ck, cv, bb, sinks, n_q, n_kv, SL)
            x = _outproj(o_p, o_s[:, :SL].reshape(TS, QW).astype(BF16), w_attn_out, x, g[3:4], bb)
            tails = [p[(s + 1) * L - buf:(s + 1) * L, QW:] for s in range(B)]
            kp_l.append(jnp.stack([t[:, :KW] for t in tails]).reshape(B, buf, n_kv, HEAD_DIM))
            vp_l.append(jnp.stack([t[:, KW:] for t in tails]).reshape(B, buf, n_kv, HEAD_DIM))
            k_new = pn[:, :, QW:QW + KW].reshape(SB, SL, n_kv, HEAD_DIM)
            v_new = pn[:, :, QW + KW:].reshape(SB, SL, n_kv, HEAD_DIM)
            ks_l.append(k_new)
            vs_l.append(v_new)
        x = _ffn_half(x, g[4:5], g[5:6], w_ffn_in, w_ffn_out, layer, 1,
                      out_split=TP if layer == depth - 1 else None)

    yp = x[0].reshape(B, L, D)
    ys = x[1].reshape(SB, SL, D)
    k_win_s = jnp.concatenate([cache_k_win, jnp.stack(ks_l).astype(cache_k_win.dtype)], axis=2)[:, :, -buf:]
    v_win_s = jnp.concatenate([cache_v_win, jnp.stack(vs_l).astype(cache_v_win.dtype)], axis=2)[:, :, -buf:]
    return (yp, ys, jnp.stack(hg_p), hg_s.astype(state_hgrn.dtype), jnp.stack(kp_l), jnp.stack(vp_l),
            k_win_s, v_win_s)
```

```python
import functools

import jax
import jax.numpy as jnp
from jax import lax
from jax.experimental import pallas as pl
from jax.experimental.pallas import tpu as pltpu

F32 = jnp.float32
BF16 = jnp.bfloat16

NORM_EPS = 1e-6
HEAD_DIM = 64
HG_DK = 128
HG_DV = 128
HG_CHUNK = 64
WINDOW = 128
PAST_LEN = 16384
ROPE_THETA = 10000.0
N_MIXERS = 2

LANES = 128
SUBLANES = 8
MIB = 1024 * 1024
VMEM_BYTES_V7X = 64 * MIB
VMEM_HEADROOM = 6 * MIB
VMEM_MIN_REQUEST = 16 * MIB
LOOP_TEMPS_BYTES = 8 * MIB

FFN_TM, FFN_TF = 1040, 256
FFN_RC = 208
HGRN_SAMPLE_PROJ_TN = 1024
FUSED_HB, FUSED_NSEQ, FUSED_RB = 2, 2, 256
ATTN_PROJ_TILES = (2080, 512)
PROJ_X_DOUBLE_BUFFER_MAX_BYTES = 12 * MIB
ROPE_ROW_CHUNK = 208
OUT_TM = 512
ATTN_Q_BLOCKS = 4
SAMPLE_PAD = 16


def _vmem_limit(nbytes):
    return int(min(VMEM_BYTES_V7X - VMEM_HEADROOM, max(nbytes * 5 // 4, VMEM_MIN_REQUEST)))


def _rms(x, g):
    return x * lax.rsqrt(jnp.mean(x * x, axis=-1, keepdims=True) + NORM_EPS) * g


def _dot(a, b):
    return jnp.dot(a, b, preferred_element_type=F32)


def _dot_nt(a, b):
    return lax.dot_general(a, b, (((1,), (1,)), ((), ())), preferred_element_type=F32)


def _dot_tn(a, b):
    return lax.dot_general(a, b, (((0,), (0,)), ((), ())), preferred_element_type=F32)


def _ffn_kernel(*refs, nt, nj, tm, rc, in_split, out_split):
    refs = list(refs)
    x_refs = [refs.pop(0) for _ in range(1 if in_split is None else 2)]
    gpre_ref, gpost_ref, wg_ref, wu_ref, wo_ref = (refs.pop(0) for _ in range(5))
    y_refs = [refs.pop(0) for _ in range(1 if out_split is None else 2)]
    acc_ref, xn_ref, xc_ref, yc_ref, in_sem, out_sem = refs

    i = pl.program_id(0)
    j = pl.program_id(1)
    nch = tm // rc
    slot = i % 2
    pslot = 1 - slot
    k = j - 1
    gpre = gpre_ref[...]
    gpost_half = 0.5 * gpost_ref[...]

    def in_copy(row0, b):
        return pltpu.make_async_copy(x_refs[0].at[pl.ds(row0, rc)], xc_ref.at[b], in_sem.at[b])

    def out_copy(row0, b):
        return pltpu.make_async_copy(yc_ref.at[b], y_refs[0].at[pl.ds(row0, rc)], out_sem.at[b])

    def crossing_row(split):
        if split is None:
            return None
        assert split % rc != 0 and nt * tm - split < rc, "exactly one chunk must cross the split point"
        return (split // rc) * rc

    in_cross = crossing_row(in_split)
    out_cross = crossing_row(out_split)
    assert out_cross is None or out_cross >= (nt - 1) * tm, "only the last row tile may cross the output split"

    def in_copies_crossing(b):
        na = in_split - in_cross
        return [pltpu.make_async_copy(x_refs[0].at[pl.ds(in_cross, na)], xc_ref.at[b, pl.ds(0, na)], in_sem.at[b]),
                pltpu.make_async_copy(x_refs[1].at[pl.ds(0, rc - na)], xc_ref.at[b, pl.ds(na, rc - na)],
                                      in_sem.at[b])]

    def out_copies_crossing(b):
        na = out_split - out_cross
        return [pltpu.make_async_copy(yc_ref.at[b, pl.ds(0, na)], y_refs[0].at[pl.ds(out_cross, na)], out_sem.at[b]),
                pltpu.make_async_copy(yc_ref.at[b, pl.ds(na, rc - na)], y_refs[1].at[pl.ds(0, rc - na)],
                                      out_sem.at[b])]

    def static_in_copies(row0, b):
        return in_copies_crossing(b) if row0 == in_cross else [in_copy(row0, b)]

    def static_out_copies(row0, b):
        return out_copies_crossing(b) if row0 == out_cross else [out_copy(row0, b)]

    def task_row(t):
        return jnp.where(t < nch, (i - 1) * tm + t * rc, (i + 1) * tm + (t - nch) * rc)

    def task_live(t):
        return (t >= 0) & (t < 2 * nch) & jnp.where(t < nch, i >= 1, i <= nt - 2)

    def task_crosses(t):
        return False if in_cross is None else task_row(t) == in_cross

    @pl.when((i == 0) & (j == 0))
    def _():
        for c in range(nch):
            cps = static_in_copies(c * rc, c % 2)
            for cp in cps:
                cp.start()
            for cp in cps:
                cp.wait()
            xn_ref[0, c * rc:(c + 1) * rc] = _rms(xc_ref[c % 2], gpre).astype(BF16)

    @pl.when(task_live(k) & jnp.logical_not(task_crosses(k)))
    def _():
        in_copy(0, k & 1).wait()

    @pl.when(task_live(k) & (k >= 2) & (k < nch))
    def _():
        out_copy(0, k & 1).wait()

    @pl.when((j == nch + 1) & (i >= 1))
    def _():
        out_copy(0, 0).wait()
        out_copy(0, 1).wait()

    @pl.when(task_live(j) & jnp.logical_not(task_crosses(j)))
    def _():
        in_copy(task_row(j), j & 1).start()

    if in_cross is not None:
        @pl.when(task_live(k) & task_crosses(k))
        def _():
            for cp in in_copies_crossing(k & 1):
                cp.wait()

        @pl.when(task_live(j) & task_crosses(j))
        def _():
            for cp in in_copies_crossing(j & 1):
                cp.start()

    def accumulate(first):
        xn = xn_ref[slot]
        hg = _dot(xn, wg_ref[...].astype(BF16))
        hu = _dot(xn, wu_ref[...].astype(BF16))
        act = (hg * jax.nn.sigmoid(hg) * hu).astype(BF16)
        contrib = _dot(act, wo_ref[...].astype(BF16))
        if first:
            acc_ref[slot] = contrib
        else:
            acc_ref[slot] += contrib

    is_epi = (k >= 0) & (k < nch) & (i >= 1)
    is_pro = (k >= nch) & (k < 2 * nch) & (i <= nt - 2)

    @pl.when(j == 0)
    def _():
        accumulate(True)

    @pl.when(is_epi)
    def _():
        accumulate(False)
        b = k & 1
        rows = pl.ds(pl.multiple_of(k * rc, rc), rc)
        yc_ref[b] = xc_ref[b] + _rms(acc_ref[pslot, rows], gpost_half)
        out_copy((i - 1) * tm + k * rc, b).start()

    @pl.when(is_pro)
    def _():
        accumulate(False)
        rows = pl.ds(pl.multiple_of((k - nch) * rc, rc), rc)
        xn_ref[pslot, rows] = _rms(xc_ref[k & 1], gpre).astype(BF16)

    @pl.when((j >= 1) & jnp.logical_not(is_epi) & jnp.logical_not(is_pro))
    def _():
        accumulate(False)

    @pl.when((i == nt - 1) & (j == nj - 1))
    def _():
        last = (nt - 1) % 2
        for c in range(nch):
            b = c % 2
            row0 = (nt - 1) * tm + c * rc
            cps = static_in_copies(row0, b)
            for cp in cps:
                cp.start()
            for cp in cps:
                cp.wait()
            yc_ref[b] = xc_ref[b] + _rms(acc_ref[last, c * rc:(c + 1) * rc], gpost_half)
            ocs = static_out_copies(row0, b)
            for oc in ocs:
                oc.start()
            for oc in ocs:
                oc.wait()


def _ffn_half(x, g_pre, g_post, w_in, w_out, layer, half, out_split=None):
    xs = x if isinstance(x, (tuple, list)) else (x,)
    in_split = xs[0].shape[0] if len(xs) == 2 else None
    T = sum(a.shape[0] for a in xs)
    D = xs[0].shape[1]
    F = w_out.shape[2]
    TM, TF, RC = FFN_TM, FFN_TF, FFN_RC
    nj = F // TF
    nt = T // TM
    assert T % TM == 0 and F % TF == 0 and TM % RC == 0 and TM // RC >= 2 and nj >= 2 * (TM // RC) + 2
    est = (2 * TM * D * (4 + 2) + 4 * RC * D * 4 + 2 * 3 * D * TF * 4
           + 3 * D * TF * 2 + 4 * TM * TF * 4 + TM * D * 4)
    any_spec = pl.BlockSpec(memory_space=pl.ANY)
    if out_split is None:
        out_specs, out_shape = any_spec, jax.ShapeDtypeStruct((T, D), F32)
    else:
        out_specs = [any_spec, any_spec]
        out_shape = [jax.ShapeDtypeStruct((out_split, D), F32), jax.ShapeDtypeStruct((T - out_split, D), F32)]
    return pl.pallas_call(
        functools.partial(_ffn_kernel, nt=nt, nj=nj, tm=TM, rc=RC, in_split=in_split, out_split=out_split),
        grid=(nt, nj),
        in_specs=[any_spec] * len(xs) + [
            pl.BlockSpec((1, D), lambda i, j: (0, 0)),
            pl.BlockSpec((1, D), lambda i, j: (0, 0)),
            pl.BlockSpec((None, None, D, TF), lambda i, j: (layer, half, 0, j)),
            pl.BlockSpec((None, None, D, TF), lambda i, j: (layer, half, 0, nj + j)),
            pl.BlockSpec((None, None, TF, D), lambda i, j: (layer, half, j, 0)),
        ],
        out_specs=out_specs,
        out_shape=out_shape,
        scratch_shapes=[pltpu.VMEM((2, TM, D), F32), pltpu.VMEM((2, TM, D), BF16),
                        pltpu.VMEM((2, RC, D), F32), pltpu.VMEM((2, RC, D), F32),
                        pltpu.SemaphoreType.DMA((2,)), pltpu.SemaphoreType.DMA((2,))],
        compiler_params=pltpu.CompilerParams(
            dimension_semantics=("arbitrary", "arbitrary"),
            vmem_limit_bytes=_vmem_limit(est)),
        name="ffn_half",
    )(*xs, g_pre, g_post, w_in, w_in, w_out)


def _rope_tile(y, cos, sin_signed):
    lane = lax.broadcasted_iota(jnp.int32, y.shape, 1)
    first_half = (lane & (HEAD_DIM - 1)) < (HEAD_DIM // 2)
    rot = jnp.where(first_half,
                    pltpu.roll(y, LANES - HEAD_DIM // 2, 1),
                    pltpu.roll(y, HEAD_DIM // 2, 1))
    return y * cos + rot * sin_signed


def _proj_kernel(x_ref, g_ref, w_ref, o_ref, xn_ref):
    @pl.when(pl.program_id(1) == 0)
    def _():
        xn_ref[...] = _rms(x_ref[...], g_ref[...]).astype(BF16)

    o_ref[...] = _dot(xn_ref[...], w_ref[...].astype(BF16))


def _proj_rope_kernel(x_ref, g_ref, w_ref, cos_ref, sin_ref, o_ref, xn_ref, *, row_chunk):
    @pl.when(pl.program_id(1) == 0)
    def _():
        xn_ref[...] = _rms(x_ref[...], g_ref[...]).astype(BF16)

    wb = w_ref[...].astype(BF16)
    for r0 in range(0, xn_ref.shape[0], row_chunk):
        rows = slice(r0, r0 + row_chunk)
        y = _dot(xn_ref[rows], wb)
        cos = cos_ref[rows]
        sin = sin_ref[rows]
        for c in range(y.shape[1] // LANES):
            cols = slice(c * LANES, (c + 1) * LANES)
            o_ref[rows, cols] = _rope_tile(y[:, cols], cos, sin)


def _proj(x, g, w, layer, tiles, rope=None, n_rope_cols=0):
    T, D = x.shape
    N = w.shape[2]
    TM, TN = tiles
    assert T % TM == 0 and N % TN == 0 and n_rope_cols % TN == 0
    x_buffers = 1 if TM * D * 4 > PROJ_X_DOUBLE_BUFFER_MAX_BYTES else 2
    est = x_buffers * TM * D * 4 + TM * D * 2 + 2 * D * TN * 4 + D * TN * 2 + 3 * TM * TN * 4
    in_specs = [
        pl.BlockSpec((TM, D), lambda i, j: (i, 0), pipeline_mode=pl.Buffered(x_buffers)),
        pl.BlockSpec((1, D), lambda i, j: (0, 0)),
        pl.BlockSpec((None, D, TN), lambda i, j: (layer, 0, j)),
    ]
    args = [x, g, w]
    if rope is None:
        body = _proj_kernel
    else:
        assert TM % ROPE_ROW_CHUNK == 0
        body = functools.partial(_proj_rope_kernel, row_chunk=ROPE_ROW_CHUNK)
        n_rope_blocks = n_rope_cols // TN
        table = pl.BlockSpec((None, TM, LANES), lambda i, j: (jnp.where(j < n_rope_blocks, 0, 1), i, 0))
        in_specs += [table, table]
        args += list(rope)
    return pl.pallas_call(
        body,
        grid=(T // TM, N // TN),
        in_specs=in_specs,
        out_specs=pl.BlockSpec((TM, TN), lambda i, j: (i, j)),
        out_shape=jax.ShapeDtypeStruct((T, N), F32),
        scratch_shapes=[pltpu.VMEM((TM, D), BF16)],
        compiler_params=pltpu.CompilerParams(
            dimension_semantics=("parallel", "arbitrary"),
            vmem_limit_bytes=_vmem_limit(est)),
        name="mixer_in_proj",
    )(*args)


def _outproj_kernel(op_ref, os_ref, w_ref, x_ref, g_ref, y_ref, wbf_ref, *, n_prompt_tiles, ts):
    i = pl.program_id(0)

    @pl.when(i == 0)
    def _():
        wbf_ref[...] = w_ref[...].astype(BF16)

    @pl.when(i < n_prompt_tiles)
    def _():
        y_ref[...] = x_ref[...] + _rms(_dot(op_ref[...], wbf_ref[...]), g_ref[...])

    @pl.when(i == n_prompt_tiles)
    def _():
        y_ref[0:ts] = x_ref[0:ts] + _rms(_dot(os_ref[...], wbf_ref[...]), g_ref[...])


def _outproj(o_prompt, o_sample, w, x, g, layer):
    T, D = x.shape
    TP, K = o_prompt.shape
    TS = o_sample.shape[0]
    TM = OUT_TM
    assert TP % TM == 0 and TS <= TM and TS % SUBLANES == 0 and TP + TS == T
    npt = TP // TM
    est = K * D * (4 + 2) + 2 * TM * K * 2 + 2 * TS * K * 2 + 5 * TM * D * 4
    return pl.pallas_call(
        functools.partial(_outproj_kernel, n_prompt_tiles=npt, ts=TS),
        grid=(npt + 1,),
        in_specs=[
            pl.BlockSpec((TM, K), lambda i: (jnp.minimum(i, npt - 1), 0)),
            pl.BlockSpec((TS, K), lambda i: (0, 0)),
            pl.BlockSpec((None, K, D), lambda i: (layer, 0, 0), pipeline_mode=pl.Buffered(1)),
            pl.BlockSpec((TM, D), lambda i: (i, 0)),
            pl.BlockSpec((1, D), lambda i: (0, 0)),
        ],
        out_specs=pl.BlockSpec((TM, D), lambda i: (i, 0)),
        out_shape=jax.ShapeDtypeStruct((T, D), F32),
        scratch_shapes=[pltpu.VMEM((K, D), BF16)],
        compiler_params=pltpu.CompilerParams(
            dimension_semantics=("arbitrary",),
            vmem_limit_bytes=_vmem_limit(est)),
        name="mixer_out_proj",
    )(o_prompt, o_sample, w, x, g)


def _tiles(x):
    return [x[SUBLANES * j:SUBLANES * (j + 1)] for j in range(x.shape[0] // SUBLANES)]


def _bcast_row(t, i):
    return jnp.broadcast_to(t[i:i + 1], t.shape)


def _tile_prefix(t, sub):
    p = t + jnp.where(sub >= 1, pltpu.roll(t, 1, 0), 0.0)
    p = p + jnp.where(sub >= 2, pltpu.roll(p, 2, 0), 0.0)
    return p + jnp.where(sub >= 4, pltpu.roll(p, 4, 0), 0.0)


def _hgrn_gates(fz, loglb, log1m, onem, valid):
    e = jnp.exp(-jnp.abs(fz))
    lsig = jnp.minimum(fz, 0.0) - jnp.log(1.0 + e)
    ct = log1m + lsig
    g = jnp.maximum(loglb, ct) + jnp.log(1.0 + jnp.exp(-jnp.abs(loglb - ct)))
    r = 1.0 / (1.0 + e)
    k = onem * jnp.where(fz > 0.0, e * r, r)
    if valid is not None:
        g = jnp.where(valid, g, 0.0)
        k = jnp.where(valid, k, 0.0)
    return g, k


def _hgrn_level_operands(m, qt, kt, gt, bt, sub):
    nv = len(qt)
    zero = jnp.zeros_like(qt[0])
    qm, km = [], []
    if m >= SUBLANES:
        mv = m // SUBLANES
        for j in range(nv):
            jr = (j // (2 * mv)) * (2 * mv) + mv - 1
            ref = _bcast_row(bt[jr], SUBLANES - 1)
            if j % (2 * mv) >= mv:
                qm.append(qt[j] * jnp.exp(bt[j] - ref))
                km.append(zero)
            else:
                qm.append(zero)
                km.append(kt[j] * jnp.exp(ref - bt[j]))
        return qm, km
    upper = (sub & m) != 0
    for j in range(nv):
        if m == 1:
            e = jnp.exp(gt[j])
            qm.append(jnp.where(upper, qt[j] * e, 0.0))
            km.append(jnp.where(upper, 0.0, kt[j]))
            continue
        if m == 4:
            ref = _bcast_row(bt[j], 3)
        else:
            ref = jnp.where(sub < 4, _bcast_row(bt[j], 1), _bcast_row(bt[j], 5))
        e = jnp.exp(-jnp.abs(bt[j] - ref))
        qm.append(jnp.where(upper, qt[j] * e, 0.0))
        km.append(jnp.where(upper, 0.0, kt[j] * e))
    return qm, km


def _hgrn_chunk_group(loads, params, sts, *, C, valid_len):
    nh = len(loads)
    sub = lax.broadcasted_iota(jnp.int32, (SUBLANES, LANES), 0)
    valid = None
    if valid_len < C:
        valid = lax.broadcasted_iota(jnp.int32, (C, LANES), 0) < valid_len
    ti = lax.broadcasted_iota(jnp.int32, (C, C), 0)
    si = lax.broadcasted_iota(jnp.int32, (C, C), 1)

    gk = [_hgrn_gates(loads[h][1], params[h][0], params[h][1], params[h][2], valid) for h in range(nh)]

    gts, bts = [], []
    for h in range(nh):
        gt = _tiles(gk[h][0])
        bt = []
        for j, t in enumerate(gt):
            p = _tile_prefix(t, sub)
            bt.append(p if j == 0 else p + _bcast_row(bt[j - 1], SUBLANES - 1))
        gts.append(gt)
        bts.append(bt)

    scores = []
    for h in range(nh):
        q, k = loads[h][0], gk[h][1]
        qt, kt = _tiles(q), _tiles(k)
        a = jnp.where(ti == si, _dot_nt(q.astype(BF16), k.astype(BF16)), 0.0)
        m = C // 2
        while m >= 1:
            qm, km = _hgrn_level_operands(m, qt, kt, gts[h], bts[h], sub)
            am = _dot_nt(jnp.concatenate(qm, axis=0).astype(BF16), jnp.concatenate(km, axis=0).astype(BF16))
            if 2 * m < C:
                am = jnp.where((ti & (-2 * m)) == (si & (-2 * m)), am, 0.0)
            a = a + am
            m //= 2
        scores.append(a)

    outs = []
    for h in range(nh):
        q, _, v, gz = loads[h]
        k = gk[h][1]
        ng = params[h][3]
        st = sts[h]
        b = jnp.concatenate(bts[h], axis=0)
        o = _dot(scores[h].astype(BF16), v.astype(BF16)) + _dot_nt((q * jnp.exp(b)).astype(BF16), st.astype(BF16))
        bl = b[C - 1:C]
        kb = (k * jnp.exp(bl - b)).astype(BF16)
        st_new = st * jnp.exp(bl) + _dot_tn(v.astype(BF16), kb)
        on = o * lax.rsqrt(jnp.mean(o * o, axis=-1, keepdims=True) + NORM_EPS) * ng
        outs.append((on * (gz * jax.nn.sigmoid(gz)), st_new))
    return outs


def _hgrn_kernel(*refs, C, NC, HB, valid_len, has_s0, n_prev):
    refs = list(refs)
    q_ref, f_ref, i_ref, g_ref, loglb_ref, log1m_ref, onem_ref, ng_ref = (refs.pop(0) for _ in range(8))
    s0_ref = refs.pop(0) if has_s0 else None
    prev_ref = refs.pop(0) if n_prev else None
    o_ref, s_ref, st_ref = refs

    for h in range(HB):
        if has_s0:
            st_ref[h] = s0_ref[0, h].T
        else:
            st_ref[h] = jnp.zeros((HG_DV, HG_DK), F32)

    def body(c, carry):
        rows = pl.ds(pl.multiple_of(c * C, C), C)
        cols = [slice(h * LANES, (h + 1) * LANES) for h in range(HB)]
        loads = [(q_ref[rows, cs], f_ref[rows, cs], i_ref[rows, cs], g_ref[rows, cs]) for cs in cols]
        params = [(loglb_ref[:, cs], log1m_ref[:, cs], onem_ref[:, cs], ng_ref[:, cs]) for cs in cols]
        outs = _hgrn_chunk_group(loads, params, [st_ref[h] for h in range(HB)], C=C, valid_len=valid_len)
        for h in range(HB):
            st_ref[h] = outs[h][1]
            o_ref[rows, cols[h]] = outs[h][0].astype(o_ref.dtype)
        return carry

    if NC == 1:
        body(0, 0)
    else:
        lax.fori_loop(0, NC, body, 0)

    for layer in range(n_prev):
        s_ref[layer] = prev_ref[layer]
    for h in range(HB):
        s_ref[n_prev, 0, h] = st_ref[h].T


def _hgrn_scan(p, nseq, L, C, HB, valid_len, lbp, ng, s0, out_dtype, prev=None):
    H = ng.shape[1] // LANES
    nhb = H // HB
    W = HB * LANES
    NC = L // C
    loglb, log1m, onem = lbp
    has_s0 = s0 is not None

    def sect(s):
        return pl.BlockSpec((L, W), lambda b, hb, s=s: (b, s * nhb + hb))

    vec = pl.BlockSpec((1, W), lambda b, hb: (0, hb))
    in_specs = [sect(0), sect(1), sect(2), sect(3), vec, vec, vec, vec]
    args = [p, p, p, p, loglb, log1m, onem, ng]
    if has_s0:
        s0_all, s0_layer = s0
        in_specs.append(pl.BlockSpec((None, 1, HB, HG_DK, HG_DV), lambda b, hb: (s0_layer, b, hb, 0, 0)))
        args.append(s0_all)
    n_prev = 0 if prev is None else prev.shape[0]
    if n_prev:
        in_specs.append(pl.BlockSpec((n_prev, 1, HB, HG_DK, HG_DV), lambda b, hb: (0, b, hb, 0, 0)))
        args.append(prev)
    est = (2 * 4 * L * W * 4 + 2 * L * W * 4 + (6 + 4 * (2 * n_prev + 1)) * HB * HG_DK * HG_DV * 4
           + LOOP_TEMPS_BYTES)
    return pl.pallas_call(
        functools.partial(_hgrn_kernel, C=C, NC=NC, HB=HB, valid_len=valid_len, has_s0=has_s0, n_prev=n_prev),
        grid=(nseq, nhb),
        in_specs=in_specs,
        out_specs=[pl.BlockSpec((L, W), lambda b, hb: (b, hb)),
                   pl.BlockSpec((n_prev + 1, 1, HB, HG_DK, HG_DV), lambda b, hb: (0, b, hb, 0, 0))],
        out_shape=[jax.ShapeDtypeStruct((nseq * L, H * LANES), out_dtype),
                   jax.ShapeDtypeStruct((n_prev + 1, nseq, H, HG_DK, HG_DV), F32)],
        scratch_shapes=[pltpu.VMEM((HB, HG_DV, HG_DK), F32)],
        compiler_params=pltpu.CompilerParams(
            dimension_semantics=("parallel", "parallel"),
            vmem_limit_bytes=_vmem_limit(est)),
        name="hgrn_scan",
    )(*args)


def _hgrn_fused_kernel(x_hbm, g_ref, wq_ref, wf_ref, wi_ref, wg_ref, loglb_ref, log1m_ref, onem_ref, ng_ref,
                       o_ref, s_ref, xn_ref, xc_ref, proj_ref, st_ref, in_sem, *, L, C, RB, NSEQ, HB):
    sg = pl.program_id(0)
    hb = pl.program_id(1)
    D = xn_ref.shape[1]
    WC = HB * LANES
    NB = L // RB
    CPB = RB // C
    w_refs = (wq_ref, wf_ref, wi_ref, wg_ref)
    assert CPB == len(w_refs)

    @pl.when(hb == 0)
    def _():
        g = g_ref[...]
        nchunk = NSEQ * NB

        def copy(c):
            return pltpu.make_async_copy(x_hbm.at[pl.ds(sg * (NSEQ * L) + c * RB, RB)],
                                         xc_ref.at[c % 2], in_sem.at[c % 2])
        copy(0).start()
        for c in range(nchunk):
            copy(c).wait()
            if c + 1 < nchunk:
                copy(c + 1).start()
            xn_ref[c * RB:(c + 1) * RB] = _rms(xc_ref[c % 2], g).astype(BF16)

    def project(rb, section, slot):
        rows = [xn_ref[pl.ds(pl.multiple_of(s * L + rb * RB, RB), RB)] for s in range(NSEQ)]
        y = _dot(jnp.concatenate(rows, axis=0), w_refs[section][...].astype(BF16))
        proj_ref[slot, :, section * WC:(section + 1) * WC] = y

    for ch in range(NSEQ * HB):
        st_ref[ch] = jnp.zeros((HG_DV, HG_DK), F32)
    for section in range(CPB):
        project(0, section, 0)

    params = [(loglb_ref[:, h * LANES:(h + 1) * LANES], log1m_ref[:, h * LANES:(h + 1) * LANES],
               onem_ref[:, h * LANES:(h + 1) * LANES], ng_ref[:, h * LANES:(h + 1) * LANES])
              for _ in range(NSEQ) for h in range(HB)]

    def block(rb, project_next):
        slot = rb % 2
        for ci in range(CPB):
            loads = []
            for s in range(NSEQ):
                rows = slice(s * RB + ci * C, s * RB + (ci + 1) * C)
                for h in range(HB):
                    loads.append(tuple(proj_ref[slot, rows, sec * WC + h * LANES:sec * WC + (h + 1) * LANES]
                                       for sec in range(4)))
            if project_next:
                project(rb + 1, ci, 1 - slot)
            outs = _hgrn_chunk_group(loads, params, [st_ref[ch] for ch in range(NSEQ * HB)], C=C, valid_len=C)
            orow = pl.ds(pl.multiple_of(rb * RB + ci * C, C), C)
            for s in range(NSEQ):
                for h in range(HB):
                    ch = s * HB + h
                    st_ref[ch] = outs[ch][1]
                    o_ref[s, orow, h * LANES:(h + 1) * LANES] = outs[ch][0].astype(o_ref.dtype)

    def body(rb, carry):
        block(rb, True)
        return carry

    lax.fori_loop(0, NB - 1, body, 0)
    block(NB - 1, False)

    for s in range(NSEQ):
        for h in range(HB):
            s_ref[s, h] = st_ref[s * HB + h].T


def _hgrn_fused(x, g, w, layer, B, L, lbp, ng):
    D = x.shape[1]
    H = ng.shape[1] // LANES
    HB, NSEQ, RB, C = FUSED_HB, FUSED_NSEQ, FUSED_RB, HG_CHUNK
    WC = HB * LANES
    nhb = H // HB
    assert B % NSEQ == 0 and H % HB == 0 and L % RB == 0 and RB // C == 4
    loglb, log1m, onem = lbp

    def sect(s):
        return pl.BlockSpec((None, D, WC), lambda sg, hb, s=s: (layer, 0, s * nhb + hb))

    vec = pl.BlockSpec((1, WC), lambda sg, hb: (0, hb))
    est = (NSEQ * L * D * 2 + 2 * RB * D * 4 + 2 * NSEQ * RB * 4 * WC * 4 + 2 * 4 * D * WC * 4
           + 2 * NSEQ * L * WC * 2 + LOOP_TEMPS_BYTES)
    return pl.pallas_call(
        functools.partial(_hgrn_fused_kernel, L=L, C=C, RB=RB, NSEQ=NSEQ, HB=HB),
        grid=(B // NSEQ, nhb),
        in_specs=[pl.BlockSpec(memory_space=pl.ANY), pl.BlockSpec((1, D), lambda sg, hb: (0, 0)),
                  sect(0), sect(1), sect(2), sect(3), vec, vec, vec, vec],
        out_specs=[pl.BlockSpec((NSEQ, L, WC), lambda sg, hb: (sg, 0, hb)),
                   pl.BlockSpec((NSEQ, HB, HG_DK, HG_DV), lambda sg, hb: (sg, hb, 0, 0))],
        out_shape=[jax.ShapeDtypeStruct((B, L, H * LANES), BF16),
                   jax.ShapeDtypeStruct((B, H, HG_DK, HG_DV), F32)],
        scratch_shapes=[pltpu.VMEM((NSEQ * L, D), BF16), pltpu.VMEM((2, RB, D), F32),
                        pltpu.VMEM((2, NSEQ * RB, 4 * WC), F32), pltpu.VMEM((NSEQ * HB, HG_DV, HG_DK), F32),
                        pltpu.SemaphoreType.DMA((2,))],
        compiler_params=pltpu.CompilerParams(
            dimension_semantics=("arbitrary", "arbitrary"),
            vmem_limit_bytes=_vmem_limit(est)),
        name="hgrn_fused",
    )(x, g, w, w, w, w, loglb, log1m, onem, ng)


def _sink_softmax(scores, sink):
    m = sink
    for s in scores:
        m = jnp.maximum(m, jnp.max(s, axis=-1, keepdims=True))
    ps = [jnp.exp(s - m) for s in scores]
    denom = jnp.exp(sink - m)
    for p in ps:
        denom = denom + jnp.sum(p, axis=-1, keepdims=True)
    return [p / denom for p in ps]


def _attn_prompt_kernel(sinks_ref, q_ref, kc_ref, kp_ref, vc_ref, vp_ref, o_ref, *, n_kv, group, qb):
    for u in range(qb):
        _attn_prompt_block(sinks_ref, q_ref, kc_ref, kp_ref, vc_ref, vp_ref, o_ref, u, n_kv=n_kv, group=group)


def _attn_prompt_block(sinks_ref, q_ref, kc_ref, kp_ref, vc_ref, vp_ref, o_ref, u, *, n_kv, group):
    i = pl.program_id(1)
    W = WINDOW
    HD = HEAD_DIM
    rows = slice(u * W, (u + 1) * W)
    prev = slice((u - 1) * W, u * W)
    j = lax.broadcasted_iota(jnp.int32, (2 * W, W), 0)
    r = lax.broadcasted_iota(jnp.int32, (2 * W, W), 1)
    first_key = jnp.where(i > 0, 0, W) if u == 0 else 0
    mask = (j >= jnp.maximum(r, first_key)) & (j <= r + W)
    scale = HD ** -0.5
    zeros = jnp.zeros((HD, W), F32)

    s_all, vt_all = [], []
    for c in range(n_kv // 2):
        cs = slice(c * LANES, (c + 1) * LANES)
        k_prev = kp_ref[:, cs] if u == 0 else kc_ref[prev, cs]
        v_prev = vp_ref[:, cs] if u == 0 else vc_ref[prev, cs]
        ktile = jnp.concatenate([k_prev, kc_ref[rows, cs]], axis=0).astype(BF16)
        vt_all.append(jnp.concatenate([v_prev, vc_ref[rows, cs]], axis=0).T.astype(BF16))
        for par in range(2):
            kh = 2 * c + par
            rhs = []
            for pr in range(group // 2):
                c0 = (kh * group + 2 * pr) * HD
                pt = (q_ref[rows, c0:c0 + LANES] * scale).T
                for half in range(2):
                    blk = pt[half * HD:(half + 1) * HD]
                    rhs.append(jnp.concatenate([blk, zeros] if par == 0 else [zeros, blk], axis=0))
            s_all.append(_dot(ktile, jnp.concatenate(rhs, axis=1).astype(BF16)))

    p_all, inv_all = [], []
    for kh in range(n_kv):
        pn, inv = [], []
        for hh in range(group):
            sink = sinks_ref[kh * group + hh]
            sm = jnp.where(mask, s_all[kh][:, hh * W:(hh + 1) * W], -jnp.inf)
            mx = jnp.maximum(jnp.max(sm, axis=0, keepdims=True), sink)
            p = jnp.exp(sm - mx)
            inv.append(1.0 / (jnp.sum(p, axis=0, keepdims=True) + jnp.exp(sink - mx)))
            pn.append(p.astype(BF16))
        p_all.append(jnp.concatenate(pn, axis=1))
        inv_all.append(jnp.concatenate(inv, axis=1))

    for kh in range(n_kv):
        par = kh % 2
        o_t = _dot(vt_all[kh // 2], p_all[kh])[par * HD:(par + 1) * HD] * inv_all[kh]
        for pr in range(group // 2):
            pair_t = jnp.concatenate([o_t[:, (2 * pr) * W:(2 * pr + 1) * W],
                                      o_t[:, (2 * pr + 1) * W:(2 * pr + 2) * W]], axis=0)
            c0 = (kh * group + 2 * pr) * HD
            o_ref[rows, c0:c0 + LANES] = pair_t.T.astype(o_ref.dtype)


def _attn_prompt(p, sinks, B, L, n_q, n_kv):
    W = WINDOW
    nb = L // W
    QW = n_q * HEAD_DIM
    KW = n_kv * HEAD_DIM
    assert 2 * HEAD_DIM == LANES and W == LANES and L % W == 0
    assert n_kv % 2 == 0 and (n_q // n_kv) % 2 == 0 and QW % KW == 0
    kblk = QW // KW
    vblk = (QW + KW) // KW
    QB = ATTN_Q_BLOCKS
    assert nb % QB == 0
    ns = nb // QB
    n_heads_scores = n_q * 2 * W * W * 4
    est = 2 * QB * (W * QW * 4 + 2 * W * KW * 4 + W * QW * 2) + 4 * W * KW * 4 + 2 * n_heads_scores + LOOP_TEMPS_BYTES
    return pl.pallas_call(
        functools.partial(_attn_prompt_kernel, n_kv=n_kv, group=n_q // n_kv, qb=QB),
        grid=(B, ns),
        in_specs=[
            pl.BlockSpec(memory_space=pltpu.SMEM),
            pl.BlockSpec((QB * W, QW), lambda b, i: (b * ns + i, 0)),
            pl.BlockSpec((QB * W, KW), lambda b, i: (b * ns + i, kblk)),
            pl.BlockSpec((W, KW), lambda b, i: (b * nb + jnp.maximum(QB * i - 1, 0), kblk)),
            pl.BlockSpec((QB * W, KW), lambda b, i: (b * ns + i, vblk)),
            pl.BlockSpec((W, KW), lambda b, i: (b * nb + jnp.maximum(QB * i - 1, 0), vblk)),
        ],
        out_specs=pl.BlockSpec((QB * W, QW), lambda b, i: (b * ns + i, 0)),
        out_shape=jax.ShapeDtypeStruct((B * L, QW), BF16),
        compiler_params=pltpu.CompilerParams(
            dimension_semantics=("parallel", "parallel"),
            vmem_limit_bytes=_vmem_limit(est)),
        name="swa_prompt",
    )(sinks, p, p, p, p, p)


def _attn_sample_kernel(sinks_ref, qkv_ref, ck_ref, cv_ref, o_ref, *, n_q, n_kv, n_new):
    R = qkv_ref.shape[1]
    W = WINDOW
    HD = HEAD_DIM
    group = n_q // n_kv
    npair = group // 2
    QW = n_q * HD
    KW = n_kv * HD
    RR = npair * R
    scale = HD ** -0.5
    row = lax.broadcasted_iota(jnp.int32, (RR, 2 * W), 0)
    r = row & (R - 1)
    j = lax.broadcasted_iota(jnp.int32, (RR, 2 * W), 1)
    mask = (j >= r) & (j <= r + W) & (j < W + n_new)
    pair_of_row = lax.broadcasted_iota(jnp.int32, (RR, 1), 0) // R
    first = lax.broadcasted_iota(jnp.int32, (2 * W, LANES), 1) < HD
    zpad = jnp.zeros((W - R, LANES), F32)

    s_all, v_all = [], []
    for c in range(n_kv // 2):
        cs = slice(c * LANES, (c + 1) * LANES)
        kn = qkv_ref[0, :, QW + c * LANES:QW + (c + 1) * LANES]
        vn = qkv_ref[0, :, QW + KW + c * LANES:QW + KW + (c + 1) * LANES]
        ktile = jnp.concatenate([ck_ref[0, :, cs], kn, zpad], axis=0)
        vtile = jnp.concatenate([cv_ref[0, :, cs], vn, zpad], axis=0)
        kswap = pltpu.roll(ktile, HD, 1)
        vswap = pltpu.roll(vtile, HD, 1)
        for par in range(2):
            kh = 2 * c + par
            k_lo, k_hi = (ktile, kswap) if par == 0 else (kswap, ktile)
            v_lo, v_hi = (vtile, vswap) if par == 0 else (vswap, vtile)
            kcat = jnp.concatenate([jnp.where(first, k_lo, 0.0), jnp.where(first, 0.0, k_hi)], axis=0)
            vcat = jnp.concatenate([jnp.where(first, v_lo, 0.0), jnp.where(first, 0.0, v_hi)], axis=0)
            qs = jnp.concatenate(
                [qkv_ref[0, :, (kh * group + 2 * pr) * HD:(kh * group + 2 * pr) * HD + LANES]
                 for pr in range(npair)], axis=0) * scale
            s_all.append(_dot_nt(qs.astype(BF16), kcat.astype(BF16)))
            v_all.append(vcat.astype(BF16))

    p_all = []
    for kh in range(n_kv):
        halves = []
        for hf in range(2):
            sink = jnp.full((RR, 1), sinks_ref[kh * group + hf], F32)
            for pr in range(1, npair):
                sink = jnp.where(pair_of_row == pr, sinks_ref[kh * group + 2 * pr + hf], sink)
            sm = jnp.where(mask, s_all[kh][:, hf * 2 * W:(hf + 1) * 2 * W], -jnp.inf)
            (p,) = _sink_softmax([sm], sink)
            halves.append(p.astype(BF16))
        p_all.append(jnp.concatenate(halves, axis=1))

    for kh in range(n_kv):
        o = _dot(p_all[kh], v_all[kh])
        for pr in range(npair):
            c0 = (kh * group + 2 * pr) * HD
            o_ref[0, :, c0:c0 + LANES] = o[pr * R:(pr + 1) * R]


def _attn_sample(qkv, ck, cv, layer, sinks, n_q, n_kv, n_new):
    B, R, NW = qkv.shape
    assert R & (R - 1) == 0 and n_new <= R and 2 * HEAD_DIM == LANES and ck.shape[2] == WINDOW
    QW = n_q * HEAD_DIM
    KW = n_kv * HEAD_DIM
    return pl.pallas_call(
        functools.partial(_attn_sample_kernel, n_q=n_q, n_kv=n_kv, n_new=n_new),
        grid=(B,),
        in_specs=[
            pl.BlockSpec(memory_space=pltpu.SMEM),
            pl.BlockSpec((1, R, NW), lambda b: (b, 0, 0)),
            pl.BlockSpec((None, 1, WINDOW, KW), lambda b: (layer, b, 0, 0)),
            pl.BlockSpec((None, 1, WINDOW, KW), lambda b: (layer, b, 0, 0)),
        ],
        out_specs=pl.BlockSpec((1, R, QW), lambda b: (b, 0, 0)),
        out_shape=jax.ShapeDtypeStruct((B, R, QW), F32),
        compiler_params=pltpu.CompilerParams(dimension_semantics=("parallel",)),
        name="swa_sample",
    )(sinks, qkv, ck, cv)


def _rope_tables(pos):
    half = HEAD_DIM // 2
    inv = jnp.power(ROPE_THETA, -jnp.arange(half, dtype=F32) * (2.0 / HEAD_DIM))
    ang = pos.astype(F32)[:, None] * inv[None, :]
    cos = jnp.cos(ang)
    sin = jnp.sin(ang)
    reps = LANES // HEAD_DIM
    cos_t = jnp.tile(jnp.concatenate([cos, cos], axis=1), (1, reps))
    sin_t = jnp.tile(jnp.concatenate([-sin, sin], axis=1), (1, reps))
    return (jnp.stack([cos_t, jnp.ones_like(cos_t)]), jnp.stack([sin_t, jnp.zeros_like(sin_t)]))


def kernel(x_prompt, x_sample, state_hgrn, cache_k_win, cache_v_win, norm_gains, w_ffn_in, w_ffn_out,
           w_hgrn_in, hgrn_lb_logits, hgrn_norm_gain, w_hgrn_out, w_attn_in, attn_sinks, w_attn_out):
    B, L, D = x_prompt.shape
    SB, SL, _ = x_sample.shape
    depth = norm_gains.shape[0]
    TP, TS = B * L, SB * SL
    H = hgrn_norm_gain.shape[1] // HG_DV
    n_kv = cache_k_win.shape[3]
    n_q = w_attn_out.shape[1] // HEAD_DIM
    QW, KW = n_q * HEAD_DIM, n_kv * HEAD_DIM
    buf = cache_k_win.shape[2]

    x = (x_prompt.reshape(TP, D), x_sample.reshape(TS, D))

    lb_all = jnp.cumsum(jax.nn.softmax(hgrn_lb_logits.astype(F32), axis=0), axis=0)
    lb_all = lb_all - lb_all[0:1]

    pos = jnp.concatenate([jnp.tile(jnp.arange(L, dtype=jnp.int32), B),
                           jnp.tile(PAST_LEN + jnp.arange(SL, dtype=jnp.int32), SB)])
    rope = _rope_tables(pos)

    hg_p, kp_l, vp_l, ks_l, vs_l = [], [], [], [], []
    hg_s = None
    for layer in range(depth):
        g = norm_gains[layer]
        x = _ffn_half(x, g[0:1], g[1:2], w_ffn_in, w_ffn_out, layer, 0)
        if layer % N_MIXERS == 0:
            a = layer // N_MIXERS
            lb = lb_all[a][None, :]
            lbp = (jnp.log(lb), jnp.log1p(-lb), 1.0 - lb)
            ng = hgrn_norm_gain[a][None, :].astype(F32)
            o_p, s_p = _hgrn_fused(x, g[2:3], w_hgrn_in, a, B, L, lbp, ng)
            o_p = o_p.reshape(TP, -1)
            p_s = _proj(x[TP:], g[2:3], w_hgrn_in, a, (TS, HGRN_SAMPLE_PROJ_TN))
            ps = jnp.pad(p_s.reshape(SB, SL, -1), ((0, 0), (0, SAMPLE_PAD - SL), (0, 0)))
            o_s, hg_s = _hgrn_scan(ps.reshape(SB * SAMPLE_PAD, -1), SB, SAMPLE_PAD, SAMPLE_PAD, H, SL,
                                   lbp, ng, (state_hgrn, a), F32, prev=hg_s)
            o_s = o_s.reshape(SB, SAMPLE_PAD, -1)[:, :SL].reshape(TS, -1).astype(BF16)
            x = _outproj(o_p, o_s, w_hgrn_out, x, g[3:4], a)
            hg_p.append(s_p.astype(state_hgrn.dtype))
        else:
            bb = layer // N_MIXERS
            p = _proj(x, g[2:3], w_attn_in, bb, ATTN_PROJ_TILES, rope=rope, n_rope_cols=QW + KW)
            sinks = attn_sinks[bb].astype(F32)
            o_p = _attn_prompt(p, sinks, B, L, n_q, n_kv)
            pn = p[TP:].reshape(SB, SL, -1)
            ck = cache_k_win.reshape(-1, SB, buf, KW)
            cv = cache_v_win.reshape(-1, SB, buf, KW)
            pn_rows = jnp.pad(pn, ((0, 0), (0, -SL % SUBLANES), (0, 0)))
            o_s = _attn_sample(pn_rows, ck, cv, bb, sinks, n_q, n_kv, SL)
            x = _outproj(o_p, o_s[:, :SL].reshape(TS, QW).astype(BF16), w_attn_out, x, g[3:4], bb)
            tails = [p[(s + 1) * L - buf:(s + 1) * L, QW:] for s in range(B)]
            kp_l.append(jnp.stack([t[:, :KW] for t in tails]).reshape(B, buf, n_kv, HEAD_DIM))
            vp_l.append(jnp.stack([t[:, KW:] for t in tails]).reshape(B, buf, n_kv, HEAD_DIM))
            k_new = pn[:, :, QW:QW + KW].reshape(SB, SL, n_kv, HEAD_DIM)
            v_new = pn[:, :, QW + KW:].reshape(SB, SL, n_kv, HEAD_DIM)
            ks_l.append(k_new)
            vs_l.append(v_new)
        x = _ffn_half(x, g[4:5], g[5:6], w_ffn_in, w_ffn_out, layer, 1,
                      out_split=TP if layer == depth - 1 else None)

    yp = x[0].reshape(B, L, D)
    ys = x[1].reshape(SB, SL, D)
    k_win_s = jnp.concatenate([cache_k_win, jnp.stack(ks_l).astype(cache_k_win.dtype)], axis=2)[:, :, -buf:]
    v_win_s = jnp.concatenate([cache_v_win, jnp.stack(vs_l).astype(cache_v_win.dtype)], axis=2)[:, :, -buf:]
    return (yp, ys, jnp.stack(hg_p), hg_s.astype(state_hgrn.dtype), jnp.stack(kp_l), jnp.stack(vp_l),
            k_win_s, v_win_s)
```

```python
import functools

import jax
import jax.numpy as jnp
from jax import lax
from jax.experimental import pallas as pl
from jax.experimental.pallas import tpu as pltpu

F32 = jnp.float32
BF16 = jnp.bfloat16

NORM_EPS = 1e-6
HEAD_DIM = 64
HG_DK = 128
HG_DV = 128
HG_CHUNK = 64
WINDOW = 128
PAST_LEN = 16384
ROPE_THETA = 10000.0
N_MIXERS = 2

LANES = 128
SUBLANES = 8
MIB = 1024 * 1024
VMEM_BYTES_V7X = 64 * MIB
VMEM_HEADROOM = 6 * MIB
VMEM_MIN_REQUEST = 16 * MIB
LOOP_TEMPS_BYTES = 8 * MIB

FFN_TM, FFN_TF = 1040, 256
FFN_RC = 208
HGRN_SAMPLE_PROJ_TN = 1024
FUSED_HB, FUSED_NSEQ, FUSED_RB = 2, 2, 256
ATTN_PROJ_TILES = (2080, 512)
PROJ_X_DOUBLE_BUFFER_MAX_BYTES = 12 * MIB
ROPE_ROW_CHUNK = 208
OUT_TM = 512
ATTN_Q_BLOCKS = 4
SAMPLE_PAD = 16


def _vmem_limit(nbytes):
    return int(min(VMEM_BYTES_V7X - VMEM_HEADROOM, max(nbytes * 5 // 4, VMEM_MIN_REQUEST)))


def _rms(x, g):
    return x * lax.rsqrt(jnp.mean(x * x, axis=-1, keepdims=True) + NORM_EPS) * g


def _dot(a, b):
    return jnp.dot(a, b, preferred_element_type=F32)


def _dot_nt(a, b):
    return lax.dot_general(a, b, (((1,), (1,)), ((), ())), preferred_element_type=F32)


def _dot_tn(a, b):
    return lax.dot_general(a, b, (((0,), (0,)), ((), ())), preferred_element_type=F32)


def _ffn_kernel(*refs, nt, nj, tm, rc, in_split, out_split):
    refs = list(refs)
    x_refs = [refs.pop(0) for _ in range(1 if in_split is None else 2)]
    gpre_ref, gpost_ref, wg_ref, wu_ref, wo_ref = (refs.pop(0) for _ in range(5))
    y_refs = [refs.pop(0) for _ in range(1 if out_split is None else 2)]
    acc_ref, xn_ref, xc_ref, yc_ref, in_sem, out_sem = refs

    i = pl.program_id(0)
    j = pl.program_id(1)
    nch = tm // rc
    slot = i % 2
    pslot = 1 - slot
    k = j - 1
    gpre = gpre_ref[...]
    gpost_half = 0.5 * gpost_ref[...]

    def in_copy(row0, b):
        return pltpu.make_async_copy(x_refs[0].at[pl.ds(row0, rc)], xc_ref.at[b], in_sem.at[b])

    def out_copy(row0, b):
        return pltpu.make_async_copy(yc_ref.at[b], y_refs[0].at[pl.ds(row0, rc)], out_sem.at[b])

    def crossing_row(split):
        if split is None:
            return None
        assert split % rc != 0 and nt * tm - split < rc, "exactly one chunk must cross the split point"
        return (split // rc) * rc

    in_cross = crossing_row(in_split)
    out_cross = crossing_row(out_split)
    assert out_cross is None or out_cross >= (nt - 1) * tm, "only the last row tile may cross the output split"

    def in_copies_crossing(b):
        na = in_split - in_cross
        return [pltpu.make_async_copy(x_refs[0].at[pl.ds(in_cross, na)], xc_ref.at[b, pl.ds(0, na)], in_sem.at[b]),
                pltpu.make_async_copy(x_refs[1].at[pl.ds(0, rc - na)], xc_ref.at[b, pl.ds(na, rc - na)],
                                      in_sem.at[b])]

    def out_copies_crossing(b):
        na = out_split - out_cross
        return [pltpu.make_async_copy(yc_ref.at[b, pl.ds(0, na)], y_refs[0].at[pl.ds(out_cross, na)], out_sem.at[b]),
                pltpu.make_async_copy(yc_ref.at[b, pl.ds(na, rc - na)], y_refs[1].at[pl.ds(0, rc - na)],
                                      out_sem.at[b])]

    def static_in_copies(row0, b):
        return in_copies_crossing(b) if row0 == in_cross else [in_copy(row0, b)]

    def static_out_copies(row0, b):
        return out_copies_crossing(b) if row0 == out_cross else [out_copy(row0, b)]

    def task_row(t):
        return jnp.where(t < nch, (i - 1) * tm + t * rc, (i + 1) * tm + (t - nch) * rc)

    def task_live(t):
        return (t >= 0) & (t < 2 * nch) & jnp.where(t < nch, i >= 1, i <= nt - 2)

    def task_crosses(t):
        return False if in_cross is None else task_row(t) == in_cross

    @pl.when((i == 0) & (j == 0))
    def _():
        for c in range(nch):
            cps = static_in_copies(c * rc, c % 2)
            for cp in cps:
                cp.start()
            for cp in cps:
                cp.wait()
            xn_ref[0, c * rc:(c + 1) * rc] = _rms(xc_ref[c % 2], gpre).astype(BF16)

    @pl.when(task_live(k) & jnp.logical_not(task_crosses(k)))
    def _():
        in_copy(0, k & 1).wait()

    @pl.when(task_live(k) & (k >= 2) & (k < nch))
    def _():
        out_copy(0, k & 1).wait()

    @pl.when((j == nch + 1) & (i >= 1))
    def _():
        out_copy(0, 0).wait()
        out_copy(0, 1).wait()

    @pl.when(task_live(j) & jnp.logical_not(task_crosses(j)))
    def _():
        in_copy(task_row(j), j & 1).start()

    if in_cross is not None:
        @pl.when(task_live(k) & task_crosses(k))
        def _():
            for cp in in_copies_crossing(k & 1):
                cp.wait()

        @pl.when(task_live(j) & task_crosses(j))
        def _():
            for cp in in_copies_crossing(j & 1):
                cp.start()

    def accumulate(first):
        xn = xn_ref[slot]
        hg = _dot(xn, wg_ref[...].astype(BF16))
        hu = _dot(xn, wu_ref[...].astype(BF16))
        act = (hg * jax.nn.sigmoid(hg) * hu).astype(BF16)
        contrib = _dot(act, wo_ref[...].astype(BF16))
        if first:
            acc_ref[slot] = contrib
        else:
            acc_ref[slot] += contrib

    is_epi = (k >= 0) & (k < nch) & (i >= 1)
    is_pro = (k >= nch) & (k < 2 * nch) & (i <= nt - 2)

    @pl.when(j == 0)
    def _():
        accumulate(True)

    @pl.when(is_epi)
    def _():
        accumulate(False)
        b = k & 1
        rows = pl.ds(pl.multiple_of(k * rc, rc), rc)
        yc_ref[b] = xc_ref[b] + _rms(acc_ref[pslot, rows], gpost_half)
        out_copy((i - 1) * tm + k * rc, b).start()

    @pl.when(is_pro)
    def _():
        accumulate(False)
        rows = pl.ds(pl.multiple_of((k - nch) * rc, rc), rc)
        xn_ref[pslot, rows] = _rms(xc_ref[k & 1], gpre).astype(BF16)

    @pl.when((j >= 1) & jnp.logical_not(is_epi) & jnp.logical_not(is_pro))
    def _():
        accumulate(False)

    @pl.when((i == nt - 1) & (j == nj - 1))
    def _():
        last = (nt - 1) % 2
        for c in range(nch):
            b = c % 2
            row0 = (nt - 1) * tm + c * rc
            cps = static_in_copies(row0, b)
            for cp in cps:
                cp.start()
            for cp in cps:
                cp.wait()
            yc_ref[b] = xc_ref[b] + _rms(acc_ref[last, c * rc:(c + 1) * rc], gpost_half)
            ocs = static_out_copies(row0, b)
            for oc in ocs:
                oc.start()
            for oc in ocs:
                oc.wait()


def _ffn_half(x, g_pre, g_post, w_in, w_out, layer, half, out_split=None):
    xs = x if isinstance(x, (tuple, list)) else (x,)
    in_split = xs[0].shape[0] if len(xs) == 2 else None
    T = sum(a.shape[0] for a in xs)
    D = xs[0].shape[1]
    F = w_out.shape[2]
    TM, TF, RC = FFN_TM, FFN_TF, FFN_RC
    nj = F // TF
    nt = T // TM
    assert T % TM == 0 and F % TF == 0 and TM % RC == 0 and TM // RC >= 2 and nj >= 2 * (TM // RC) + 2
    est = (2 * TM * D * (4 + 2) + 4 * RC * D * 4 + 2 * 3 * D * TF * 4
           + 3 * D * TF * 2 + 4 * TM * TF * 4 + TM * D * 4)
    any_spec = pl.BlockSpec(memory_space=pl.ANY)
    if out_split is None:
        out_specs, out_shape = any_spec, jax.ShapeDtypeStruct((T, D), F32)
    else:
        out_specs = [any_spec, any_spec]
        out_shape = [jax.ShapeDtypeStruct((out_split, D), F32), jax.ShapeDtypeStruct((T - out_split, D), F32)]
    return pl.pallas_call(
        functools.partial(_ffn_kernel, nt=nt, nj=nj, tm=TM, rc=RC, in_split=in_split, out_split=out_split),
        grid=(nt, nj),
        in_specs=[any_spec] * len(xs) + [
            pl.BlockSpec((1, D), lambda i, j: (0, 0)),
            pl.BlockSpec((1, D), lambda i, j: (0, 0)),
            pl.BlockSpec((None, None, D, TF), lambda i, j: (layer, half, 0, j)),
            pl.BlockSpec((None, None, D, TF), lambda i, j: (layer, half, 0, nj + j)),
            pl.BlockSpec((None, None, TF, D), lambda i, j: (layer, half, j, 0)),
        ],
        out_specs=out_specs,
        out_shape=out_shape,
        scratch_shapes=[pltpu.VMEM((2, TM, D), F32), pltpu.VMEM((2, TM, D), BF16),
                        pltpu.VMEM((2, RC, D), F32), pltpu.VMEM((2, RC, D), F32),
                        pltpu.SemaphoreType.DMA((2,)), pltpu.SemaphoreType.DMA((2,))],
        compiler_params=pltpu.CompilerParams(
            dimension_semantics=("arbitrary", "arbitrary"),
            vmem_limit_bytes=_vmem_limit(est)),
        name="ffn_half",
    )(*xs, g_pre, g_post, w_in, w_in, w_out)


def _rope_tile(y, cos, sin_signed):
    lane = lax.broadcasted_iota(jnp.int32, y.shape, 1)
    first_half = (lane & (HEAD_DIM - 1)) < (HEAD_DIM // 2)
    rot = jnp.where(first_half,
                    pltpu.roll(y, LANES - HEAD_DIM // 2, 1),
                    pltpu.roll(y, HEAD_DIM // 2, 1))
    return y * cos + rot * sin_signed


def _proj_kernel(x_ref, g_ref, w_ref, o_ref, xn_ref):
    @pl.when(pl.program_id(1) == 0)
    def _():
        xn_ref[...] = _rms(x_ref[...], g_ref[...]).astype(BF16)

    o_ref[...] = _dot(xn_ref[...], w_ref[...].astype(BF16))


def _proj_rope_kernel(x_ref, g_ref, w_ref, cos_ref, sin_ref, o_ref, xn_ref, *, row_chunk):
    @pl.when(pl.program_id(1) == 0)
    def _():
        xn_ref[...] = _rms(x_ref[...], g_ref[...]).astype(BF16)

    wb = w_ref[...].astype(BF16)
    for r0 in range(0, xn_ref.shape[0], row_chunk):
        rows = slice(r0, r0 + row_chunk)
        y = _dot(xn_ref[rows], wb)
        cos = cos_ref[rows]
        sin = sin_ref[rows]
        for c in range(y.shape[1] // LANES):
            cols = slice(c * LANES, (c + 1) * LANES)
            o_ref[rows, cols] = _rope_tile(y[:, cols], cos, sin)


def _proj(x, g, w, layer, tiles, rope=None, n_rope_cols=0):
    T, D = x.shape
    N = w.shape[2]
    TM, TN = tiles
    assert T % TM == 0 and N % TN == 0 and n_rope_cols % TN == 0
    x_buffers = 1 if TM * D * 4 > PROJ_X_DOUBLE_BUFFER_MAX_BYTES else 2
    est = x_buffers * TM * D * 4 + TM * D * 2 + 2 * D * TN * 4 + D * TN * 2 + 3 * TM * TN * 4
    in_specs = [
        pl.BlockSpec((TM, D), lambda i, j: (i, 0), pipeline_mode=pl.Buffered(x_buffers)),
        pl.BlockSpec((1, D), lambda i, j: (0, 0)),
        pl.BlockSpec((None, D, TN), lambda i, j: (layer, 0, j)),
    ]
    args = [x, g, w]
    if rope is None:
        body = _proj_kernel
    else:
        assert TM % ROPE_ROW_CHUNK == 0
        body = functools.partial(_proj_rope_kernel, row_chunk=ROPE_ROW_CHUNK)
        n_rope_blocks = n_rope_cols // TN
        table = pl.BlockSpec((None, TM, LANES), lambda i, j: (jnp.where(j < n_rope_blocks, 0, 1), i, 0))
        in_specs += [table, table]
        args += list(rope)
    return pl.pallas_call(
        body,
        grid=(T // TM, N // TN),
        in_specs=in_specs,
        out_specs=pl.BlockSpec((TM, TN), lambda i, j: (i, j)),
        out_shape=jax.ShapeDtypeStruct((T, N), F32),
        scratch_shapes=[pltpu.VMEM((TM, D), BF16)],
        compiler_params=pltpu.CompilerParams(
            dimension_semantics=("parallel", "arbitrary"),
            vmem_limit_bytes=_vmem_limit(est)),
        name="mixer_in_proj",
    )(*args)


def _outproj_kernel(op_ref, os_ref, w_ref, x_ref, g_ref, y_ref, wbf_ref, *, n_prompt_tiles, ts):
    i = pl.program_id(0)

    @pl.when(i == 0)
    def _():
        wbf_ref[...] = w_ref[...].astype(BF16)

    @pl.when(i < n_prompt_tiles)
    def _():
        y_ref[...] = x_ref[...] + _rms(_dot(op_ref[...], wbf_ref[...]), g_ref[...])

    @pl.when(i == n_prompt_tiles)
    def _():
        y_ref[0:ts] = x_ref[0:ts] + _rms(_dot(os_ref[...], wbf_ref[...]), g_ref[...])


def _outproj(o_prompt, o_sample, w, x, g, layer):
    T, D = x.shape
    TP, K = o_prompt.shape
    TS = o_sample.shape[0]
    TM = OUT_TM
    assert TP % TM == 0 and TS <= TM and TS % SUBLANES == 0 and TP + TS == T
    npt = TP // TM
    est = K * D * (4 + 2) + 2 * TM * K * 2 + 2 * TS * K * 2 + 5 * TM * D * 4
    return pl.pallas_call(
        functools.partial(_outproj_kernel, n_prompt_tiles=npt, ts=TS),
        grid=(npt + 1,),
        in_specs=[
            pl.BlockSpec((TM, K), lambda i: (jnp.minimum(i, npt - 1), 0)),
            pl.BlockSpec((TS, K), lambda i: (0, 0)),
            pl.BlockSpec((None, K, D), lambda i: (layer, 0, 0), pipeline_mode=pl.Buffered(1)),
            pl.BlockSpec((TM, D), lambda i: (i, 0)),
            pl.BlockSpec((1, D), lambda i: (0, 0)),
        ],
        out_specs=pl.BlockSpec((TM, D), lambda i: (i, 0)),
        out_shape=jax.ShapeDtypeStruct((T, D), F32),
        scratch_shapes=[pltpu.VMEM((K, D), BF16)],
        compiler_params=pltpu.CompilerParams(
            dimension_semantics=("arbitrary",),
            vmem_limit_bytes=_vmem_limit(est)),
        name="mixer_out_proj",
    )(o_prompt, o_sample, w, x, g)


def _tiles(x):
    return [x[SUBLANES * j:SUBLANES * (j + 1)] for j in range(x.shape[0] // SUBLANES)]


def _bcast_row(t, i):
    return jnp.broadcast_to(t[i:i + 1], t.shape)


def _tile_prefix(t, sub):
    p = t + jnp.where(sub >= 1, pltpu.roll(t, 1, 0), 0.0)
    p = p + jnp.where(sub >= 2, pltpu.roll(p, 2, 0), 0.0)
    return p + jnp.where(sub >= 4, pltpu.roll(p, 4, 0), 0.0)


def _hgrn_gates(fz, loglb, log1m, onem, valid):
    e = jnp.exp(-jnp.abs(fz))
    lsig = jnp.minimum(fz, 0.0) - jnp.log(1.0 + e)
    ct = log1m + lsig
    g = jnp.maximum(loglb, ct) + jnp.log(1.0 + jnp.exp(-jnp.abs(loglb - ct)))
    r = 1.0 / (1.0 + e)
    k = onem * jnp.where(fz > 0.0, e * r, r)
    if valid is not None:
        g = jnp.where(valid, g, 0.0)
        k = jnp.where(valid, k, 0.0)
    return g, k


def _hgrn_level_operands(m, qt, kt, gt, bt, sub):
    nv = len(qt)
    zero = jnp.zeros_like(qt[0])
    qm, km = [], []
    if m >= SUBLANES:
        mv = m // SUBLANES
        for j in range(nv):
            jr = (j // (2 * mv)) * (2 * mv) + mv - 1
            ref = _bcast_row(bt[jr], SUBLANES - 1)
            if j % (2 * mv) >= mv:
                qm.append(qt[j] * jnp.exp(bt[j] - ref))
                km.append(zero)
            else:
                qm.append(zero)
                km.append(kt[j] * jnp.exp(ref - bt[j]))
        return qm, km
    upper = (sub & m) != 0
    for j in range(nv):
        if m == 1:
            e = jnp.exp(gt[j])
            qm.append(jnp.where(upper, qt[j] * e, 0.0))
            km.append(jnp.where(upper, 0.0, kt[j]))
            continue
        if m == 4:
            ref = _bcast_row(bt[j], 3)
        else:
            ref = jnp.where(sub < 4, _bcast_row(bt[j], 1), _bcast_row(bt[j], 5))
        e = jnp.exp(-jnp.abs(bt[j] - ref))
        qm.append(jnp.where(upper, qt[j] * e, 0.0))
        km.append(jnp.where(upper, 0.0, kt[j] * e))
    return qm, km


def _hgrn_chunk_group(loads, params, sts, *, C, valid_len):
    nh = len(loads)
    sub = lax.broadcasted_iota(jnp.int32, (SUBLANES, LANES), 0)
    valid = None
    if valid_len < C:
        valid = lax.broadcasted_iota(jnp.int32, (C, LANES), 0) < valid_len
    ti = lax.broadcasted_iota(jnp.int32, (C, C), 0)
    si = lax.broadcasted_iota(jnp.int32, (C, C), 1)

    gk = [_hgrn_gates(loads[h][1], params[h][0], params[h][1], params[h][2], valid) for h in range(nh)]

    gts, bts = [], []
    for h in range(nh):
        gt = _tiles(gk[h][0])
        bt = []
        for j, t in enumerate(gt):
            p = _tile_prefix(t, sub)
            bt.append(p if j == 0 else p + _bcast_row(bt[j - 1], SUBLANES - 1))
        gts.append(gt)
        bts.append(bt)

    scores = []
    for h in range(nh):
        q, k = loads[h][0], gk[h][1]
        qt, kt = _tiles(q), _tiles(k)
        a = jnp.where(ti == si, _dot_nt(q.astype(BF16), k.astype(BF16)), 0.0)
        m = C // 2
        while m >= 1:
            qm, km = _hgrn_level_operands(m, qt, kt, gts[h], bts[h], sub)
            am = _dot_nt(jnp.concatenate(qm, axis=0).astype(BF16), jnp.concatenate(km, axis=0).astype(BF16))
            if 2 * m < C:
                am = jnp.where((ti & (-2 * m)) == (si & (-2 * m)), am, 0.0)
            a = a + am
            m //= 2
        scores.append(a)

    outs = []
    for h in range(nh):
        q, _, v, gz = loads[h]
        k = gk[h][1]
        ng = params[h][3]
        st = sts[h]
        b = jnp.concatenate(bts[h], axis=0)
        o = _dot(scores[h].astype(BF16), v.astype(BF16)) + _dot_nt((q * jnp.exp(b)).astype(BF16), st.astype(BF16))
        bl = b[C - 1:C]
        kb = (k * jnp.exp(bl - b)).astype(BF16)
        st_new = st * jnp.exp(bl) + _dot_tn(v.astype(BF16), kb)
        on = o * lax.rsqrt(jnp.mean(o * o, axis=-1, keepdims=True) + NORM_EPS) * ng
        outs.append((on * (gz * jax.nn.sigmoid(gz)), st_new))
    return outs


def _hgrn_kernel(*refs, C, NC, HB, valid_len, has_s0, n_prev):
    refs = list(refs)
    q_ref, f_ref, i_ref, g_ref, loglb_ref, log1m_ref, onem_ref, ng_ref = (refs.pop(0) for _ in range(8))
    s0_ref = refs.pop(0) if has_s0 else None
    prev_ref = refs.pop(0) if n_prev else None
    o_ref, s_ref, st_ref = refs

    for h in range(HB):
        if has_s0:
            st_ref[h] = s0_ref[0, h].T
        else:
            st_ref[h] = jnp.zeros((HG_DV, HG_DK), F32)

    def body(c, carry):
        rows = pl.ds(pl.multiple_of(c * C, C), C)
        cols = [slice(h * LANES, (h + 1) * LANES) for h in range(HB)]
        loads = [(q_ref[rows, cs], f_ref[rows, cs], i_ref[rows, cs], g_ref[rows, cs]) for cs in cols]
        params = [(loglb_ref[:, cs], log1m_ref[:, cs], onem_ref[:, cs], ng_ref[:, cs]) for cs in cols]
        outs = _hgrn_chunk_group(loads, params, [st_ref[h] for h in range(HB)], C=C, valid_len=valid_len)
        for h in range(HB):
            st_ref[h] = outs[h][1]
            o_ref[rows, cols[h]] = outs[h][0].astype(o_ref.dtype)
        return carry

    if NC == 1:
        body(0, 0)
    else:
        lax.fori_loop(0, NC, body, 0)

    for layer in range(n_prev):
        s_ref[layer] = prev_ref[layer]
    for h in range(HB):
        s_ref[n_prev, 0, h] = st_ref[h].T


def _hgrn_scan(p, nseq, L, C, HB, valid_len, lbp, ng, s0, out_dtype, prev=None):
    H = ng.shape[1] // LANES
    nhb = H // HB
    W = HB * LANES
    NC = L // C
    loglb, log1m, onem = lbp
    has_s0 = s0 is not None

    def sect(s):
        return pl.BlockSpec((L, W), lambda b, hb, s=s: (b, s * nhb + hb))

    vec = pl.BlockSpec((1, W), lambda b, hb: (0, hb))
    in_specs = [sect(0), sect(1), sect(2), sect(3), vec, vec, vec, vec]
    args = [p, p, p, p, loglb, log1m, onem, ng]
    if has_s0:
        s0_all, s0_layer = s0
        in_specs.append(pl.BlockSpec((None, 1, HB, HG_DK, HG_DV), lambda b, hb: (s0_layer, b, hb, 0, 0)))
        args.append(s0_all)
    n_prev = 0 if prev is None else prev.shape[0]
    if n_prev:
        in_specs.append(pl.BlockSpec((n_prev, 1, HB, HG_DK, HG_DV), lambda b, hb: (0, b, hb, 0, 0)))
        args.append(prev)
    est = (2 * 4 * L * W * 4 + 2 * L * W * 4 + (6 + 4 * (2 * n_prev + 1)) * HB * HG_DK * HG_DV * 4
           + LOOP_TEMPS_BYTES)
    return pl.pallas_call(
        functools.partial(_hgrn_kernel, C=C, NC=NC, HB=HB, valid_len=valid_len, has_s0=has_s0, n_prev=n_prev),
        grid=(nseq, nhb),
        in_specs=in_specs,
        out_specs=[pl.BlockSpec((L, W), lambda b, hb: (b, hb)),
                   pl.BlockSpec((n_prev + 1, 1, HB, HG_DK, HG_DV), lambda b, hb: (0, b, hb, 0, 0))],
        out_shape=[jax.ShapeDtypeStruct((nseq * L, H * LANES), out_dtype),
                   jax.ShapeDtypeStruct((n_prev + 1, nseq, H, HG_DK, HG_DV), F32)],
        scratch_shapes=[pltpu.VMEM((HB, HG_DV, HG_DK), F32)],
        compiler_params=pltpu.CompilerParams(
            dimension_semantics=("parallel", "parallel"),
            vmem_limit_bytes=_vmem_limit(est)),
        name="hgrn_scan",
    )(*args)


def _hgrn_fused_kernel(x_hbm, g_ref, wq_ref, wf_ref, wi_ref, wg_ref, loglb_ref, log1m_ref, onem_ref, ng_ref,
                       o_ref, s_ref, xn_ref, xc_ref, proj_ref, st_ref, in_sem, *, L, C, RB, NSEQ, HB):
    sg = pl.program_id(0)
    hb = pl.program_id(1)
    D = xn_ref.shape[1]
    WC = HB * LANES
    NB = L // RB
    CPB = RB // C
    w_refs = (wq_ref, wf_ref, wi_ref, wg_ref)
    assert CPB == len(w_refs)

    @pl.when(hb == 0)
    def _():
        g = g_ref[...]
        nchunk = NSEQ * NB

        def copy(c):
            return pltpu.make_async_copy(x_hbm.at[pl.ds(sg * (NSEQ * L) + c * RB, RB)],
                                         xc_ref.at[c % 2], in_sem.at[c % 2])
        copy(0).start()
        for c in range(nchunk):
            copy(c).wait()
            if c + 1 < nchunk:
                copy(c + 1).start()
            xn_ref[c * RB:(c + 1) * RB] = _rms(xc_ref[c % 2], g).astype(BF16)

    def project(rb, section, slot):
        rows = [xn_ref[pl.ds(pl.multiple_of(s * L + rb * RB, RB), RB)] for s in range(NSEQ)]
        y = _dot(jnp.concatenate(rows, axis=0), w_refs[section][...].astype(BF16))
        proj_ref[slot, :, section * WC:(section + 1) * WC] = y

    for ch in range(NSEQ * HB):
        st_ref[ch] = jnp.zeros((HG_DV, HG_DK), F32)
    for section in range(CPB):
        project(0, section, 0)

    params = [(loglb_ref[:, h * LANES:(h + 1) * LANES], log1m_ref[:, h * LANES:(h + 1) * LANES],
               onem_ref[:, h * LANES:(h + 1) * LANES], ng_ref[:, h * LANES:(h + 1) * LANES])
              for _ in range(NSEQ) for h in range(HB)]

    def block(rb, project_next):
        slot = rb % 2
        for ci in range(CPB):
            loads = []
            for s in range(NSEQ):
                rows = slice(s * RB + ci * C, s * RB + (ci + 1) * C)
                for h in range(HB):
                    loads.append(tuple(proj_ref[slot, rows, sec * WC + h * LANES:sec * WC + (h + 1) * LANES]
                                       for sec in range(4)))
            if project_next:
                project(rb + 1, ci, 1 - slot)
            outs = _hgrn_chunk_group(loads, params, [st_ref[ch] for ch in range(NSEQ * HB)], C=C, valid_len=C)
            orow = pl.ds(pl.multiple_of(rb * RB + ci * C, C), C)
            for s in range(NSEQ):
                for h in range(HB):
                    ch = s * HB + h
                    st_ref[ch] = outs[ch][1]
                    o_ref[s, orow, h * LANES:(h + 1) * LANES] = outs[ch][0].astype(o_ref.dtype)

    def body(rb, carry):
        block(rb, True)
        return carry

    lax.fori_loop(0, NB - 1, body, 0)
    block(NB - 1, False)

    for s in range(NSEQ):
        for h in range(HB):
            s_ref[s, h] = st_ref[s * HB + h].T


def _hgrn_fused(x, g, w, layer, B, L, lbp, ng):
    D = x.shape[1]
    H = ng.shape[1] // LANES
    HB, NSEQ, RB, C = FUSED_HB, FUSED_NSEQ, FUSED_RB, HG_CHUNK
    WC = HB * LANES
    nhb = H // HB
    assert B % NSEQ == 0 and H % HB == 0 and L % RB == 0 and RB // C == 4
    loglb, log1m, onem = lbp

    def sect(s):
        return pl.BlockSpec((None, D, WC), lambda sg, hb, s=s: (layer, 0, s * nhb + hb))

    vec = pl.BlockSpec((1, WC), lambda sg, hb: (0, hb))
    est = (NSEQ * L * D * 2 + 2 * RB * D * 4 + 2 * NSEQ * RB * 4 * WC * 4 + 2 * 4 * D * WC * 4
           + 2 * NSEQ * L * WC * 2 + LOOP_TEMPS_BYTES)
    return pl.pallas_call(
        functools.partial(_hgrn_fused_kernel, L=L, C=C, RB=RB, NSEQ=NSEQ, HB=HB),
        grid=(B // NSEQ, nhb),
        in_specs=[pl.BlockSpec(memory_space=pl.ANY), pl.BlockSpec((1, D), lambda sg, hb: (0, 0)),
                  sect(0), sect(1), sect(2), sect(3), vec, vec, vec, vec],
        out_specs=[pl.BlockSpec((NSEQ, L, WC), lambda sg, hb: (sg, 0, hb)),
                   pl.BlockSpec((NSEQ, HB, HG_DK, HG_DV), lambda sg, hb: (sg, hb, 0, 0))],
        out_shape=[jax.ShapeDtypeStruct((B, L, H * LANES), BF16),
                   jax.ShapeDtypeStruct((B, H, HG_DK, HG_DV), F32)],
        scratch_shapes=[pltpu.VMEM((NSEQ * L, D), BF16), pltpu.VMEM((2, RB, D), F32),
                        pltpu.VMEM((2, NSEQ * RB, 4 * WC), F32), pltpu.VMEM((NSEQ * HB, HG_DV, HG_DK), F32),
                        pltpu.SemaphoreType.DMA((2,))],
        compiler_params=pltpu.CompilerParams(
            dimension_semantics=("arbitrary", "arbitrary"),
            vmem_limit_bytes=_vmem_limit(est)),
        name="hgrn_fused",
    )(x, g, w, w, w, w, loglb, log1m, onem, ng)


def _sink_softmax(scores, sink):
    m = sink
    for s in scores:
        m = jnp.maximum(m, jnp.max(s, axis=-1, keepdims=True))
    ps = [jnp.exp(s - m) for s in scores]
    denom = jnp.exp(sink - m)
    for p in ps:
        denom = denom + jnp.sum(p, axis=-1, keepdims=True)
    return [p / denom for p in ps]


def _attn_prompt_kernel(sinks_ref, q_ref, kc_ref, kp_ref, vc_ref, vp_ref, o_ref, *, n_kv, group, qb):
    for u in range(qb):
        _attn_prompt_block(sinks_ref, q_ref, kc_ref, kp_ref, vc_ref, vp_ref, o_ref, u, n_kv=n_kv, group=group)


def _attn_prompt_block(sinks_ref, q_ref, kc_ref, kp_ref, vc_ref, vp_ref, o_ref, u, *, n_kv, group):
    i = pl.program_id(1)
    W = WINDOW
    HD = HEAD_DIM
    rows = slice(u * W, (u + 1) * W)
    prev = slice((u - 1) * W, u * W)
    j = lax.broadcasted_iota(jnp.int32, (2 * W, W), 0)
    r = lax.broadcasted_iota(jnp.int32, (2 * W, W), 1)
    first_key = jnp.where(i > 0, 0, W) if u == 0 else 0
    mask = (j >= jnp.maximum(r, first_key)) & (j <= r + W)
    scale = HD ** -0.5
    zeros = jnp.zeros((HD, W), F32)

    s_all, vt_all = [], []
    for c in range(n_kv // 2):
        cs = slice(c * LANES, (c + 1) * LANES)
        k_prev = kp_ref[:, cs] if u == 0 else kc_ref[prev, cs]
        v_prev = vp_ref[:, cs] if u == 0 else vc_ref[prev, cs]
        ktile = jnp.concatenate([k_prev, kc_ref[rows, cs]], axis=0).astype(BF16)
        vt_all.append(jnp.concatenate([v_prev, vc_ref[rows, cs]], axis=0).T.astype(BF16))
        for par in range(2):
            kh = 2 * c + par
            rhs = []
            for pr in range(group // 2):
                c0 = (kh * group + 2 * pr) * HD
                pt = (q_ref[rows, c0:c0 + LANES] * scale).T
                for half in range(2):
                    blk = pt[half * HD:(half + 1) * HD]
                    rhs.append(jnp.concatenate([blk, zeros] if par == 0 else [zeros, blk], axis=0))
            s_all.append(_dot(ktile, jnp.concatenate(rhs, axis=1).astype(BF16)))

    p_all, inv_all = [], []
    for kh in range(n_kv):
        pn, inv = [], []
        for hh in range(group):
            sink = sinks_ref[kh * group + hh]
            sm = jnp.where(mask, s_all[kh][:, hh * W:(hh + 1) * W], -jnp.inf)
            mx = jnp.maximum(jnp.max(sm, axis=0, keepdims=True), sink)
            p = jnp.exp(sm - mx)
            inv.append(1.0 / (jnp.sum(p, axis=0, keepdims=True) + jnp.exp(sink - mx)))
            pn.append(p.astype(BF16))
        p_all.append(jnp.concatenate(pn, axis=1))
        inv_all.append(jnp.concatenate(inv, axis=1))

    for kh in range(n_kv):
        par = kh % 2
        o_t = _dot(vt_all[kh // 2], p_all[kh])[par * HD:(par + 1) * HD] * inv_all[kh]
        for pr in range(group // 2):
            pair_t = jnp.concatenate([o_t[:, (2 * pr) * W:(2 * pr + 1) * W],
                                      o_t[:, (2 * pr + 1) * W:(2 * pr + 2) * W]], axis=0)
            c0 = (kh * group + 2 * pr) * HD
            o_ref[rows, c0:c0 + LANES] = pair_t.T.astype(o_ref.dtype)


def _attn_prompt(p, sinks, B, L, n_q, n_kv):
    W = WINDOW
    nb = L // W
    QW = n_q * HEAD_DIM
    KW = n_kv * HEAD_DIM
    assert 2 * HEAD_DIM == LANES and W == LANES and L % W == 0
    assert n_kv % 2 == 0 and (n_q // n_kv) % 2 == 0 and QW % KW == 0
    kblk = QW // KW
    vblk = (QW + KW) // KW
    QB = ATTN_Q_BLOCKS
    assert nb % QB == 0
    ns = nb // QB
    n_heads_scores = n_q * 2 * W * W * 4
    est = 2 * QB * (W * QW * 4 + 2 * W * KW * 4 + W * QW * 2) + 4 * W * KW * 4 + 2 * n_heads_scores + LOOP_TEMPS_BYTES
    return pl.pallas_call(
        functools.partial(_attn_prompt_kernel, n_kv=n_kv, group=n_q // n_kv, qb=QB),
        grid=(B, ns),
        in_specs=[
            pl.BlockSpec(memory_space=pltpu.SMEM),
            pl.BlockSpec((QB * W, QW), lambda b, i: (b * ns + i, 0)),
            pl.BlockSpec((QB * W, KW), lambda b, i: (b * ns + i, kblk)),
            pl.BlockSpec((W, KW), lambda b, i: (b * nb + jnp.maximum(QB * i - 1, 0), kblk)),
            pl.BlockSpec((QB * W, KW), lambda b, i: (b * ns + i, vblk)),
            pl.BlockSpec((W, KW), lambda b, i: (b * nb + jnp.maximum(QB * i - 1, 0), vblk)),
        ],
        out_specs=pl.BlockSpec((QB * W, QW), lambda b, i: (b * ns + i, 0)),
        out_shape=jax.ShapeDtypeStruct((B * L, QW), BF16),
        compiler_params=pltpu.CompilerParams(
            dimension_semantics=("parallel", "parallel"),
            vmem_limit_bytes=_vmem_limit(est)),
        name="swa_prompt",
    )(sinks, p, p, p, p, p)


def _attn_sample_kernel(sinks_ref, qkv_ref, ck_ref, cv_ref, o_ref, *, n_q, n_kv, n_new):
    R = qkv_ref.shape[1]
    W = WINDOW
    HD = HEAD_DIM
    group = n_q // n_kv
    npair = group // 2
    QW = n_q * HD
    KW = n_kv * HD
    RR = npair * R
    scale = HD ** -0.5
    row = lax.broadcasted_iota(jnp.int32, (RR, 2 * W), 0)
    r = row & (R - 1)
    j = lax.broadcasted_iota(jnp.int32, (RR, 2 * W), 1)
    mask = (j >= r) & (j <= r + W) & (j < W + n_new)
    pair_of_row = lax.broadcasted_iota(jnp.int32, (RR, 1), 0) // R
    first = lax.broadcasted_iota(jnp.int32, (2 * W, LANES), 1) < HD
    zpad = jnp.zeros((W - R, LANES), F32)

    s_all, v_all = [], []
    for c in range(n_kv // 2):
        cs = slice(c * LANES, (c + 1) * LANES)
        kn = qkv_ref[0, :, QW + c * LANES:QW + (c + 1) * LANES]
        vn = qkv_ref[0, :, QW + KW + c * LANES:QW + KW + (c + 1) * LANES]
        ktile = jnp.concatenate([ck_ref[0, :, cs], kn, zpad], axis=0)
        vtile = jnp.concatenate([cv_ref[0, :, cs], vn, zpad], axis=0)
        kswap = pltpu.roll(ktile, HD, 1)
        vswap = pltpu.roll(vtile, HD, 1)
        for par in range(2):
            kh = 2 * c + par
            k_lo, k_hi = (ktile, kswap) if par == 0 else (kswap, ktile)
            v_lo, v_hi = (vtile, vswap) if par == 0 else (vswap, vtile)
            kcat = jnp.concatenate([jnp.where(first, k_lo, 0.0), jnp.where(first, 0.0, k_hi)], axis=0)
            vcat = jnp.concatenate([jnp.where(first, v_lo, 0.0), jnp.where(first, 0.0, v_hi)], axis=0)
            qs = jnp.concatenate(
                [qkv_ref[0, :, (kh * group + 2 * pr) * HD:(kh * group + 2 * pr) * HD + LANES]
                 for pr in range(npair)], axis=0) * scale
            s_all.append(_dot_nt(qs.astype(BF16), kcat.astype(BF16)))
            v_all.append(vcat.astype(BF16))

    p_all = []
    for kh in range(n_kv):
        halves = []
        for hf in range(2):
            sink = jnp.full((RR, 1), sinks_ref[kh * group + hf], F32)
            for pr in range(1, npair):
                sink = jnp.where(pair_of_row == pr, sinks_ref[kh * group + 2 * pr + hf], sink)
            sm = jnp.where(mask, s_all[kh][:, hf * 2 * W:(hf + 1) * 2 * W], -jnp.inf)
            (p,) = _sink_softmax([sm], sink)
            halves.append(p.astype(BF16))
        p_all.append(jnp.concatenate(halves, axis=1))

    for kh in range(n_kv):
        o = _dot(p_all[kh], v_all[kh])
        for pr in range(npair):
            c0 = (kh * group + 2 * pr) * HD
            o_ref[0, :, c0:c0 + LANES] = o[pr * R:(pr + 1) * R]


def _attn_sample(qkv, ck, cv, layer, sinks, n_q, n_kv, n_new):
    B, R, NW = qkv.shape
    assert R & (R - 1) == 0 and n_new <= R and 2 * HEAD_DIM == LANES and ck.shape[2] == WINDOW
    QW = n_q * HEAD_DIM
    KW = n_kv * HEAD_DIM
    return pl.pallas_call(
        functools.partial(_attn_sample_kernel, n_q=n_q, n_kv=n_kv, n_new=n_new),
        grid=(B,),
        in_specs=[
            pl.BlockSpec(memory_space=pltpu.SMEM),
            pl.BlockSpec((1, R, NW), lambda b: (b, 0, 0)),
            pl.BlockSpec((None, 1, WINDOW, KW), lambda b: (layer, b, 0, 0)),
            pl.BlockSpec((None, 1, WINDOW, KW), lambda b: (layer, b, 0, 0)),
        ],
        out_specs=pl.BlockSpec((1, R, QW), lambda b: (b, 0, 0)),
        out_shape=jax.ShapeDtypeStruct((B, R, QW), F32),
        compiler_params=pltpu.CompilerParams(dimension_semantics=("parallel",)),
        name="swa_sample",
    )(sinks, qkv, ck, cv)


def _rope_tables(pos):
    half = HEAD_DIM // 2
    inv = jnp.power(ROPE_THETA, -jnp.arange(half, dtype=F32) * (2.0 / HEAD_DIM))
    ang = pos.astype(F32)[:, None] * inv[None, :]
    cos = jnp.cos(ang)
    sin = jnp.sin(ang)
    reps = LANES // HEAD_DIM
    cos_t = jnp.tile(jnp.concatenate([cos, cos], axis=1), (1, reps))
    sin_t = jnp.tile(jnp.concatenate([-sin, sin], axis=1), (1, reps))
    return (jnp.stack([cos_t, jnp.ones_like(cos_t)]), jnp.stack([sin_t, jnp.zeros_like(sin_t)]))


def kernel(x_prompt, x_sample, state_hgrn, cache_k_win, cache_v_win, norm_gains, w_ffn_in, w_ffn_out,
           w_hgrn_in, hgrn_lb_logits, hgrn_norm_gain, w_hgrn_out, w_attn_in, attn_sinks, w_attn_out):
    B, L, D = x_prompt.shape
    SB, SL, _ = x_sample.shape
    depth = norm_gains.shape[0]
    TP, TS = B * L, SB * SL
    H = hgrn_norm_gain.shape[1] // HG_DV
    n_kv = cache_k_win.shape[3]
    n_q = w_attn_out.shape[1] // HEAD_DIM
    QW, KW = n_q * HEAD_DIM, n_kv * HEAD_DIM
    buf = cache_k_win.shape[2]

    x = (x_prompt.reshape(TP, D), x_sample.reshape(TS, D))

    lb_all = jnp.cumsum(jax.nn.softmax(hgrn_lb_logits.astype(F32), axis=0), axis=0)
    lb_all = lb_all - lb_all[0:1]

    pos = jnp.concatenate([jnp.tile(jnp.arange(L, dtype=jnp.int32), B),
                           jnp.tile(PAST_LEN + jnp.arange(SL, dtype=jnp.int32), SB)])
    rope = _rope_tables(pos)

    hg_p, kp_l, vp_l, ks_l, vs_l = [], [], [], [], []
    hg_s = None
    for layer in range(depth):
        g = norm_gains[layer]
        x = _ffn_half(x, g[0:1], g[1:2], w_ffn_in, w_ffn_out, layer, 0)
        if layer % N_MIXERS == 0:
            a = layer // N_MIXERS
            lb = lb_all[a][None, :]
            lbp = (jnp.log(lb), jnp.log1p(-lb), 1.0 - lb)
            ng = hgrn_norm_gain[a][None, :].astype(F32)
            o_p, s_p = _hgrn_fused(x, g[2:3], w_hgrn_in, a, B, L, lbp, ng)
            o_p = o_p.reshape(TP, -1)
            xs = jnp.pad(x[TP:].reshape(SB, SL, D), ((0, 0), (0, SAMPLE_PAD - SL), (0, 0)))
            p_s = _proj(xs.reshape(SB * SAMPLE_PAD, D), g[2:3], w_hgrn_in, a, (SB * SAMPLE_PAD, HGRN_SAMPLE_PROJ_TN))
            o_s, hg_s = _hgrn_scan(p_s, SB, SAMPLE_PAD, SAMPLE_PAD, H, SL,
                                   lbp, ng, (state_hgrn, a), F32, prev=hg_s)
            o_s = o_s.reshape(SB, SAMPLE_PAD, -1)[:, :SL].reshape(TS, -1).astype(BF16)
            x = _outproj(o_p, o_s, w_hgrn_out, x, g[3:4], a)
            hg_p.append(s_p.astype(state_hgrn.dtype))
        else:
            bb = layer // N_MIXERS
            p = _proj(x, g[2:3], w_attn_in, bb, ATTN_PROJ_TILES, rope=rope, n_rope_cols=QW + KW)
            sinks = attn_sinks[bb].astype(F32)
            o_p = _attn_prompt(p, sinks, B, L, n_q, n_kv)
            pn = p[TP:].reshape(SB, SL, -1)
            ck = cache_k_win.reshape(-1, SB, buf, KW)
            cv = cache_v_win.reshape(-1, SB, buf, KW)
            pn_rows = jnp.pad(pn, ((0, 0), (0, -SL % SUBLANES), (0, 0)))
            o_s = _attn_sample(pn_rows, ck, cv, bb, sinks, n_q, n_kv, SL)
            x = _outproj(o_p, o_s[:, :SL].reshape(TS, QW).astype(BF16), w_attn_out, x, g[3:4], bb)
            tails = [p[(s + 1) * L - buf:(s + 1) * L, QW:] for s in range(B)]
            kp_l.append(jnp.stack([t[:, :KW] for t in tails]).reshape(B, buf, n_kv, HEAD_DIM))
            vp_l.append(jnp.stack([t[:, KW:] for t in tails]).reshape(B, buf, n_kv, HEAD_DIM))
            k_new = pn[:, :, QW:QW + KW].reshape(SB, SL, n_kv, HEAD_DIM)
            v_new = pn[:, :, QW + KW:].reshape(SB, SL, n_kv, HEAD_DIM)
            ks_l.append(k_new)
            vs_l.append(v_new)
        x = _ffn_half(x, g[4:5], g[5:6], w_ffn_in, w_ffn_out, layer, 1,
                      out_split=TP if layer == depth - 1 else None)

    yp = x[0].reshape(B, L, D)
    ys = x[1].reshape(SB, SL, D)
    k_win_s = jnp.concatenate([cache_k_win, jnp.stack(ks_l).astype(cache_k_win.dtype)], axis=2)[:, :, -buf:]
    v_win_s = jnp.concatenate([cache_v_win, jnp.stack(vs_l).astype(cache_v_win.dtype)], axis=2)[:, :, -buf:]
    return (yp, ys, jnp.stack(hg_p), hg_s.astype(state_hgrn.dtype), jnp.stack(kp_l), jnp.stack(vp_l),
            k_win_s, v_win_s)
```
